```python
import jax
import jax.numpy as jnp
from jax import lax
import numpy as np

D_MODEL = 2048
BATCH = 2
SEQ = 8192
DEPTH = 2

CTX_LEN = 256
GRID_W = 64

A_HEADS = 8
A_DV = D_MODEL // A_HEADS
A_DK = A_DV // 2
A_CHUNK = 64
F_BIAS = 3.0
B_WIDTH = D_MODEL
CONV_W = 3
C_HEADS = 16
C_DH = D_MODEL // C_HEADS
WIN_ROWS = 8
WIN_COLS = 16
PEER_HEADS = 8
PEER_NKEYS = 128
PEER_N = PEER_NKEYS * PEER_NKEYS
PEER_QDIM = 256
PEER_HALF = PEER_QDIM // 2
PEER_TOPK = 16
PEER_BLOCK = 128
N_BRANCH = 3
ROPE_BASE = 10000.0
EPS = 1e-6

PROJ_LAYOUT = (
    ('a_k', A_HEADS * A_DK),
    ('a_v', A_HEADS * A_DV),
    ('a_gates', 4 * A_HEADS),
    ('c_k', C_HEADS * C_DH),
    ('c_v', C_HEADS * C_DH),
    ('a_q', A_HEADS * A_DK),
    ('a_o', A_HEADS * A_DV),
    ('b_b', B_WIDTH),
    ('b_c', B_WIDTH),
    ('b_x', B_WIDTH),
    ('c_q', C_HEADS * C_DH),
    ('gate', N_BRANCH * D_MODEL),
)
KV_WIDTH = A_HEADS * A_DK + A_HEADS * A_DV + 4 * A_HEADS + 2 * C_HEADS * C_DH
PROJ_WIDTH = KV_WIDTH + A_HEADS * A_DK + A_HEADS * A_DV + 3 * B_WIDTH + C_HEADS * C_DH + N_BRANCH * D_MODEL

kernel_name = 'hybrid_mlstm_shortconv_natten_peer'


def split_cols(p):
    out, off = {}, 0
    for name, w in PROJ_LAYOUT:
        if off + w > p.shape[-1]:
            break
        out[name] = p[..., off:off + w]
        off += w
    return out


def rms_norm(t, g):
    tf = t.astype(jnp.float32)
    y = tf * lax.rsqrt(jnp.mean(tf * tf, axis=-1, keepdims=True) + EPS)
    return (y * g.astype(jnp.float32)).astype(t.dtype)


def modulate(t, shift, scale):
    return t * (1 + scale) + shift


def to_heads(t, n):
    b, s, _ = t.shape
    return t.reshape(b, s, n, -1).transpose(0, 2, 1, 3)


def to_c_heads(t):
    return t.reshape(t.shape[0], t.shape[1], C_HEADS, C_DH)


def flip_seq(t):
    return jnp.flip(t, axis=2)


def rope2d(t):
    s = t.shape[2]
    half = t.shape[-1] // 2
    nf = half // 2
    inv = ROPE_BASE ** (-jnp.arange(nf, dtype=jnp.float32) / nf)
    pos = jnp.arange(s)

    def rot(u, p):
        ang = p.astype(jnp.float32)[:, None] * inv
        cos, sin = jnp.cos(ang), jnp.sin(ang)
        u1, u2 = u[..., :nf], u[..., nf:]
        return jnp.concatenate([u1 * cos - u2 * sin, u2 * cos + u1 * sin], axis=-1)

    tf = t.astype(jnp.float32)
    out = jnp.concatenate([rot(tf[..., :half], pos // GRID_W), rot(tf[..., half:], pos % GRID_W)], axis=-1)
    return out.astype(t.dtype)


def mlstm_gates(gp, bias):
    b, s, _ = gp.shape
    g = gp.astype(jnp.float32).reshape(b, s, 4, A_HEADS) + bias.astype(jnp.float32)
    g = g.transpose(2, 0, 3, 1)
    return (g[0], jax.nn.log_sigmoid(g[1]), g[2], jax.nn.log_sigmoid(g[3]))


def mlstm_final_state(k, v, li, lf):
    kf, vf = k.astype(jnp.float32), v.astype(jnp.float32)
    cs = jnp.cumsum(lf, axis=-1)
    w = cs[..., -1:] - cs + li
    m = jnp.max(w, axis=-1)
    ke = kf * jnp.exp(w - m[..., None])[..., None]
    big_c = jnp.einsum('bhsd,bhse->bhde', ke, vf)
    n = jnp.sum(ke, axis=2)
    return (big_c, n, m)


def zero_state(bsz):
    return (jnp.zeros((bsz, A_HEADS, A_DK, A_DV), jnp.float32),
            jnp.zeros((bsz, A_HEADS, A_DK), jnp.float32),
            jnp.zeros((bsz, A_HEADS), jnp.float32))


def mlstm_chunkwise(q, k, v, li, lf, state):
    b, h, s, _ = q.shape
    nc = s // A_CHUNK

    def chunks(t):
        t = t.reshape(b, h, nc, A_CHUNK, *t.shape[3:])
        return jnp.moveaxis(t, 2, 0)

    tril = jnp.tril(jnp.ones((A_CHUNK, A_CHUNK), dtype=bool))

    def step(carry, xs):
        big_c, n, m = carry
        qc, kc, vc, lic, lfc = xs
        bc = jnp.cumsum(lfc, axis=-1)
        dlog = jnp.where(tril, bc[..., :, None] - bc[..., None, :] + lic[..., None, :], -jnp.inf)
        inter = bc + m[..., None]
        m_row = jnp.maximum(inter, jnp.max(dlog, axis=-1))
        w_inter = jnp.exp(inter - m_row)
        sc = jnp.einsum('bhtd,bhsd->bhts', qc, kc) * jnp.exp(dlog - m_row[..., None])
        num = w_inter[..., None] * jnp.einsum('bhtd,bhde->bhte', qc, big_c) + jnp.einsum('bhts,bhse->bhte', sc, vc)
        den = w_inter * jnp.einsum('bhtd,bhd->bht', qc, n) + jnp.sum(sc, axis=-1)
        hc = num / jnp.maximum(jnp.abs(den), jnp.exp(-m_row))[..., None]
        b_last = bc[..., -1]
        w_s = b_last[..., None] - bc + lic
        m_new = jnp.maximum(b_last + m, jnp.max(w_s, axis=-1))
        decay = jnp.exp(b_last + m - m_new)
        ke = kc * jnp.exp(w_s - m_new[..., None])[..., None]
        c_new = decay[..., None, None] * big_c + jnp.einsum('bhsd,bhse->bhde', ke, vc)
        n_new = decay[..., None] * n + jnp.sum(ke, axis=2)
        return (c_new, n_new, m_new), hc

    state, hs = lax.scan(step, state, (chunks(q), chunks(k), chunks(v), chunks(li), chunks(lf)))
    return jnp.moveaxis(hs, 0, 2).reshape(b, h, s, -1), state


def mlstm_mixer(q, k, v, o, gates, state_f, state_b, hnorm_g):
    q = q.astype(jnp.float32) * (A_DK ** -0.5)
    k = k.astype(jnp.float32)
    v = v.astype(jnp.float32)
    li_f, lf_f, li_b, lf_b = gates
    h_f, _ = mlstm_chunkwise(q, k, v, li_f, lf_f, state_f)
    h_b, _ = mlstm_chunkwise(flip_seq(q), flip_seq(k), flip_seq(v), flip_seq(li_b), flip_seq(lf_b), state_b)
    hs = h_f + flip_seq(h_b)
    hs = hs * lax.rsqrt(jnp.mean(hs * hs, axis=-1, keepdims=True) + EPS) * hnorm_g.astype(jnp.float32).reshape(A_HEADS, 1, A_DV)
    bsz, _, s, _ = hs.shape
    hs = hs.transpose(0, 2, 1, 3).reshape(bsz, s, A_HEADS * A_DV)
    return (jax.nn.sigmoid(o.astype(jnp.float32)) * hs).astype(o.dtype)


def short_conv_mixer(bg, cg, xin, w):
    u = cg * xin
    s = u.shape[1]
    pad = CONV_W // 2
    up = jnp.pad(u, ((0, 0), (pad, pad), (0, 0)))
    y = up[:, 0:s] * w[0]
    for j in range(1, CONV_W):
        y = y + up[:, j:j + s] * w[j]
    return bg * y


def neighbourhood_attention(q, k, v, k_ctx, v_ctx, rpb):
    bsz, s, nh, dh = q.shape
    rows = s // GRID_W
    wr = min(WIN_ROWS, rows)
    nloc = wr * WIN_COLS
    qg = (q.reshape(bsz, rows, GRID_W, nh, dh) * (dh ** -0.5)).transpose(1, 0, 2, 3, 4)
    kg = k.reshape(bsz, rows, GRID_W, nh, dh)
    vg = v.reshape(bsz, rows, GRID_W, nh, dh)
    col = np.arange(GRID_W)
    col0 = np.clip(col - WIN_COLS // 2, 0, GRID_W - WIN_COLS)
    col_idx = col0[:, None] + np.arange(WIN_COLS)
    dc_idx = col_idx - col[:, None] + (WIN_COLS - 1)

    def row_step(args):
        r, q_row = args
        r0 = jnp.clip(r - wr // 2, 0, rows - wr)
        k_win = lax.dynamic_slice_in_dim(kg, r0, wr, axis=1)[:, :, col_idx]
        v_win = lax.dynamic_slice_in_dim(vg, r0, wr, axis=1)[:, :, col_idx]
        dr_idx = r0 + jnp.arange(wr) - r + (WIN_ROWS - 1)
        bias = rpb[:, dr_idx[None, :, None], dc_idx[:, None, :]]
        s_loc = jnp.einsum('bqhd,brqwhd->bhqrw', q_row, k_win).astype(jnp.float32) + bias.astype(jnp.float32)
        s_ctx = jnp.einsum('bqhd,bchd->bhqc', q_row, k_ctx).astype(jnp.float32)
        sc = jnp.concatenate([s_loc.reshape(bsz, nh, GRID_W, nloc), s_ctx], axis=-1)
        p = jax.nn.softmax(sc, axis=-1).astype(v.dtype)
        p_loc = p[..., :nloc].reshape(bsz, nh, GRID_W, wr, WIN_COLS)
        p_ctx = p[..., nloc:]
        return jnp.einsum('bhqrw,brqwhd->bqhd', p_loc, v_win) + jnp.einsum('bhqc,bchd->bqhd', p_ctx, v_ctx)

    out = lax.map(row_step, (jnp.arange(rows), qg))
    return out.transpose(1, 0, 2, 3, 4).reshape(bsz, s, nh * dh)


def context_attention(q, k, v):
    sc = jnp.einsum('bqhd,bkhd->bhqk', q, k).astype(jnp.float32) * (q.shape[-1] ** -0.5)
    p = jax.nn.softmax(sc, axis=-1).astype(v.dtype)
    return jnp.einsum('bhqk,bkhd->bqhd', p, v).reshape(q.shape[0], q.shape[1], -1)


def merge_branches(gate_pre, ya, yb, yc, w_a, w_b, w_c, w_o):
    b, s, _ = gate_pre.shape
    g = jax.nn.sigmoid(gate_pre.astype(jnp.float32)).astype(ya.dtype).reshape(b, s, N_BRANCH, D_MODEL)
    y = g[:, :, 0] * (ya @ w_a) + g[:, :, 1] * (yb @ w_b) + g[:, :, 2] * (yc @ w_c)
    return y @ w_o


def peer_ffn(xn, wq, sub_keys, u, v):
    b, t, d = xn.shape
    xb = xn.reshape(-1, PEER_BLOCK, d)

    def block(xt):
        q = (xt @ wq).reshape(PEER_BLOCK, PEER_HEADS, 2, PEER_HALF)
        s = jnp.einsum('thpd,hpkd->thpk', q, sub_keys).astype(jnp.float32)
        s_top, i_top = lax.top_k(s, PEER_TOPK)
        cand = (s_top[:, :, 0, :, None] + s_top[:, :, 1, None, :]).reshape(PEER_BLOCK, PEER_HEADS, -1)
        cand_idx = (i_top[:, :, 0, :, None] * PEER_NKEYS + i_top[:, :, 1, None, :]).reshape(PEER_BLOCK, PEER_HEADS, -1)
        best, pos = lax.top_k(cand, PEER_TOPK)
        idx = jnp.take_along_axis(cand_idx, pos, axis=-1)
        g = jax.nn.softmax(best, axis=-1)
        act = jax.nn.gelu(jnp.einsum('td,thkd->thk', xt, u[idx]).astype(jnp.float32), approximate=False)
        return jnp.einsum('thk,thkd->td', (g * act).astype(xt.dtype), v[idx])

    return lax.map(block, xb).reshape(b, t, d)


def setup_inputs(seed: int = 0) -> dict:
    key = jax.random.key(seed)
    ks = jax.random.split(key, 21)
    nl, d = DEPTH, D_MODEL

    def nrm(k, shape, scale):
        return scale * jax.random.normal(k, shape, jnp.float32)

    gate_off = jnp.array([0.0, F_BIAS, 0.0, F_BIAS], jnp.float32)[None, :, None]
    return {
        'x': nrm(ks[0], (BATCH, SEQ, d), 1.0),
        'c': nrm(ks[1], (BATCH, d), 1.0),
        'ctx': nrm(ks[2], (BATCH, CTX_LEN, d), 1.0),
        'c_ctx': nrm(ks[3], (d,), 1.0),
        'w_ada': nrm(ks[4], (nl, d, 6 * d), 0.5 * d ** -0.5),
        'b_ada': nrm(ks[5], (nl, 6 * d), 0.02),
        'norm_g': 1.0 + nrm(ks[6], (nl, 2, d), 0.05),
        'w_in': nrm(ks[7], (nl, d, PROJ_WIDTH), d ** -0.5),
        'a_gate_b': gate_off + nrm(ks[8], (nl, 4, A_HEADS), 0.1),
        'a_hnorm_g': 1.0 + nrm(ks[9], (nl, A_HEADS * A_DV), 0.05),
        'b_conv': nrm(ks[10], (nl, CONV_W, B_WIDTH), CONV_W ** -0.5),
        'c_qk_g': 1.0 + nrm(ks[11], (nl, 2, C_DH), 0.05),
        'c_rpb': nrm(ks[12], (nl, C_HEADS, 2 * WIN_ROWS - 1, 2 * WIN_COLS - 1), 0.1),
        'w_a_out': nrm(ks[13], (nl, A_HEADS * A_DV, d), (A_HEADS * A_DV) ** -0.5),
        'w_b_out': nrm(ks[14], (nl, B_WIDTH, d), B_WIDTH ** -0.5),
        'w_c_out': nrm(ks[15], (nl, C_HEADS * C_DH, d), (C_HEADS * C_DH) ** -0.5),
        'w_out': nrm(ks[16], (nl, d, d), d ** -0.5),
        'peer_wq': nrm(ks[17], (nl, d, PEER_HEADS * PEER_QDIM), d ** -0.5),
        'peer_keys': nrm(ks[18], (nl, PEER_HEADS, 2, PEER_NKEYS, PEER_HALF), PEER_HALF ** -0.5),
        'peer_u': nrm(ks[19], (nl, PEER_N, d), d ** -0.5),
        'peer_v': nrm(ks[20], (nl, PEER_N, d), PEER_HEADS ** -0.5),
    }


def reference(x, c, ctx, c_ctx, w_ada, b_ada, norm_g, w_in, a_gate_b, a_hnorm_g, b_conv, c_qk_g, c_rpb,
              w_a_out, w_b_out, w_c_out, w_out, peer_wq, peer_keys, peer_u, peer_v):
    bsz = x.shape[0]
    h_ctx = ctx
    for l in range(DEPTH):
        last = l == DEPTH - 1
        mod_x = jnp.split((jax.nn.silu(c) @ w_ada[l] + b_ada[l])[:, None, :], 6, axis=-1)
        mod_c = jnp.split(jax.nn.silu(c_ctx) @ w_ada[l] + b_ada[l], 6, axis=-1)

        cn = modulate(rms_norm(h_ctx, norm_g[l, 0]), mod_c[0], mod_c[1])
        pc = split_cols(cn @ (w_in[l, :, :KV_WIDTH] if last else w_in[l]))
        ka_c = to_heads(pc['a_k'], A_HEADS)
        va_c = to_heads(pc['a_v'], A_HEADS)
        gates_c = mlstm_gates(pc['a_gates'], a_gate_b[l])
        state_f = mlstm_final_state(ka_c, va_c, gates_c[0], gates_c[1])
        state_b = mlstm_final_state(flip_seq(ka_c), flip_seq(va_c), flip_seq(gates_c[2]), flip_seq(gates_c[3]))
        kc_c = rms_norm(to_c_heads(pc['c_k']), c_qk_g[l, 1])
        vc_c = to_c_heads(pc['c_v'])

        xn = modulate(rms_norm(x, norm_g[l, 0]), mod_x[0], mod_x[1])
        p = split_cols(xn @ w_in[l])
        ya = mlstm_mixer(rope2d(to_heads(p['a_q'], A_HEADS)), rope2d(to_heads(p['a_k'], A_HEADS)),
                         to_heads(p['a_v'], A_HEADS), p['a_o'], mlstm_gates(p['a_gates'], a_gate_b[l]),
                         state_f, state_b, a_hnorm_g[l])
        yb = short_conv_mixer(p['b_b'], p['b_c'], p['b_x'], b_conv[l])
        yc = neighbourhood_attention(rms_norm(to_c_heads(p['c_q']), c_qk_g[l, 0]),
                                     rms_norm(to_c_heads(p['c_k']), c_qk_g[l, 1]),
                                     to_c_heads(p['c_v']), kc_c, vc_c, c_rpb[l])
        mix = merge_branches(p['gate'], ya, yb, yc, w_a_out[l], w_b_out[l], w_c_out[l], w_out[l])
        x_new = x + mod_x[2] * mix
        xn2 = modulate(rms_norm(x_new, norm_g[l, 1]), mod_x[3], mod_x[4])
        x_new = x_new + mod_x[5] * peer_ffn(xn2, peer_wq[l], peer_keys[l], peer_u[l], peer_v[l])

        if not last:
            ya_c = mlstm_mixer(to_heads(pc['a_q'], A_HEADS), ka_c, va_c, pc['a_o'], gates_c,
                               zero_state(bsz), zero_state(bsz), a_hnorm_g[l])
            yb_c = short_conv_mixer(pc['b_b'], pc['b_c'], pc['b_x'], b_conv[l])
            yc_c = context_attention(rms_norm(to_c_heads(pc['c_q']), c_qk_g[l, 0]), kc_c, vc_c)
            mix_c = merge_branches(pc['gate'], ya_c, yb_c, yc_c, w_a_out[l], w_b_out[l], w_c_out[l], w_out[l])
            hc = h_ctx + mod_c[2] * mix_c
            hcn = modulate(rms_norm(hc, norm_g[l, 1]), mod_c[3], mod_c[4])
            h_ctx = hc + mod_c[5] * peer_ffn(hcn, peer_wq[l], peer_keys[l], peer_u[l], peer_v[l])
        x = x_new
    return x
```

```python
import functools

import jax
import jax.numpy as jnp
import numpy as np
from jax import lax
from jax.experimental import pallas as pl
from jax.experimental.pallas import tpu as pltpu

F32 = jnp.float32
BF16 = jnp.bfloat16
HIGHEST = lax.Precision.HIGHEST

GRID_W = 64
ROPE_BASE = 10000.0
EPS = 1e-6
PEER_TOPK = 16
A_DK = 128
A_DV = 256
C_DH = 128
SCAN_CHUNK = 256
NA_QROWS = 8
NA_KROWS = 16
NEG = -1e30

VMEM_LIMIT = 56 * 1024 * 1024
LANE = 128


def _cp(*sem):
    return pltpu.CompilerParams(dimension_semantics=sem, vmem_limit_bytes=VMEM_LIMIT)


def _sigmoid(x):
    return 1.0 / (1.0 + jnp.exp(-x))


def _log_sigmoid(x):
    return jnp.minimum(x, 0.0) - jnp.log1p(jnp.exp(-jnp.abs(x)))


def _pick(n, pref):
    t = min(n, pref)
    while n % t:
        t //= 2
    return t


def _adaln_kernel(c_ref, w_ref, b_ref, o_ref):
    cc = c_ref[...]
    a = cc * _sigmoid(cc)
    o_ref[...] = jnp.dot(a, w_ref[...], precision=HIGHEST, preferred_element_type=F32) + b_ref[...]


def adaln(cond, w_ada, b_ada):
    nl, d, n = w_ada.shape
    rows = cond.shape[0]
    tn = _pick(n, 1024)
    return pl.pallas_call(
        _adaln_kernel,
        grid=(nl, n // tn),
        in_specs=[
            pl.BlockSpec((rows, d), lambda l, j: (0, 0)),
            pl.BlockSpec((None, d, tn), lambda l, j: (l, 0, j)),
            pl.BlockSpec((None, 1, tn), lambda l, j: (l, 0, j)),
        ],
        out_specs=pl.BlockSpec((None, rows, tn), lambda l, j: (l, 0, j)),
        out_shape=jax.ShapeDtypeStruct((nl, rows, n), F32),
        compiler_params=_cp("parallel", "parallel"),
        name="adaln",
    )(cond, w_ada, b_ada.reshape(nl, 1, n))


def _norm_mod_kernel(x_ref, g_ref, sh_ref, sc_ref, o_ref):
    x = x_ref[...]
    y = x * lax.rsqrt(jnp.mean(x * x, axis=-1, keepdims=True) + EPS) * g_ref[...]
    o_ref[...] = (y * (1.0 + sc_ref[...]) + sh_ref[...]).astype(o_ref.dtype)


def norm_mod(x, g, shift, scale):
    b, s, d = x.shape
    tm = _pick(s, 512)
    vec = pl.BlockSpec((None, 1, d), lambda bi, i: (bi, 0, 0))
    return pl.pallas_call(
        _norm_mod_kernel,
        grid=(b, s // tm),
        in_specs=[
            pl.BlockSpec((None, tm, d), lambda bi, i: (bi, i, 0)),
            pl.BlockSpec((1, d), lambda bi, i: (0, 0)),
            vec, vec,
        ],
        out_specs=pl.BlockSpec((None, tm, d), lambda bi, i: (bi, i, 0)),
        out_shape=jax.ShapeDtypeStruct((b, s, d), BF16),
        compiler_params=_cp("parallel", "parallel"),
        name="norm_mod",
    )(x, g.reshape(1, d), shift.reshape(b, 1, d), scale.reshape(b, 1, d))


def _mm_kernel(a_ref, w_ref, o_ref):
    o_ref[...] = jnp.dot(a_ref[...], w_ref[...], preferred_element_type=F32)


def _mm_res_kernel(a_ref, w_ref, x_ref, g_ref, o_ref):
    acc = jnp.dot(a_ref[...], w_ref[...], preferred_element_type=F32)
    o_ref[...] = x_ref[...] + g_ref[...] * acc


def matmul(a, w, n_cols=None, residual=None):
    b, s, k = a.shape
    n = w.shape[1] if n_cols is None else n_cols
    tm = _pick(s, 1024)
    tn = _pick(n, 1024)
    in_specs = [
        pl.BlockSpec((None, tm, k), lambda bi, i, j: (bi, i, 0)),
        pl.BlockSpec((k, tn), lambda bi, i, j: (0, j)),
    ]
    args = [a, w]
    kern = _mm_kernel
    if residual is not None:
        x, gate = residual
        in_specs += [
            pl.BlockSpec((None, tm, tn), lambda bi, i, j: (bi, i, j)),
            pl.BlockSpec((None, 1, tn), lambda bi, i, j: (bi, 0, j)),
        ]
        args += [x, gate.reshape(b, 1, n)]
        kern = _mm_res_kernel
    return pl.pallas_call(
        kern,
        grid=(b, s // tm, n // tn),
        in_specs=in_specs,
        out_specs=pl.BlockSpec((None, tm, tn), lambda bi, i, j: (bi, i, j)),
        out_shape=jax.ShapeDtypeStruct((b, s, n), F32),
        compiler_params=_cp("parallel", "parallel", "arbitrary"),
        name="proj_res" if residual is not None else "proj",
    )(*args)


def _gates_kernel(x_ref, wf_ref, wi_ref, wft_ref, wit_ref, bfc_ref, bic_ref, bfr_ref, bir_ref,
                  gc_ref, gr_ref, *, heads):
    x = x_ref[...]
    ln = x.shape[0]
    r_i = lax.broadcasted_iota(jnp.int32, (ln, ln), 0)
    c_i = lax.broadcasted_iota(jnp.int32, (ln, ln), 1)
    tril = (c_i <= r_i).astype(F32)
    triu = (c_i >= r_i).astype(F32)

    gf = jnp.dot(x, wf_ref[...], preferred_element_type=F32) + bfc_ref[...]
    gi = jnp.dot(x, wi_ref[...], preferred_element_type=F32) + bic_ref[...]
    ls = _log_sigmoid(gf)
    pre = jnp.dot(tril, ls, precision=HIGHEST, preferred_element_type=F32)
    suf = jnp.dot(triu, ls, precision=HIGHEST, preferred_element_type=F32)
    lane = lax.broadcasted_iota(jnp.int32, gf.shape, 1)
    cum = jnp.where(lane < 2 * heads, pre, suf)
    is_a = (lane < heads) | ((lane >= 2 * heads) & (lane < 3 * heads))
    gc_ref[...] = jnp.where(is_a, cum, gi - cum)

    dn = (((1,), (1,)), ((), ()))
    gft = lax.dot_general(wft_ref[...], x, dn, preferred_element_type=F32) + bfr_ref[...]
    git = lax.dot_general(wit_ref[...], x, dn, preferred_element_type=F32) + bir_ref[...]
    lst = _log_sigmoid(gft)
    pre_t = jnp.dot(lst, triu, precision=HIGHEST, preferred_element_type=F32)
    suf_t = jnp.dot(lst, tril, precision=HIGHEST, preferred_element_type=F32)
    row = lax.broadcasted_iota(jnp.int32, gft.shape, 0)
    cum_t = jnp.where(row < 2 * heads, pre_t, suf_t)
    is_a_t = (row < heads) | ((row >= 2 * heads) & (row < 3 * heads))
    gr_ref[...] = jnp.where(is_a_t, cum_t, git - cum_t)


def gate_prep(xn, w_gates, gate_b):
    b, s, d = xn.shape
    heads = gate_b.shape[1]
    nrow = 4 * heads
    ln = SCAN_CHUNK
    wi_f, wf_f, wi_b, wf_b = [w_gates[:, g * heads:(g + 1) * heads] for g in range(4)]
    bi_f, bf_f, bi_b, bf_b = [gate_b[g] for g in range(4)]
    zw = jnp.zeros_like(wi_f)
    zb = jnp.zeros_like(bi_f)
    w_f = jnp.concatenate([wf_f, wf_f, wf_b, wf_b], axis=1)
    w_i = jnp.concatenate([zw, wi_f, zw, wi_b], axis=1)
    b_f = jnp.concatenate([bf_f, bf_f, bf_b, bf_b])
    b_i = jnp.concatenate([zb, bi_f, zb, bi_b])
    padc = LANE - nrow
    w_f_c = jnp.pad(w_f, ((0, 0), (0, padc))).astype(BF16)
    w_i_c = jnp.pad(w_i, ((0, 0), (0, padc))).astype(BF16)
    b_f_c = jnp.pad(b_f, (0, padc)).reshape(1, LANE)
    b_i_c = jnp.pad(b_i, (0, padc)).reshape(1, LANE)
    w_f_r = w_f.T.astype(BF16)
    w_i_r = w_i.T.astype(BF16)
    b_f_r = b_f.reshape(nrow, 1)
    b_i_r = b_i.reshape(nrow, 1)
    full = lambda shp: pl.BlockSpec(shp, lambda bi, i: (0, 0))
    return pl.pallas_call(
        functools.partial(_gates_kernel, heads=heads),
        grid=(b, s // ln),
        in_specs=[
            pl.BlockSpec((None, ln, d), lambda bi, i: (bi, i, 0)),
            full((d, LANE)), full((d, LANE)), full((nrow, d)), full((nrow, d)),
            full((1, LANE)), full((1, LANE)), full((nrow, 1)), full((nrow, 1)),
        ],
        out_specs=[
            pl.BlockSpec((None, ln, LANE), lambda bi, i: (bi, i, 0)),
            pl.BlockSpec((None, nrow, ln), lambda bi, i: (bi, 0, i)),
        ],
        out_shape=[
            jax.ShapeDtypeStruct((b, s, LANE), F32),
            jax.ShapeDtypeStruct((b, nrow, s), F32),
        ],
        compiler_params=_cp("parallel", "parallel"),
        name="gate_prep",
    )(xn, w_f_c, w_i_c, w_f_r, w_i_r, b_f_c, b_i_c, b_f_r, b_i_r)


def _rope(t, cos, sin):
    lane = lax.broadcasted_iota(jnp.int32, t.shape, 1)
    quarter = t.shape[1] // 4
    partner = jnp.where((lane & (2 * quarter - 1)) < quarter,
                        pltpu.roll(t, t.shape[1] - quarter, axis=1),
                        pltpu.roll(t, quarter, axis=1))
    return t * cos + partner * sin


def _mlstm_dir(q, k, v, a_col, r_col, r_row, b_last, mask, c_ref, m_ref, h_ref):
    ln = q.shape[0]
    dv = v.shape[1]
    m = m_ref[...][:, 0:1]
    v_ext = jnp.concatenate([v, jnp.ones((ln, LANE), F32)], axis=1).astype(BF16)
    dlog = jnp.where(mask, a_col + r_row, NEG)
    m_row = jnp.maximum(a_col + m, jnp.max(dlog, axis=1, keepdims=True))
    w_inter = jnp.exp(a_col + m - m_row)
    dmat = jnp.exp(dlog - m_row)
    s = lax.dot_general(q, k.astype(BF16), (((1,), (1,)), ((), ())), preferred_element_type=F32)
    sc = (s * dmat).astype(BF16)
    c_ext = c_ref[...]
    num = w_inter * jnp.dot(q, c_ext.astype(BF16), preferred_element_type=F32)
    num = num + jnp.dot(sc, v_ext, preferred_element_type=F32)
    den = num[:, dv:]
    denom = jnp.maximum(jnp.abs(den), jnp.exp(-m_row))
    inv = 1.0 / denom
    h_ref[...] = num[:, :dv] * jnp.concatenate([inv] * (dv // LANE), axis=1)
    r_max = jnp.max(r_row, axis=1, keepdims=True)
    m_new = b_last + jnp.maximum(m, r_max)
    decay = jnp.exp(b_last + m - m_new)
    ke = (k * jnp.exp(b_last + r_col - m_new)).astype(BF16)
    upd = lax.dot_general(ke, v_ext, (((0,), (0,)), ((), ())), preferred_element_type=F32)
    c_ref[...] = decay * c_ext + upd
    m_ref[...] = jnp.broadcast_to(m_new, m_ref.shape)


def _mlstm_kernel(qf_ref, kf_ref, vf_ref, gcf_ref, grf_ref, qb_ref, kb_ref, vb_ref, gcb_ref, grb_ref,
                  cosf_ref, sinf_ref, cosb_ref, sinb_ref, c0f_ref, m0f_ref, c0b_ref, m0b_ref,
                  hf_ref, hb_ref, cff_ref, mff_ref, cfb_ref, mfb_ref,
                  cf_s, mf_s, cb_s, mb_s, *, heads, use_rope):
    h = pl.program_id(1)
    i = pl.program_id(2)

    @pl.when(i == 0)
    def _():
        cf_s[...] = c0f_ref[...]
        mf_s[...] = m0f_ref[...]
        cb_s[...] = c0b_ref[...]
        mb_s[...] = m0b_ref[...]

    ln = qf_ref.shape[0]
    r_i = lax.broadcasted_iota(jnp.int32, (ln, ln), 0)
    c_i = lax.broadcasted_iota(jnp.int32, (ln, ln), 1)
    lane = lax.broadcasted_iota(jnp.int32, (ln, LANE), 1)
    rowi = lax.broadcasted_iota(jnp.int32, (4 * heads, ln), 0)
    scale = A_DK ** -0.5

    def col(g_ref, idx):
        return jnp.sum(jnp.where(lane == idx, g_ref[...], 0.0), axis=1, keepdims=True)

    def rowv(g_ref, idx):
        return jnp.sum(jnp.where(rowi == idx, g_ref[...], 0.0), axis=0, keepdims=True)

    def prep(q_ref, k_ref, cos_ref, sin_ref):
        q = q_ref[...]
        k = k_ref[...]
        if use_rope:
            q = _rope(q, cos_ref[...], sin_ref[...])
            k = _rope(k, cos_ref[...], sin_ref[...])
        return (q * scale).astype(BF16), k

    q, k = prep(qf_ref, kf_ref, cosf_ref, sinf_ref)
    a_col = col(gcf_ref, h)
    r_col = col(gcf_ref, heads + h)
    r_row = rowv(grf_ref, heads + h)
    b_last = a_col[ln - 1:ln, :]
    _mlstm_dir(q, k, vf_ref[...], a_col, r_col, r_row, b_last, c_i <= r_i, cf_s, mf_s, hf_ref)

    q, k = prep(qb_ref, kb_ref, cosb_ref, sinb_ref)
    a_col = col(gcb_ref, 2 * heads + h)
    r_col = col(gcb_ref, 3 * heads + h)
    r_row = rowv(grb_ref, 3 * heads + h)
    b_last = a_col[0:1, :]
    _mlstm_dir(q, k, vb_ref[...], a_col, r_col, r_row, b_last, c_i >= r_i, cb_s, mb_s, hb_ref)

    cff_ref[...] = cf_s[...]
    mff_ref[...] = mf_s[...]
    cfb_ref[...] = cb_s[...]
    mfb_ref[...] = mb_s[...]


def mlstm(p, offs, gc, gr, rope_tabs, state_f, state_b):
    b, s, _ = p.shape
    heads = gr.shape[1] // 4
    ln = SCAN_CHUNK
    nc = s // ln
    oq, ok, ov = [o // A_DK for o in offs[:2]] + [offs[2] // A_DV]
    use_rope = rope_tabs is not None
    if not use_rope:
        rope_tabs = (jnp.zeros((s, A_DK), F32),) * 2
    cos, sin = rope_tabs

    fwd = lambda i: i
    bwd = lambda i: nc - 1 - i

    def specs(ix):
        return [
            pl.BlockSpec((None, ln, A_DK), lambda bi, h, i: (bi, ix(i), oq + h)),
            pl.BlockSpec((None, ln, A_DK), lambda bi, h, i: (bi, ix(i), ok + h)),
            pl.BlockSpec((None, ln, A_DV), lambda bi, h, i: (bi, ix(i), ov + h)),
            pl.BlockSpec((None, ln, LANE), lambda bi, h, i: (bi, ix(i), 0)),
            pl.BlockSpec((None, 4 * heads, ln), lambda bi, h, i: (bi, 0, ix(i))),
        ]

    def tab(ix):
        return pl.BlockSpec((ln, A_DK), lambda bi, h, i: (ix(i), 0))

    dce = A_DV + LANE
    c_spec = pl.BlockSpec((None, None, A_DK, dce), lambda bi, h, i: (bi, h, 0, 0))
    m_spec = pl.BlockSpec((None, None, 1, LANE), lambda bi, h, i: (bi, h, 0, 0))
    outs = pl.pallas_call(
        functools.partial(_mlstm_kernel, heads=heads, use_rope=use_rope),
        grid=(b, heads, nc),
        in_specs=specs(fwd) + specs(bwd) + [tab(fwd), tab(fwd), tab(bwd), tab(bwd),
                                            c_spec, m_spec, c_spec, m_spec],
        out_specs=[
            pl.BlockSpec((None, ln, A_DV), lambda bi, h, i: (bi, i, h)),
            pl.BlockSpec((None, ln, A_DV), lambda bi, h, i: (bi, nc - 1 - i, h)),
            c_spec, m_spec, c_spec, m_spec,
        ],
        out_shape=[
            jax.ShapeDtypeStruct((b, s, heads * A_DV), F32),
            jax.ShapeDtypeStruct((b, s, heads * A_DV), F32),
            jax.ShapeDtypeStruct((b, heads, A_DK, dce), F32),
            jax.ShapeDtypeStruct((b, heads, 1, LANE), F32),
            jax.ShapeDtypeStruct((b, heads, A_DK, dce), F32),
            jax.ShapeDtypeStruct((b, heads, 1, LANE), F32),
        ],
        scratch_shapes=[
            pltpu.VMEM((A_DK, dce), F32), pltpu.VMEM((1, LANE), F32),
            pltpu.VMEM((A_DK, dce), F32), pltpu.VMEM((1, LANE), F32),
        ],
        compiler_params=_cp("parallel", "parallel", "arbitrary"),
        name="mlstm",
    )(p, p, p, gc, gr, p, p, p, gc, gr, cos, sin, cos, sin,
      state_f[0], state_f[1], state_b[0], state_b[1])
    h_f, h_b, cf, mf, cb, mb = outs
    return h_f, h_b, (cf, mf), (cb, mb)


def _mlstm_out_kernel(hf_ref, hb_ref, o_ref, g_ref, y_ref):
    hs = hf_ref[...] + hb_ref[...]
    hs = hs * lax.rsqrt(jnp.mean(hs * hs, axis=-1, keepdims=True) + EPS) * g_ref[...]
    y_ref[...] = (_sigmoid(o_ref[...]) * hs).astype(y_ref.dtype)


def mlstm_out(h_f, h_b, p, off_o, hnorm_g):
    b, s, w = h_f.shape
    heads = w // A_DV
    tm = _pick(s, 1024)
    oo = off_o // A_DV
    blk = lambda off: pl.BlockSpec((None, tm, A_DV), lambda bi, i, h: (bi, i, off + h))
    return pl.pallas_call(
        _mlstm_out_kernel,
        grid=(b, s // tm, heads),
        in_specs=[blk(0), blk(0), blk(oo), pl.BlockSpec((1, A_DV), lambda bi, i, h: (0, h))],
        out_specs=blk(0),
        out_shape=jax.ShapeDtypeStruct((b, s, w), BF16),
        compiler_params=_cp("parallel", "parallel", "parallel"),
        name="mlstm_out",
    )(h_f, h_b, p, hnorm_g.reshape(1, w))


def _conv_kernel(bb_ref, bc_ref, bx_ref, pc_ref, px_ref, nc_ref, nx_ref, w_ref, y_ref):
    i = pl.program_id(1)
    last = pl.num_programs(1) - 1
    u = bc_ref[...] * bx_ref[...]
    tm = u.shape[0]
    hr = pc_ref.shape[0]
    u_prev = pc_ref[...][hr - 1:hr, :] * px_ref[...][hr - 1:hr, :]
    u_next = nc_ref[...][0:1, :] * nx_ref[...][0:1, :]
    u_prev = jnp.where(i == 0, 0.0, u_prev)
    u_next = jnp.where(i == last, 0.0, u_next)
    row = lax.broadcasted_iota(jnp.int32, u.shape, 0)
    dn = jnp.where(row == 0, u_prev, pltpu.roll(u, 1, axis=0))
    up = jnp.where(row == tm - 1, u_next, pltpu.roll(u, tm - 1, axis=0))
    w = w_ref[...]
    y = dn * w[0:1, :] + u * w[1:2, :] + up * w[2:3, :]
    y_ref[...] = (bb_ref[...] * y).astype(y_ref.dtype)


def short_conv(p, off_b, off_c, off_x, w_conv):
    b, s, _ = p.shape
    kw, width = w_conv.shape
    tn = _pick(width, 512)
    tm = _pick(s, 512)
    hr = 8
    nhb = s // hr
    ob, oc, ox = off_b // tn, off_c // tn, off_x // tn
    main = lambda o: pl.BlockSpec((None, tm, tn), lambda bi, i, j: (bi, i, o + j))
    prev = lambda o: pl.BlockSpec(
        (None, hr, tn), lambda bi, i, j: (bi, jnp.maximum(i * (tm // hr) - 1, 0), o + j))
    nxt = lambda o: pl.BlockSpec(
        (None, hr, tn), lambda bi, i, j: (bi, jnp.minimum((i + 1) * (tm // hr), nhb - 1), o + j))
    return pl.pallas_call(
        _conv_kernel,
        grid=(b, s // tm, width // tn),
        in_specs=[main(ob), main(oc), main(ox), prev(oc), prev(ox), nxt(oc), nxt(ox),
                  pl.BlockSpec((kw, tn), lambda bi, i, j: (0, j))],
        out_specs=pl.BlockSpec((None, tm, tn), lambda bi, i, j: (bi, i, j)),
        out_shape=jax.ShapeDtypeStruct((b, s, width), BF16),
        compiler_params=_cp("parallel", "parallel", "parallel"),
        name="short_conv",
    )(p, p, p, p, p, p, p, w_conv)


def _rms(t, g):
    return t * lax.rsqrt(jnp.mean(t * t, axis=-1, keepdims=True) + EPS) * g


def _natten_kernel(q_ref, k_ref, v_ref, kc_ref, vc_ref, bias_ref, gq_ref, gk_ref, o_ref, *, rows):
    j = pl.program_id(2)
    nk = NA_KROWS * GRID_W
    w0 = jnp.clip(NA_QROWS * j - (NA_KROWS - NA_QROWS) // 2, 0, rows - NA_KROWS)
    start = pl.multiple_of(w0 * GRID_W, GRID_W * 4)
    gk = gk_ref[...]
    qn = (_rms(q_ref[...], gq_ref[...]) * (C_DH ** -0.5)).astype(BF16)
    kw = _rms(k_ref[pl.ds(start, nk), :], gk).astype(BF16)
    vw = v_ref[pl.ds(start, nk), :].astype(BF16)
    kc = _rms(kc_ref[...], gk).astype(BF16)
    vc = vc_ref[...].astype(BF16)
    dn = (((1,), (1,)), ((), ()))
    s_loc = lax.dot_general(qn, kw, dn, preferred_element_type=F32) + bias_ref[...]
    s_ctx = lax.dot_general(qn, kc, dn, preferred_element_type=F32)
    m = jnp.maximum(jnp.max(s_loc, axis=1, keepdims=True), jnp.max(s_ctx, axis=1, keepdims=True))
    p_loc = jnp.exp(s_loc - m)
    p_ctx = jnp.exp(s_ctx - m)
    l = jnp.sum(p_loc, axis=1, keepdims=True) + jnp.sum(p_ctx, axis=1, keepdims=True)
    o = jnp.dot(p_loc.astype(BF16), vw, preferred_element_type=F32)
    o = o + jnp.dot(p_ctx.astype(BF16), vc, preferred_element_type=F32)
    o_ref[...] = (o / l).astype(o_ref.dtype)


def _natten_bias(rpb, rows):
    heads, nr, ncb = rpb.shape
    win_r, win_c = (nr + 1) // 2, (ncb + 1) // 2
    qr = np.arange(NA_QROWS)[:, None, None, None]
    qc = np.arange(GRID_W)[None, :, None, None]
    kr = np.arange(NA_KROWS)[None, None, :, None]
    kc = np.arange(GRID_W)[None, None, None, :]
    col0 = np.clip(qc - win_c // 2, 0, GRID_W - win_c)
    col_ok = (kc >= col0) & (kc < col0 + win_c)
    dc = np.clip(kc - qc + win_c - 1, 0, ncb - 1)
    nblk = rows // NA_QROWS
    out = []
    for jb in (0, 1, nblk - 1):
        w0 = int(np.clip(NA_QROWS * jb - (NA_KROWS - NA_QROWS) // 2, 0, rows - NA_KROWS))
        r_abs = NA_QROWS * jb + qr
        k_abs = w0 + kr
        r0 = np.clip(r_abs - win_r // 2, 0, rows - win_r)
        ok = (k_abs >= r0) & (k_abs < r0 + win_r) & col_ok
        dr = np.clip(k_abs - r_abs + win_r - 1, 0, nr - 1)
        dr_b, dc_b, ok_b = np.broadcast_arrays(dr, dc, ok)
        n_q, n_k = NA_QROWS * GRID_W, NA_KROWS * GRID_W
        bias = rpb[:, dr_b.reshape(n_q, n_k), dc_b.reshape(n_q, n_k)]
        out.append(jnp.where(jnp.asarray(ok_b.reshape(n_q, n_k)), bias.astype(F32), NEG))
    return jnp.stack(out)


def natten(p, pc, off_q, off_k, off_v, rpb, g_q, g_k):
    b, s, _ = p.shape
    ctx = pc.shape[1]
    heads = rpb.shape[0]
    rows = s // GRID_W
    nblk = rows // NA_QROWS
    nq, nk = NA_QROWS * GRID_W, NA_KROWS * GRID_W
    oq, ok, ov = off_q // C_DH, off_k // C_DH, off_v // C_DH
    bias = _natten_bias(rpb, rows)

    def variant(j):
        return jnp.where(j == 0, 0, jnp.where(j == nblk - 1, 2, 1))

    return pl.pallas_call(
        functools.partial(_natten_kernel, rows=rows),
        grid=(b, heads, nblk),
        in_specs=[
            pl.BlockSpec((None, nq, C_DH), lambda bi, h, j: (bi, j, oq + h)),
            pl.BlockSpec((None, s, C_DH), lambda bi, h, j: (bi, 0, ok + h)),
            pl.BlockSpec((None, s, C_DH), lambda bi, h, j: (bi, 0, ov + h)),
            pl.BlockSpec((None, ctx, C_DH), lambda bi, h, j: (bi, 0, ok + h)),
            pl.BlockSpec((None, ctx, C_DH), lambda bi, h, j: (bi, 0, ov + h)),
            pl.BlockSpec((None, None, nq, nk), lambda bi, h, j: (variant(j), h, 0, 0)),
            pl.BlockSpec((1, C_DH), lambda bi, h, j: (0, 0)),
            pl.BlockSpec((1, C_DH), lambda bi, h, j: (0, 0)),
        ],
        out_specs=pl.BlockSpec((None, nq, C_DH), lambda bi, h, j: (bi, j, h)),
        out_shape=jax.ShapeDtypeStruct((b, s, heads * C_DH), BF16),
        compiler_params=_cp("parallel", "parallel", "arbitrary"),
        name="natten",
    )(p, p, p, pc, pc, bias, g_q.reshape(1, C_DH), g_k.reshape(1, C_DH))


def _ctx_attn_kernel(q_ref, k_ref, v_ref, gq_ref, gk_ref, o_ref):
    qn = (_rms(q_ref[...], gq_ref[...]) * (C_DH ** -0.5)).astype(BF16)
    kn = _rms(k_ref[...], gk_ref[...]).astype(BF16)
    s = lax.dot_general(qn, kn, (((1,), (1,)), ((), ())), preferred_element_type=F32)
    m = jnp.max(s, axis=1, keepdims=True)
    pr = jnp.exp(s - m)
    l = jnp.sum(pr, axis=1, keepdims=True)
    o = jnp.dot(pr.astype(BF16), v_ref[...].astype(BF16), preferred_element_type=F32)
    o_ref[...] = (o / l).astype(o_ref.dtype)


def ctx_attn(pc, off_q, off_k, off_v, heads, g_q, g_k):
    b, ctx, _ = pc.shape
    blk = lambda o: pl.BlockSpec((None, ctx, C_DH), lambda bi, h: (bi, 0, o // C_DH + h))
    gsp = pl.BlockSpec((1, C_DH), lambda bi, h: (0, 0))
    return pl.pallas_call(
        _ctx_attn_kernel,
        grid=(b, heads),
        in_specs=[blk(off_q), blk(off_k), blk(off_v), gsp, gsp],
        out_specs=blk(0),
        out_shape=jax.ShapeDtypeStruct((b, ctx, heads * C_DH), BF16),
        compiler_params=_cp("parallel", "parallel"),
        name="ctx_attn",
    )(pc, pc, pc, g_q.reshape(1, C_DH), g_k.reshape(1, C_DH))


def _merge_kernel(ya_ref, yb_ref, yc_ref, wa_ref, wb_ref, wc_ref, ga_ref, gb_ref, gc_ref, o_ref):
    y = _sigmoid(ga_ref[...]) * jnp.dot(ya_ref[...], wa_ref[...], preferred_element_type=F32)
    y = y + _sigmoid(gb_ref[...]) * jnp.dot(yb_ref[...], wb_ref[...], preferred_element_type=F32)
    y = y + _sigmoid(gc_ref[...]) * jnp.dot(yc_ref[...], wc_ref[...], preferred_element_type=F32)
    o_ref[...] = y.astype(o_ref.dtype)


def merge(ya, yb, yc, w_a, w_b, w_c, p, off_gate):
    b, s, k = ya.shape
    d = w_a.shape[1]
    tm = _pick(s, 512)
    tn = _pick(d, 512)
    og = off_gate // tn
    nb = d // tn
    act = pl.BlockSpec((None, tm, k), lambda bi, i, j: (bi, i, 0))
    wsp = pl.BlockSpec((k, tn), lambda bi, i, j: (0, j))
    gate = lambda g: pl.BlockSpec((None, tm, tn), lambda bi, i, j: (bi, i, og + g * nb + j))
    return pl.pallas_call(
        _merge_kernel,
        grid=(b, s // tm, nb),
        in_specs=[act, act, act, wsp, wsp, wsp, gate(0), gate(1), gate(2)],
        out_specs=pl.BlockSpec((None, tm, tn), lambda bi, i, j: (bi, i, j)),
        out_shape=jax.ShapeDtypeStruct((b, s, d), BF16),
        compiler_params=_cp("parallel", "parallel", "arbitrary"),
        name="merge",
    )(ya, yb, yc, w_a, w_b, w_c, p, p, p)


def _top_values(cur, k):
    n = cur.shape[0]
    idx = lax.broadcasted_iota(jnp.int32, cur.shape, 0).astype(F32)
    vals = []
    for _ in range(k):
        mx = jnp.max(cur, axis=0, keepdims=True)
        vals.append(mx)
        first = jnp.min(jnp.where(cur == mx, idx, float(n)), axis=0, keepdims=True)
        cur = jnp.where(idx == first, -jnp.inf, cur)
    return vals


def _peer_select_kernel(q_ref, keys_ref, s0_ref, s1_ref, e0_ref, e1_ref, thr_ref, *, heads):
    dn = (((1,), (1,)), ((), ()))
    half = keys_ref.shape[2]
    for h in range(heads):
        sides = []
        for side in range(2):
            hp = 2 * h + side
            qh = q_ref[:, hp * half:(hp + 1) * half].astype(BF16)
            st = lax.dot_general(keys_ref[hp], qh, dn, preferred_element_type=F32)
            vals = _top_values(st, PEER_TOPK)
            sides.append((st, vals))
        (s0, v0), (s1, v1) = sides
        v1s = jnp.concatenate(v1, axis=0)
        cand = jnp.concatenate([v0[i] + v1s for i in range(PEER_TOPK)], axis=0)
        best = _top_values(cand, PEER_TOPK)
        thr = best[PEER_TOPK - 1]
        z = best[0] - best[0] + 1.0
        for kk in range(1, PEER_TOPK):
            z = z + jnp.exp(best[kk] - best[0])
        inv_z = 1.0 / z
        s0_ref[h] = s0
        s1_ref[h] = s1
        e0_ref[h] = jnp.where(s0 >= v0[PEER_TOPK - 1], jnp.exp(s0 - v0[0]), 0.0) * inv_z
        e1_ref[h] = jnp.where(s1 >= v1[PEER_TOPK - 1], jnp.exp(s1 - v1[0]), 0.0)
        thr_ref[h:h + 1, :] = thr


def peer_select(q, keys):
    b, s, _ = q.shape
    heads, _, nkeys, half = keys.shape
    tt = _pick(s, 256)
    keys_b = keys.reshape(heads * 2, nkeys, half).astype(BF16)
    big = pl.BlockSpec((None, heads, nkeys, tt), lambda bi, i: (bi, 0, 0, i))
    big_shape = jax.ShapeDtypeStruct((b, heads, nkeys, s), F32)
    return pl.pallas_call(
        functools.partial(_peer_select_kernel, heads=heads),
        grid=(b, s // tt),
        in_specs=[
            pl.BlockSpec((None, tt, q.shape[2]), lambda bi, i: (bi, i, 0)),
            pl.BlockSpec((heads * 2, nkeys, half), lambda bi, i: (0, 0, 0)),
        ],
        out_specs=[big, big, big, big, pl.BlockSpec((None, heads, tt), lambda bi, i: (bi, 0, i))],
        out_shape=[big_shape, big_shape, big_shape, big_shape,
                   jax.ShapeDtypeStruct((b, heads, s), F32)],
        compiler_params=_cp("parallel", "parallel"),
        name="peer_select",
    )(q, keys_b)


def _gelu(x):
    return 0.5 * x * (1.0 + lax.erf(x * (2.0 ** -0.5)))


def _peer_dense_kernel(x_ref, u_ref, v_ref, s0_ref, e0_ref, s1_ref, e1_ref, thr_ref, res_ref, g_ref,
                       o_ref, st_s, wg_s, acc_s, *, heads, a_blk):
    c = pl.program_id(2)
    nkeys = s1_ref.shape[1]
    tt = x_ref.shape[0]

    @pl.when(c == 0)
    def _():
        acc_s[...] = jnp.zeros_like(acc_s)

    st_s[...] = lax.dot_general(u_ref[...], x_ref[...], (((1,), (1,)), ((), ())),
                                preferred_element_type=F32)

    def lane_tile(tl, carry):
        ls = pl.ds(pl.multiple_of(tl * LANE, LANE), LANE)
        for a in range(a_blk):
            rows = slice(a * nkeys, (a + 1) * nkeys)
            w = jnp.zeros((nkeys, LANE), F32)
            for h in range(heads):
                s0r = s0_ref[h, a:a + 1, ls]
                e0r = e0_ref[h, a:a + 1, ls]
                sel = (s1_ref[h, :, ls] + s0r) >= thr_ref[h:h + 1, ls]
                w = w + jnp.where(sel, e1_ref[h, :, ls] * e0r, 0.0)
            wg_s[rows, ls] = (w * _gelu(st_s[rows, ls])).astype(wg_s.dtype)
        return carry

    lax.fori_loop(0, tt // LANE, lane_tile, 0)
    acc_s[...] += lax.dot_general(wg_s[...], v_ref[...], (((0,), (0,)), ((), ())),
                                  preferred_element_type=F32)

    @pl.when(c == pl.num_programs(2) - 1)
    def _():
        o_ref[...] = res_ref[...] + g_ref[...] * acc_s[...]


def peer_dense(xn, u_b, v_b, sel, res, gate):
    b, s, d = xn.shape
    s0t, s1t, e0t, e1t, thr = sel
    heads, nkeys = s0t.shape[1], s0t.shape[2]
    n_exp = u_b.shape[0]
    a_blk = 8
    ec = a_blk * nkeys
    tt = _pick(s, 512)
    tok = pl.BlockSpec((None, tt, d), lambda bi, i, c: (bi, i, 0))
    exp_blk = pl.BlockSpec((ec, d), lambda bi, i, c: (c, 0))
    a_sel = pl.BlockSpec((None, heads, a_blk, tt), lambda bi, i, c: (bi, 0, c, i))
    b_sel = pl.BlockSpec((None, heads, nkeys, tt), lambda bi, i, c: (bi, 0, 0, i))
    return pl.pallas_call(
        functools.partial(_peer_dense_kernel, heads=heads, a_blk=a_blk),
        grid=(b, s // tt, n_exp // ec),
        in_specs=[tok, exp_blk, exp_blk, a_sel, a_sel, b_sel, b_sel,
                  pl.BlockSpec((None, heads, tt), lambda bi, i, c: (bi, 0, i)),
                  tok, pl.BlockSpec((None, 1, d), lambda bi, i, c: (bi, 0, 0))],
        out_specs=tok,
        out_shape=jax.ShapeDtypeStruct((b, s, d), F32),
        scratch_shapes=[pltpu.VMEM((ec, tt), F32), pltpu.VMEM((ec, tt), BF16), pltpu.VMEM((tt, d), F32)],
        compiler_params=_cp("parallel", "parallel", "arbitrary"),
        name="peer_dense",
    )(xn, u_b, v_b, s0t, e0t, s1t, e1t, thr, res, gate.reshape(b, 1, d))


def _rope_tables(s, dk):
    half = dk // 2
    nf = half // 2
    inv = ROPE_BASE ** (-jnp.arange(nf, dtype=F32) / nf)
    pos = jnp.arange(s)
    ang_r = (pos // GRID_W).astype(F32)[:, None] * inv
    ang_c = (pos % GRID_W).astype(F32)[:, None] * inv
    cos = jnp.concatenate([jnp.cos(ang_r)] * 2 + [jnp.cos(ang_c)] * 2, axis=1)
    sin = jnp.concatenate([-jnp.sin(ang_r), jnp.sin(ang_r), -jnp.sin(ang_c), jnp.sin(ang_c)], axis=1)
    return cos, sin


def _init_state(b, heads, m_value):
    return (jnp.zeros((b, heads, A_DK, A_DV + LANE), F32),
            jnp.full((b, heads, 1, LANE), m_value, F32))


def kernel(x, c, ctx, c_ctx, w_ada, b_ada, norm_g, w_in, a_gate_b, a_hnorm_g, b_conv, c_qk_g, c_rpb,
           w_a_out, w_b_out, w_c_out, w_out, peer_wq, peer_keys, peer_u, peer_v):
    bsz, seq, d = x.shape
    depth = w_ada.shape[0]
    a_heads = a_gate_b.shape[2]
    c_heads = c_rpb.shape[1]

    off_ak = 0
    off_av = off_ak + a_heads * A_DK
    gate_lo = off_av + a_heads * A_DV
    gate_hi = gate_lo + 4 * a_heads
    off_ck = gate_lo
    off_cv = off_ck + c_heads * C_DH
    kv_width = off_cv + c_heads * C_DH
    off_aq = kv_width
    off_ao = off_aq + a_heads * A_DK
    off_bb = off_ao + a_heads * A_DV
    off_bc = off_bb + d
    off_bx = off_bc + d
    off_cq = off_bx + d
    off_gate = off_cq + c_heads * C_DH

    rows = -(-(bsz + 1) // 8) * 8
    cond = jnp.concatenate([c, c_ctx[None, :], jnp.zeros((rows - bsz - 1, d), F32)], axis=0)
    mods = adaln(cond, w_ada, b_ada)

    rope_tabs = _rope_tables(seq, A_DK)
    h_ctx = ctx
    for l in range(depth):
        last = l == depth - 1
        mod_x = [mods[l, :bsz, k * d:(k + 1) * d] for k in range(6)]
        mod_c = [jnp.broadcast_to(mods[l, bsz, k * d:(k + 1) * d], (bsz, d)) for k in range(6)]
        w_proj = jnp.concatenate([w_in[l, :, :gate_lo], w_in[l, :, gate_hi:]], axis=1).astype(BF16)
        w_gates = w_in[l, :, gate_lo:gate_hi]
        g_q, g_k = c_qk_g[l, 0], c_qk_g[l, 1]

        cn = norm_mod(h_ctx, norm_g[l, 0], mod_c[0], mod_c[1])
        pc = matmul(cn, w_proj, n_cols=kv_width if last else None)
        gc_c, gr_c = gate_prep(cn, w_gates, a_gate_b[l])
        offs_c = (off_ak, off_ak, off_av) if last else (off_aq, off_ak, off_av)
        ninf = _init_state(bsz, a_heads, -jnp.inf)
        _, _, state_f, state_b = mlstm(pc, offs_c, gc_c, gr_c, None, ninf, ninf)

        xn = norm_mod(x, norm_g[l, 0], mod_x[0], mod_x[1])
        p = matmul(xn, w_proj)
        gc_x, gr_x = gate_prep(xn, w_gates, a_gate_b[l])
        h_f, h_b, _, _ = mlstm(p, (off_aq, off_ak, off_av), gc_x, gr_x, rope_tabs, state_f, state_b)
        ya = mlstm_out(h_f, h_b, p, off_ao, a_hnorm_g[l])
        yb = short_conv(p, off_bb, off_bc, off_bx, b_conv[l])
        yc = natten(p, pc, off_cq, off_ck, off_cv, c_rpb[l], g_q, g_k)
        w_a, w_b, w_c, w_o = [w[l].astype(BF16) for w in (w_a_out, w_b_out, w_c_out, w_out)]
        y = merge(ya, yb, yc, w_a, w_b, w_c, p, off_gate)
        x_new = matmul(y, w_o, residual=(x, mod_x[2]))
        xn2 = norm_mod(x_new, norm_g[l, 1], mod_x[3], mod_x[4])
        wq_b = peer_wq[l].astype(BF16)
        u_b = peer_u[l].astype(BF16)
        v_b = peer_v[l].astype(BF16)
        sel = peer_select(matmul(xn2, wq_b), peer_keys[l])
        x_out = peer_dense(xn2, u_b, v_b, sel, x_new, mod_x[5])

        if not last:
            zero = _init_state(bsz, a_heads, 0.0)
            hc_f, hc_b, _, _ = mlstm(pc, (off_aq, off_ak, off_av), gc_c, gr_c, None, zero, zero)
            ya_c = mlstm_out(hc_f, hc_b, pc, off_ao, a_hnorm_g[l])
            yb_c = short_conv(pc, off_bb, off_bc, off_bx, b_conv[l])
            yc_c = ctx_attn(pc, off_cq, off_ck, off_cv, c_heads, g_q, g_k)
            y_c = merge(ya_c, yb_c, yc_c, w_a, w_b, w_c, pc, off_gate)
            hc = matmul(y_c, w_o, residual=(h_ctx, mod_c[2]))
            hcn = norm_mod(hc, norm_g[l, 1], mod_c[3], mod_c[4])
            sel_c = peer_select(matmul(hcn, wq_b), peer_keys[l])
            h_ctx = peer_dense(hcn, u_b, v_b, sel_c, hc, mod_c[5])
        x = x_out
    return x
```

```python
import functools

import jax
import jax.numpy as jnp
import numpy as np
from jax import lax
from jax.experimental import pallas as pl
from jax.experimental.pallas import tpu as pltpu

F32 = jnp.float32
BF16 = jnp.bfloat16
HIGHEST = lax.Precision.HIGHEST

GRID_W = 64
ROPE_BASE = 10000.0
EPS = 1e-6
PEER_TOPK = 16
A_DK = 128
A_DV = 256
C_DH = 128
SCAN_CHUNK = 256
NA_QROWS = 8
NA_KROWS = 16
NEG = -1e30

VMEM_LIMIT = 56 * 1024 * 1024
LANE = 128


def _cp(*sem, flags=None):
    return pltpu.CompilerParams(dimension_semantics=sem, vmem_limit_bytes=VMEM_LIMIT, flags=flags)


def _sigmoid(x):
    return 1.0 / (1.0 + jnp.exp(-x))


def _log_sigmoid(x):
    return jnp.minimum(x, 0.0) - jnp.log1p(jnp.exp(-jnp.abs(x)))


def _pick(n, pref):
    t = min(n, pref)
    while n % t:
        t //= 2
    return t


def _adaln_kernel(c_ref, w_ref, b_ref, o_ref):
    cc = c_ref[...]
    a = cc * _sigmoid(cc)
    o_ref[...] = jnp.dot(a, w_ref[...], precision=HIGHEST, preferred_element_type=F32) + b_ref[...]


def adaln(cond, w_ada, b_ada):
    nl, d, n = w_ada.shape
    rows = cond.shape[0]
    tn = _pick(n, 1024)
    return pl.pallas_call(
        _adaln_kernel,
        grid=(nl, n // tn),
        in_specs=[
            pl.BlockSpec((rows, d), lambda l, j: (0, 0)),
            pl.BlockSpec((None, d, tn), lambda l, j: (l, 0, j)),
            pl.BlockSpec((None, 1, tn), lambda l, j: (l, 0, j)),
        ],
        out_specs=pl.BlockSpec((None, rows, tn), lambda l, j: (l, 0, j)),
        out_shape=jax.ShapeDtypeStruct((nl, rows, n), F32),
        compiler_params=_cp("parallel", "parallel"),
        name="adaln",
    )(cond, w_ada, b_ada.reshape(nl, 1, n))


def _norm_mod_kernel(x_ref, g_ref, sh_ref, sc_ref, o_ref):
    x = x_ref[...]
    y = x * lax.rsqrt(jnp.mean(x * x, axis=-1, keepdims=True) + EPS) * g_ref[...]
    o_ref[...] = (y * (1.0 + sc_ref[...]) + sh_ref[...]).astype(o_ref.dtype)


def norm_mod(x, g, shift, scale):
    b, s, d = x.shape
    tm = _pick(s, 512)
    vec = pl.BlockSpec((None, 1, d), lambda bi, i: (bi, 0, 0))
    return pl.pallas_call(
        _norm_mod_kernel,
        grid=(b, s // tm),
        in_specs=[
            pl.BlockSpec((None, tm, d), lambda bi, i: (bi, i, 0)),
            pl.BlockSpec((1, d), lambda bi, i: (0, 0)),
            vec, vec,
        ],
        out_specs=pl.BlockSpec((None, tm, d), lambda bi, i: (bi, i, 0)),
        out_shape=jax.ShapeDtypeStruct((b, s, d), BF16),
        compiler_params=_cp("parallel", "parallel"),
        name="norm_mod",
    )(x, g.reshape(1, d), shift.reshape(b, 1, d), scale.reshape(b, 1, d))


def _mm_kernel(a_ref, w_ref, o_ref):
    o_ref[...] = jnp.dot(a_ref[...], w_ref[...], preferred_element_type=F32)


def _mm_res_kernel(a_ref, w_ref, x_ref, g_ref, o_ref):
    acc = jnp.dot(a_ref[...], w_ref[...], preferred_element_type=F32)
    o_ref[...] = x_ref[...] + g_ref[...] * acc


def matmul(a, w, n_cols=None, residual=None):
    b, s, k = a.shape
    n = w.shape[1] if n_cols is None else n_cols
    tm = _pick(s, 1024)
    tn = _pick(n, 1024)
    in_specs = [
        pl.BlockSpec((None, tm, k), lambda bi, i, j: (bi, i, 0)),
        pl.BlockSpec((k, tn), lambda bi, i, j: (0, j)),
    ]
    args = [a, w]
    kern = _mm_kernel
    if residual is not None:
        x, gate = residual
        in_specs += [
            pl.BlockSpec((None, tm, tn), lambda bi, i, j: (bi, i, j)),
            pl.BlockSpec((None, 1, tn), lambda bi, i, j: (bi, 0, j)),
        ]
        args += [x, gate.reshape(b, 1, n)]
        kern = _mm_res_kernel
    return pl.pallas_call(
        kern,
        grid=(b, s // tm, n // tn),
        in_specs=in_specs,
        out_specs=pl.BlockSpec((None, tm, tn), lambda bi, i, j: (bi, i, j)),
        out_shape=jax.ShapeDtypeStruct((b, s, n), F32),
        compiler_params=_cp("parallel", "parallel", "arbitrary"),
        name="proj_res" if residual is not None else "proj",
    )(*args)


def _gates_kernel(x_ref, wf_ref, wi_ref, wft_ref, wit_ref, bfc_ref, bic_ref, bfr_ref, bir_ref,
                  gc_ref, gr_ref, *, heads):
    x = x_ref[...]
    ln = x.shape[0]
    r_i = lax.broadcasted_iota(jnp.int32, (ln, ln), 0)
    c_i = lax.broadcasted_iota(jnp.int32, (ln, ln), 1)
    tril = (c_i <= r_i).astype(F32)
    triu = (c_i >= r_i).astype(F32)

    gf = jnp.dot(x, wf_ref[...], preferred_element_type=F32) + bfc_ref[...]
    gi = jnp.dot(x, wi_ref[...], preferred_element_type=F32) + bic_ref[...]
    ls = _log_sigmoid(gf)
    pre = jnp.dot(tril, ls, precision=HIGHEST, preferred_element_type=F32)
    suf = jnp.dot(triu, ls, precision=HIGHEST, preferred_element_type=F32)
    lane = lax.broadcasted_iota(jnp.int32, gf.shape, 1)
    cum = jnp.where(lane < 2 * heads, pre, suf)
    is_a = (lane < heads) | ((lane >= 2 * heads) & (lane < 3 * heads))
    gc_ref[...] = jnp.where(is_a, cum, gi - cum)

    dn = (((1,), (1,)), ((), ()))
    gft = lax.dot_general(wft_ref[...], x, dn, preferred_element_type=F32) + bfr_ref[...]
    git = lax.dot_general(wit_ref[...], x, dn, preferred_element_type=F32) + bir_ref[...]
    lst = _log_sigmoid(gft)
    pre_t = jnp.dot(lst, triu, precision=HIGHEST, preferred_element_type=F32)
    suf_t = jnp.dot(lst, tril, precision=HIGHEST, preferred_element_type=F32)
    row = lax.broadcasted_iota(jnp.int32, gft.shape, 0)
    cum_t = jnp.where(row < 2 * heads, pre_t, suf_t)
    is_a_t = (row < heads) | ((row >= 2 * heads) & (row < 3 * heads))
    gr_ref[...] = jnp.where(is_a_t, cum_t, git - cum_t)


def gate_prep(xn, w_gates, gate_b):
    b, s, d = xn.shape
    heads = gate_b.shape[1]
    nrow = 4 * heads
    ln = SCAN_CHUNK
    wi_f, wf_f, wi_b, wf_b = [w_gates[:, g * heads:(g + 1) * heads] for g in range(4)]
    bi_f, bf_f, bi_b, bf_b = [gate_b[g] for g in range(4)]
    zw = jnp.zeros_like(wi_f)
    zb = jnp.zeros_like(bi_f)
    w_f = jnp.concatenate([wf_f, wf_f, wf_b, wf_b], axis=1)
    w_i = jnp.concatenate([zw, wi_f, zw, wi_b], axis=1)
    b_f = jnp.concatenate([bf_f, bf_f, bf_b, bf_b])
    b_i = jnp.concatenate([zb, bi_f, zb, bi_b])
    padc = LANE - nrow
    w_f_c = jnp.pad(w_f, ((0, 0), (0, padc))).astype(BF16)
    w_i_c = jnp.pad(w_i, ((0, 0), (0, padc))).astype(BF16)
    b_f_c = jnp.pad(b_f, (0, padc)).reshape(1, LANE)
    b_i_c = jnp.pad(b_i, (0, padc)).reshape(1, LANE)
    w_f_r = w_f.T.astype(BF16)
    w_i_r = w_i.T.astype(BF16)
    b_f_r = b_f.reshape(nrow, 1)
    b_i_r = b_i.reshape(nrow, 1)
    full = lambda shp: pl.BlockSpec(shp, lambda bi, i: (0, 0))
    return pl.pallas_call(
        functools.partial(_gates_kernel, heads=heads),
        grid=(b, s // ln),
        in_specs=[
            pl.BlockSpec((None, ln, d), lambda bi, i: (bi, i, 0)),
            full((d, LANE)), full((d, LANE)), full((nrow, d)), full((nrow, d)),
            full((1, LANE)), full((1, LANE)), full((nrow, 1)), full((nrow, 1)),
        ],
        out_specs=[
            pl.BlockSpec((None, ln, LANE), lambda bi, i: (bi, i, 0)),
            pl.BlockSpec((None, nrow, ln), lambda bi, i: (bi, 0, i)),
        ],
        out_shape=[
            jax.ShapeDtypeStruct((b, s, LANE), F32),
            jax.ShapeDtypeStruct((b, nrow, s), F32),
        ],
        compiler_params=_cp("parallel", "parallel"),
        name="gate_prep",
    )(xn, w_f_c, w_i_c, w_f_r, w_i_r, b_f_c, b_i_c, b_f_r, b_i_r)


def _rope(t, cos, sin):
    lane = lax.broadcasted_iota(jnp.int32, t.shape, 1)
    quarter = t.shape[1] // 4
    partner = jnp.where((lane & (2 * quarter - 1)) < quarter,
                        pltpu.roll(t, t.shape[1] - quarter, axis=1),
                        pltpu.roll(t, quarter, axis=1))
    return t * cos + partner * sin


def _mlstm_dir(q, k, v, a_col, r_col, r_row, b_last, mask, c_ref, m_ref, h_ref):
    ln = q.shape[0]
    dv = v.shape[1]
    m = m_ref[...][:, 0:1]
    v_ext = jnp.concatenate([v, jnp.ones((ln, LANE), F32)], axis=1).astype(BF16)
    dlog = jnp.where(mask, a_col + r_row, NEG)
    m_row = jnp.maximum(a_col + m, jnp.max(dlog, axis=1, keepdims=True))
    w_inter = jnp.exp(a_col + m - m_row)
    dmat = jnp.exp(dlog - m_row)
    s = lax.dot_general(q, k.astype(BF16), (((1,), (1,)), ((), ())), preferred_element_type=F32)
    sc = (s * dmat).astype(BF16)
    c_ext = c_ref[...]
    num = w_inter * jnp.dot(q, c_ext.astype(BF16), preferred_element_type=F32)
    num = num + jnp.dot(sc, v_ext, preferred_element_type=F32)
    den = num[:, dv:]
    denom = jnp.maximum(jnp.abs(den), jnp.exp(-m_row))
    inv = 1.0 / denom
    h_ref[...] = num[:, :dv] * jnp.concatenate([inv] * (dv // LANE), axis=1)
    r_max = jnp.max(r_row, axis=1, keepdims=True)
    m_new = b_last + jnp.maximum(m, r_max)
    decay = jnp.exp(b_last + m - m_new)
    ke = (k * jnp.exp(b_last + r_col - m_new)).astype(BF16)
    upd = lax.dot_general(ke, v_ext, (((0,), (0,)), ((), ())), preferred_element_type=F32)
    c_ref[...] = decay * c_ext + upd
    m_ref[...] = jnp.broadcast_to(m_new, m_ref.shape)


def _mlstm_kernel(qf_ref, kf_ref, vf_ref, gcf_ref, grf_ref, qb_ref, kb_ref, vb_ref, gcb_ref, grb_ref,
                  cosf_ref, sinf_ref, cosb_ref, sinb_ref, c0f_ref, m0f_ref, c0b_ref, m0b_ref,
                  hf_ref, hb_ref, cff_ref, mff_ref, cfb_ref, mfb_ref,
                  cf_s, mf_s, cb_s, mb_s, *, heads, use_rope):
    h = pl.program_id(1)
    i = pl.program_id(2)

    @pl.when(i == 0)
    def _():
        cf_s[...] = c0f_ref[...]
        mf_s[...] = m0f_ref[...]
        cb_s[...] = c0b_ref[...]
        mb_s[...] = m0b_ref[...]

    ln = qf_ref.shape[0]
    r_i = lax.broadcasted_iota(jnp.int32, (ln, ln), 0)
    c_i = lax.broadcasted_iota(jnp.int32, (ln, ln), 1)
    lane = lax.broadcasted_iota(jnp.int32, (ln, LANE), 1)
    rowi = lax.broadcasted_iota(jnp.int32, (4 * heads, ln), 0)
    scale = A_DK ** -0.5

    def col(g_ref, idx):
        return jnp.sum(jnp.where(lane == idx, g_ref[...], 0.0), axis=1, keepdims=True)

    def rowv(g_ref, idx):
        return jnp.sum(jnp.where(rowi == idx, g_ref[...], 0.0), axis=0, keepdims=True)

    def prep(q_ref, k_ref, cos_ref, sin_ref):
        q = q_ref[...]
        k = k_ref[...]
        if use_rope:
            q = _rope(q, cos_ref[...], sin_ref[...])
            k = _rope(k, cos_ref[...], sin_ref[...])
        return (q * scale).astype(BF16), k

    q, k = prep(qf_ref, kf_ref, cosf_ref, sinf_ref)
    a_col = col(gcf_ref, h)
    r_col = col(gcf_ref, heads + h)
    r_row = rowv(grf_ref, heads + h)
    b_last = a_col[ln - 1:ln, :]
    _mlstm_dir(q, k, vf_ref[...], a_col, r_col, r_row, b_last, c_i <= r_i, cf_s, mf_s, hf_ref)

    q, k = prep(qb_ref, kb_ref, cosb_ref, sinb_ref)
    a_col = col(gcb_ref, 2 * heads + h)
    r_col = col(gcb_ref, 3 * heads + h)
    r_row = rowv(grb_ref, 3 * heads + h)
    b_last = a_col[0:1, :]
    _mlstm_dir(q, k, vb_ref[...], a_col, r_col, r_row, b_last, c_i >= r_i, cb_s, mb_s, hb_ref)

    cff_ref[...] = cf_s[...]
    mff_ref[...] = mf_s[...]
    cfb_ref[...] = cb_s[...]
    mfb_ref[...] = mb_s[...]


def mlstm(p, offs, gc, gr, rope_tabs, state_f, state_b):
    b, s, _ = p.shape
    heads = gr.shape[1] // 4
    ln = SCAN_CHUNK
    nc = s // ln
    oq, ok, ov = [o // A_DK for o in offs[:2]] + [offs[2] // A_DV]
    use_rope = rope_tabs is not None
    if not use_rope:
        rope_tabs = (jnp.zeros((s, A_DK), F32),) * 2
    cos, sin = rope_tabs

    fwd = lambda i: i
    bwd = lambda i: nc - 1 - i

    def specs(ix):
        return [
            pl.BlockSpec((None, ln, A_DK), lambda bi, h, i: (bi, ix(i), oq + h)),
            pl.BlockSpec((None, ln, A_DK), lambda bi, h, i: (bi, ix(i), ok + h)),
            pl.BlockSpec((None, ln, A_DV), lambda bi, h, i: (bi, ix(i), ov + h)),
            pl.BlockSpec((None, ln, LANE), lambda bi, h, i: (bi, ix(i), 0)),
            pl.BlockSpec((None, 4 * heads, ln), lambda bi, h, i: (bi, 0, ix(i))),
        ]

    def tab(ix):
        return pl.BlockSpec((ln, A_DK), lambda bi, h, i: (ix(i), 0))

    dce = A_DV + LANE
    c_spec = pl.BlockSpec((None, None, A_DK, dce), lambda bi, h, i: (bi, h, 0, 0))
    m_spec = pl.BlockSpec((None, None, 1, LANE), lambda bi, h, i: (bi, h, 0, 0))
    outs = pl.pallas_call(
        functools.partial(_mlstm_kernel, heads=heads, use_rope=use_rope),
        grid=(b, heads, nc),
        in_specs=specs(fwd) + specs(bwd) + [tab(fwd), tab(fwd), tab(bwd), tab(bwd),
                                            c_spec, m_spec, c_spec, m_spec],
        out_specs=[
            pl.BlockSpec((None, ln, A_DV), lambda bi, h, i: (bi, i, h)),
            pl.BlockSpec((None, ln, A_DV), lambda bi, h, i: (bi, nc - 1 - i, h)),
            c_spec, m_spec, c_spec, m_spec,
        ],
        out_shape=[
            jax.ShapeDtypeStruct((b, s, heads * A_DV), F32),
            jax.ShapeDtypeStruct((b, s, heads * A_DV), F32),
            jax.ShapeDtypeStruct((b, heads, A_DK, dce), F32),
            jax.ShapeDtypeStruct((b, heads, 1, LANE), F32),
            jax.ShapeDtypeStruct((b, heads, A_DK, dce), F32),
            jax.ShapeDtypeStruct((b, heads, 1, LANE), F32),
        ],
        scratch_shapes=[
            pltpu.VMEM((A_DK, dce), F32), pltpu.VMEM((1, LANE), F32),
            pltpu.VMEM((A_DK, dce), F32), pltpu.VMEM((1, LANE), F32),
        ],
        compiler_params=_cp("parallel", "parallel", "arbitrary"),
        name="mlstm",
    )(p, p, p, gc, gr, p, p, p, gc, gr, cos, sin, cos, sin,
      state_f[0], state_f[1], state_b[0], state_b[1])
    h_f, h_b, cf, mf, cb, mb = outs
    return h_f, h_b, (cf, mf), (cb, mb)


def _mlstm_out_kernel(hf_ref, hb_ref, o_ref, g_ref, y_ref):
    hs = hf_ref[...] + hb_ref[...]
    hs = hs * lax.rsqrt(jnp.mean(hs * hs, axis=-1, keepdims=True) + EPS) * g_ref[...]
    y_ref[...] = (_sigmoid(o_ref[...]) * hs).astype(y_ref.dtype)


def mlstm_out(h_f, h_b, p, off_o, hnorm_g):
    b, s, w = h_f.shape
    heads = w // A_DV
    tm = _pick(s, 1024)
    oo = off_o // A_DV
    blk = lambda off: pl.BlockSpec((None, tm, A_DV), lambda bi, i, h: (bi, i, off + h))
    return pl.pallas_call(
        _mlstm_out_kernel,
        grid=(b, s // tm, heads),
        in_specs=[blk(0), blk(0), blk(oo), pl.BlockSpec((1, A_DV), lambda bi, i, h: (0, h))],
        out_specs=blk(0),
        out_shape=jax.ShapeDtypeStruct((b, s, w), BF16),
        compiler_params=_cp("parallel", "parallel", "parallel"),
        name="mlstm_out",
    )(h_f, h_b, p, hnorm_g.reshape(1, w))


def _conv_kernel(bb_ref, bc_ref, bx_ref, pc_ref, px_ref, nc_ref, nx_ref, w_ref, y_ref):
    i = pl.program_id(1)
    last = pl.num_programs(1) - 1
    u = bc_ref[...] * bx_ref[...]
    tm = u.shape[0]
    hr = pc_ref.shape[0]
    u_prev = pc_ref[...][hr - 1:hr, :] * px_ref[...][hr - 1:hr, :]
    u_next = nc_ref[...][0:1, :] * nx_ref[...][0:1, :]
    u_prev = jnp.where(i == 0, 0.0, u_prev)
    u_next = jnp.where(i == last, 0.0, u_next)
    row = lax.broadcasted_iota(jnp.int32, u.shape, 0)
    dn = jnp.where(row == 0, u_prev, pltpu.roll(u, 1, axis=0))
    up = jnp.where(row == tm - 1, u_next, pltpu.roll(u, tm - 1, axis=0))
    w = w_ref[...]
    y = dn * w[0:1, :] + u * w[1:2, :] + up * w[2:3, :]
    y_ref[...] = (bb_ref[...] * y).astype(y_ref.dtype)


def short_conv(p, off_b, off_c, off_x, w_conv):
    b, s, _ = p.shape
    kw, width = w_conv.shape
    tn = _pick(width, 512)
    tm = _pick(s, 512)
    hr = 8
    nhb = s // hr
    ob, oc, ox = off_b // tn, off_c // tn, off_x // tn
    main = lambda o: pl.BlockSpec((None, tm, tn), lambda bi, i, j: (bi, i, o + j))
    prev = lambda o: pl.BlockSpec(
        (None, hr, tn), lambda bi, i, j: (bi, jnp.maximum(i * (tm // hr) - 1, 0), o + j))
    nxt = lambda o: pl.BlockSpec(
        (None, hr, tn), lambda bi, i, j: (bi, jnp.minimum((i + 1) * (tm // hr), nhb - 1), o + j))
    return pl.pallas_call(
        _conv_kernel,
        grid=(b, s // tm, width // tn),
        in_specs=[main(ob), main(oc), main(ox), prev(oc), prev(ox), nxt(oc), nxt(ox),
                  pl.BlockSpec((kw, tn), lambda bi, i, j: (0, j))],
        out_specs=pl.BlockSpec((None, tm, tn), lambda bi, i, j: (bi, i, j)),
        out_shape=jax.ShapeDtypeStruct((b, s, width), BF16),
        compiler_params=_cp("parallel", "parallel", "parallel"),
        name="short_conv",
    )(p, p, p, p, p, p, p, w_conv)


def _rms(t, g):
    return t * lax.rsqrt(jnp.mean(t * t, axis=-1, keepdims=True) + EPS) * g


def _natten_kernel(q_ref, k_ref, v_ref, kc_ref, vc_ref, bias_ref, gq_ref, gk_ref, o_ref, *, rows):
    j = pl.program_id(2)
    nk = NA_KROWS * GRID_W
    w0 = jnp.clip(NA_QROWS * j - (NA_KROWS - NA_QROWS) // 2, 0, rows - NA_KROWS)
    start = pl.multiple_of(w0 * GRID_W, GRID_W * 4)
    gk = gk_ref[...]
    qn = (_rms(q_ref[...], gq_ref[...]) * (C_DH ** -0.5)).astype(BF16)
    kw = _rms(k_ref[pl.ds(start, nk), :], gk).astype(BF16)
    vw = v_ref[pl.ds(start, nk), :].astype(BF16)
    kc = _rms(kc_ref[...], gk).astype(BF16)
    vc = vc_ref[...].astype(BF16)
    dn = (((1,), (1,)), ((), ()))
    s_loc = lax.dot_general(qn, kw, dn, preferred_element_type=F32) + bias_ref[...]
    s_ctx = lax.dot_general(qn, kc, dn, preferred_element_type=F32)
    m = jnp.maximum(jnp.max(s_loc, axis=1, keepdims=True), jnp.max(s_ctx, axis=1, keepdims=True))
    p_loc = jnp.exp(s_loc - m)
    p_ctx = jnp.exp(s_ctx - m)
    l = jnp.sum(p_loc, axis=1, keepdims=True) + jnp.sum(p_ctx, axis=1, keepdims=True)
    o = jnp.dot(p_loc.astype(BF16), vw, preferred_element_type=F32)
    o = o + jnp.dot(p_ctx.astype(BF16), vc, preferred_element_type=F32)
    o_ref[...] = (o / l).astype(o_ref.dtype)


def _natten_bias(rpb, rows):
    heads, nr, ncb = rpb.shape
    win_r, win_c = (nr + 1) // 2, (ncb + 1) // 2
    qr = np.arange(NA_QROWS)[:, None, None, None]
    qc = np.arange(GRID_W)[None, :, None, None]
    kr = np.arange(NA_KROWS)[None, None, :, None]
    kc = np.arange(GRID_W)[None, None, None, :]
    col0 = np.clip(qc - win_c // 2, 0, GRID_W - win_c)
    col_ok = (kc >= col0) & (kc < col0 + win_c)
    dc = np.clip(kc - qc + win_c - 1, 0, ncb - 1)[0, :, 0, :]
    hot_c = (dc[:, :, None] == np.arange(ncb)).astype(np.float32)
    by_col = jnp.einsum('hrc,qkc->hrqk', rpb.astype(F32), hot_c, precision=HIGHEST)
    nblk = rows // NA_QROWS
    n_q, n_k = NA_QROWS * GRID_W, NA_KROWS * GRID_W
    out = []
    for jb in (0, 1, nblk - 1):
        w0 = int(np.clip(NA_QROWS * jb - (NA_KROWS - NA_QROWS) // 2, 0, rows - NA_KROWS))
        r_abs = NA_QROWS * jb + qr
        k_abs = w0 + kr
        r0 = np.clip(r_abs - win_r // 2, 0, rows - win_r)
        ok = np.broadcast_to((k_abs >= r0) & (k_abs < r0 + win_r) & col_ok,
                             (NA_QROWS, GRID_W, NA_KROWS, GRID_W)).reshape(n_q, n_k)
        dr = np.clip(k_abs - r_abs + win_r - 1, 0, nr - 1)[:, 0, :, 0]
        hot_r = (dr[:, :, None] == np.arange(nr)).astype(np.float32)
        bias = jnp.einsum('abr,hrqk->haqbk', hot_r, by_col, precision=HIGHEST).reshape(heads, n_q, n_k)
        out.append(jnp.where(jnp.asarray(ok), bias, NEG))
    return jnp.stack(out)


def natten(p, pc, off_q, off_k, off_v, rpb, g_q, g_k):
    b, s, _ = p.shape
    ctx = pc.shape[1]
    heads = rpb.shape[0]
    rows = s // GRID_W
    nblk = rows // NA_QROWS
    nq, nk = NA_QROWS * GRID_W, NA_KROWS * GRID_W
    oq, ok, ov = off_q // C_DH, off_k // C_DH, off_v // C_DH
    bias = _natten_bias(rpb, rows)

    def variant(j):
        return jnp.where(j == 0, 0, jnp.where(j == nblk - 1, 2, 1))

    return pl.pallas_call(
        functools.partial(_natten_kernel, rows=rows),
        grid=(b, heads, nblk),
        in_specs=[
            pl.BlockSpec((None, nq, C_DH), lambda bi, h, j: (bi, j, oq + h)),
            pl.BlockSpec((None, s, C_DH), lambda bi, h, j: (bi, 0, ok + h)),
            pl.BlockSpec((None, s, C_DH), lambda bi, h, j: (bi, 0, ov + h)),
            pl.BlockSpec((None, ctx, C_DH), lambda bi, h, j: (bi, 0, ok + h)),
            pl.BlockSpec((None, ctx, C_DH), lambda bi, h, j: (bi, 0, ov + h)),
            pl.BlockSpec((None, None, nq, nk), lambda bi, h, j: (variant(j), h, 0, 0)),
            pl.BlockSpec((1, C_DH), lambda bi, h, j: (0, 0)),
            pl.BlockSpec((1, C_DH), lambda bi, h, j: (0, 0)),
        ],
        out_specs=pl.BlockSpec((None, nq, C_DH), lambda bi, h, j: (bi, j, h)),
        out_shape=jax.ShapeDtypeStruct((b, s, heads * C_DH), BF16),
        compiler_params=_cp("parallel", "parallel", "arbitrary"),
        name="natten",
    )(p, p, p, pc, pc, bias, g_q.reshape(1, C_DH), g_k.reshape(1, C_DH))


def _ctx_attn_kernel(q_ref, k_ref, v_ref, gq_ref, gk_ref, o_ref):
    qn = (_rms(q_ref[...], gq_ref[...]) * (C_DH ** -0.5)).astype(BF16)
    kn = _rms(k_ref[...], gk_ref[...]).astype(BF16)
    s = lax.dot_general(qn, kn, (((1,), (1,)), ((), ())), preferred_element_type=F32)
    m = jnp.max(s, axis=1, keepdims=True)
    pr = jnp.exp(s - m)
    l = jnp.sum(pr, axis=1, keepdims=True)
    o = jnp.dot(pr.astype(BF16), v_ref[...].astype(BF16), preferred_element_type=F32)
    o_ref[...] = (o / l).astype(o_ref.dtype)


def ctx_attn(pc, off_q, off_k, off_v, heads, g_q, g_k):
    b, ctx, _ = pc.shape
    blk = lambda o: pl.BlockSpec((None, ctx, C_DH), lambda bi, h: (bi, 0, o // C_DH + h))
    gsp = pl.BlockSpec((1, C_DH), lambda bi, h: (0, 0))
    return pl.pallas_call(
        _ctx_attn_kernel,
        grid=(b, heads),
        in_specs=[blk(off_q), blk(off_k), blk(off_v), gsp, gsp],
        out_specs=blk(0),
        out_shape=jax.ShapeDtypeStruct((b, ctx, heads * C_DH), BF16),
        compiler_params=_cp("parallel", "parallel"),
        name="ctx_attn",
    )(pc, pc, pc, g_q.reshape(1, C_DH), g_k.reshape(1, C_DH))


def _merge_kernel(ya_ref, yb_ref, yc_ref, wa_ref, wb_ref, wc_ref, ga_ref, gb_ref, gc_ref, o_ref):
    y = _sigmoid(ga_ref[...]) * jnp.dot(ya_ref[...], wa_ref[...], preferred_element_type=F32)
    y = y + _sigmoid(gb_ref[...]) * jnp.dot(yb_ref[...], wb_ref[...], preferred_element_type=F32)
    y = y + _sigmoid(gc_ref[...]) * jnp.dot(yc_ref[...], wc_ref[...], preferred_element_type=F32)
    o_ref[...] = y.astype(o_ref.dtype)


def merge(ya, yb, yc, w_a, w_b, w_c, p, off_gate):
    b, s, k = ya.shape
    d = w_a.shape[1]
    tm = _pick(s, 512)
    tn = _pick(d, 512)
    og = off_gate // tn
    nb = d // tn
    act = pl.BlockSpec((None, tm, k), lambda bi, i, j: (bi, i, 0))
    wsp = pl.BlockSpec((k, tn), lambda bi, i, j: (0, j))
    gate = lambda g: pl.BlockSpec((None, tm, tn), lambda bi, i, j: (bi, i, og + g * nb + j))
    return pl.pallas_call(
        _merge_kernel,
        grid=(b, s // tm, nb),
        in_specs=[act, act, act, wsp, wsp, wsp, gate(0), gate(1), gate(2)],
        out_specs=pl.BlockSpec((None, tm, tn), lambda bi, i, j: (bi, i, j)),
        out_shape=jax.ShapeDtypeStruct((b, s, d), BF16),
        compiler_params=_cp("parallel", "parallel", "arbitrary"),
        name="merge",
    )(ya, yb, yc, w_a, w_b, w_c, p, p, p)


def _top_values(cur, k):
    n = cur.shape[0]
    idx = lax.broadcasted_iota(jnp.int32, cur.shape, 0).astype(F32)
    vals = []
    for _ in range(k):
        mx = jnp.max(cur, axis=0, keepdims=True)
        vals.append(mx)
        first = jnp.min(jnp.where(cur == mx, idx, float(n)), axis=0, keepdims=True)
        cur = jnp.where(idx == first, -jnp.inf, cur)
    return vals


SUBLANES = 8


def _sort_network(n):
    pairs = []
    p = 1
    while p < n:
        k = p
        while k >= 1:
            for j in range(k % p, n - k, 2 * k):
                for i in range(min(k, n - j - k)):
                    if (i + j) // (2 * p) == (i + j + k) // (2 * p):
                        pairs.append((i + j, i + j + k))
            k //= 2
        p *= 2
    return pairs


def _exchange(v, i, j):
    hi, lo = jnp.maximum(v[i], v[j]), jnp.minimum(v[i], v[j])
    v[i], v[j] = hi, lo


def _topk_sorted(st):
    k = PEER_TOPK
    assert st.shape[0] == k * SUBLANES
    v = [st[i * SUBLANES:(i + 1) * SUBLANES, :] for i in range(k)]
    for i, j in _sort_network(k):
        _exchange(v, i, j)
    shift = SUBLANES // 2
    while shift >= 1:
        v = [jnp.maximum(v[i], pltpu.roll(v[k - 1 - i], shift, axis=0)) for i in range(k)]
        step = k // 2
        while step >= 1:
            for i in range(k):
                if i & step == 0:
                    _exchange(v, i, i + step)
            step //= 2
        shift //= 2
    return v


def _stack_rows(vals, like):
    row = lax.broadcasted_iota(jnp.int32, like.shape, 0)
    out = jnp.full(like.shape, -jnp.inf, F32)
    for r, val in enumerate(vals):
        out = jnp.where(row == r, val, out)
    return out


def _pair_candidates(v0, v1):
    k = PEER_TOPK
    like = v0[0]
    row = lax.broadcasted_iota(jnp.int32, like.shape, 0)
    v1_rows = [_stack_rows(v1[g:g + SUBLANES], like) for g in range(0, k, SUBLANES)]
    blocks, singles = [], []
    for i in range(k):
        cnt = min(k, (k + 1) // (i + 1))
        if cnt == 1:
            singles.append(v0[i])
            continue
        for g in range(-(-cnt // SUBLANES)):
            valid = min(SUBLANES, cnt - g * SUBLANES)
            blk = v0[i] + v1_rows[g]
            blocks.append(blk if valid == SUBLANES else jnp.where(row < valid, blk, -jnp.inf))
    for g in range(0, len(singles), SUBLANES):
        blocks.append(_stack_rows(singles[g:g + SUBLANES], like) + v1[0])
    return jnp.concatenate(blocks, axis=0)


def _peer_select_kernel(q_ref, keys_ref, s0_ref, e0_ref, c_ref, e1_ref, *, heads):
    dn = (((1,), (1,)), ((), ()))
    half = keys_ref.shape[2]
    k = PEER_TOPK
    for h in range(heads):
        sides = []
        for side in range(2):
            hp = 2 * h + side
            qh = q_ref[:, hp * half:(hp + 1) * half].astype(BF16)
            st = lax.dot_general(keys_ref[hp], qh, dn, preferred_element_type=F32)
            sides.append((st, _topk_sorted(st)))
        (s0, v0), (s1, v1) = sides
        best = _top_values(_pair_candidates(v0, v1), k + 1)
        thr = 0.5 * (best[k - 1] + best[k])
        z = jnp.ones_like(thr)
        for kk in range(1, k):
            z = z + jnp.exp(best[kk] - best[0])
        inv_z = 1.0 / z
        s0_ref[h] = s0
        e0_ref[h] = jnp.where(s0 >= v0[k - 1][0:1, :], jnp.exp(s0 - v0[0][0:1, :]), 0.0) * inv_z
        c_ref[h] = thr - s1
        e1_ref[h] = jnp.where(s1 >= v1[k - 1][0:1, :], jnp.exp(s1 - v1[0][0:1, :]), 0.0)


def peer_select(q, keys):
    b, s, _ = q.shape
    heads, _, nkeys, half = keys.shape
    tt = _pick(s, 256)
    keys_b = keys.reshape(heads * 2, nkeys, half).astype(BF16)
    big = pl.BlockSpec((None, heads, nkeys, tt), lambda bi, i: (bi, 0, 0, i))
    big_shape = jax.ShapeDtypeStruct((b, heads, nkeys, s), F32)
    return pl.pallas_call(
        functools.partial(_peer_select_kernel, heads=heads),
        grid=(b, s // tt),
        in_specs=[
            pl.BlockSpec((None, tt, q.shape[2]), lambda bi, i: (bi, i, 0)),
            pl.BlockSpec((heads * 2, nkeys, half), lambda bi, i: (0, 0, 0)),
        ],
        out_specs=[big, big, big, big],
        out_shape=[big_shape] * 4,
        compiler_params=_cp("parallel", "parallel"),
        name="peer_select",
    )(q, keys_b)


def _gelu(x):
    return 0.5 * x * (1.0 + lax.erf(x * (2.0 ** -0.5)))


def _peer_dense_kernel(x_ref, u_ref, vt_ref, s0_ref, e0_ref, c_ref, e1_ref, res_ref, g_ref,
                       o_ref, st_s, wg_s, acc_s, *, heads, a_blk):
    c = pl.program_id(2)
    last = pl.num_programs(2) - 1
    nkeys = c_ref.shape[1]
    tt = x_ref.shape[0]

    @pl.when(c == 0)
    def _():
        acc_s[...] = jnp.zeros_like(acc_s)
        wg_s[...] = jnp.zeros_like(wg_s)

    pair = 2 * nkeys
    nsl = a_blk // 2
    for k in range(nsl):
        rows = slice(k * pair, (k + 1) * pair)
        st_s[k] = lax.dot_general(u_ref[rows, :], x_ref[...], (((1,), (1,)), ((), ())),
                                  preferred_element_type=F32)
        acc_s[...] += jnp.dot(vt_ref[:, rows], wg_s[rows, :], preferred_element_type=F32)
    bg = 32
    for k in range(nsl):
        for tl in range(tt // LANE):
            ls = slice(tl * LANE, (tl + 1) * LANE)
            for g in range(nkeys // bg):
                bs = slice(g * bg, (g + 1) * bg)
                w = [jnp.zeros((bg, LANE), F32) for _ in range(2)]
                for h in range(heads):
                    cv = c_ref[h, bs, ls]
                    ev = e1_ref[h, bs, ls]
                    for a2 in range(2):
                        a = 2 * k + a2
                        s0r = s0_ref[h, a:a + 1, ls]
                        e0r = e0_ref[h, a:a + 1, ls]
                        w[a2] = w[a2] + jnp.where(s0r >= cv, ev * e0r, 0.0)
                for a2 in range(2):
                    r0 = a2 * nkeys + g * bg
                    act = _gelu(st_s[k, r0:r0 + bg, ls])
                    wg_s[k * pair + r0:k * pair + r0 + bg, ls] = (w[a2] * act).astype(wg_s.dtype)

    @pl.when(c == last)
    def _():
        o_ref[...] = res_ref[...] + g_ref[...] * acc_s[...].T


def peer_dense(xn, u_b, vt_b, sel, res, gate):
    b, s, d = xn.shape
    s0t, e0t, ct, e1t = sel
    heads, nkeys = s0t.shape[1], s0t.shape[2]
    n_exp = u_b.shape[0]
    a_blk = 8
    ec = a_blk * nkeys
    nchunk = n_exp // ec
    tt = _pick(s, 512)
    once = pl.Buffered(1)
    tok = lambda: pl.BlockSpec((None, tt, d), lambda bi, i, c: (bi, i, 0), pipeline_mode=once)
    cur = lambda c: jnp.minimum(c, nchunk - 1)
    prev = lambda c: jnp.maximum(c - 1, 0)
    a_sel = pl.BlockSpec((None, heads, a_blk, tt), lambda bi, i, c: (bi, 0, cur(c), i))
    b_sel = pl.BlockSpec((None, heads, nkeys, tt), lambda bi, i, c: (bi, 0, 0, i), pipeline_mode=once)
    return pl.pallas_call(
        functools.partial(_peer_dense_kernel, heads=heads, a_blk=a_blk),
        grid=(b, s // tt, nchunk + 1),
        in_specs=[tok(),
                  pl.BlockSpec((ec, d), lambda bi, i, c: (cur(c), 0)),
                  pl.BlockSpec((d, ec), lambda bi, i, c: (0, prev(c))),
                  a_sel, a_sel, b_sel, b_sel, tok(),
                  pl.BlockSpec((None, 1, d), lambda bi, i, c: (bi, 0, 0))],
        out_specs=pl.BlockSpec((None, tt, d), lambda bi, i, c: (bi, i, 0)),
        out_shape=jax.ShapeDtypeStruct((b, s, d), F32),
        scratch_shapes=[pltpu.VMEM((a_blk // 2, 2 * nkeys, tt), F32), pltpu.VMEM((ec, tt), BF16),
                        pltpu.VMEM((d, tt), F32)],
        compiler_params=_cp("parallel", "parallel", "arbitrary"),
        name="peer_dense",
    )(xn, u_b, vt_b, s0t, e0t, ct, e1t, res, gate.reshape(b, 1, d))


def _rope_tables(s, dk):
    half = dk // 2
    nf = half // 2
    inv = ROPE_BASE ** (-jnp.arange(nf, dtype=F32) / nf)
    pos = jnp.arange(s)
    ang_r = (pos // GRID_W).astype(F32)[:, None] * inv
    ang_c = (pos % GRID_W).astype(F32)[:, None] * inv
    cos = jnp.concatenate([jnp.cos(ang_r)] * 2 + [jnp.cos(ang_c)] * 2, axis=1)
    sin = jnp.concatenate([-jnp.sin(ang_r), jnp.sin(ang_r), -jnp.sin(ang_c), jnp.sin(ang_c)], axis=1)
    return cos, sin


def _init_state(b, heads, m_value):
    return (jnp.zeros((b, heads, A_DK, A_DV + LANE), F32),
            jnp.full((b, heads, 1, LANE), m_value, F32))


def kernel(x, c, ctx, c_ctx, w_ada, b_ada, norm_g, w_in, a_gate_b, a_hnorm_g, b_conv, c_qk_g, c_rpb,
           w_a_out, w_b_out, w_c_out, w_out, peer_wq, peer_keys, peer_u, peer_v):
    bsz, seq, d = x.shape
    depth = w_ada.shape[0]
    a_heads = a_gate_b.shape[2]
    c_heads = c_rpb.shape[1]

    off_ak = 0
    off_av = off_ak + a_heads * A_DK
    gate_lo = off_av + a_heads * A_DV
    gate_hi = gate_lo + 4 * a_heads
    off_ck = gate_lo
    off_cv = off_ck + c_heads * C_DH
    kv_width = off_cv + c_heads * C_DH
    off_aq = kv_width
    off_ao = off_aq + a_heads * A_DK
    off_bb = off_ao + a_heads * A_DV
    off_bc = off_bb + d
    off_bx = off_bc + d
    off_cq = off_bx + d
    off_gate = off_cq + c_heads * C_DH

    rows = -(-(bsz + 1) // 8) * 8
    cond = jnp.concatenate([c, c_ctx[None, :], jnp.zeros((rows - bsz - 1, d), F32)], axis=0)
    mods = adaln(cond, w_ada, b_ada)

    rope_tabs = _rope_tables(seq, A_DK)
    h_ctx = ctx
    for l in range(depth):
        last = l == depth - 1
        mod_x = [mods[l, :bsz, k * d:(k + 1) * d] for k in range(6)]
        mod_c = [jnp.broadcast_to(mods[l, bsz, k * d:(k + 1) * d], (bsz, d)) for k in range(6)]
        w_proj = jnp.concatenate([w_in[l, :, :gate_lo], w_in[l, :, gate_hi:]], axis=1).astype(BF16)
        w_gates = w_in[l, :, gate_lo:gate_hi]
        g_q, g_k = c_qk_g[l, 0], c_qk_g[l, 1]

        cn = norm_mod(h_ctx, norm_g[l, 0], mod_c[0], mod_c[1])
        pc = matmul(cn, w_proj, n_cols=kv_width if last else None)
        gc_c, gr_c = gate_prep(cn, w_gates, a_gate_b[l])
        offs_c = (off_ak, off_ak, off_av) if last else (off_aq, off_ak, off_av)
        ninf = _init_state(bsz, a_heads, -jnp.inf)
        _, _, state_f, state_b = mlstm(pc, offs_c, gc_c, gr_c, None, ninf, ninf)

        xn = norm_mod(x, norm_g[l, 0], mod_x[0], mod_x[1])
        p = matmul(xn, w_proj)
        gc_x, gr_x = gate_prep(xn, w_gates, a_gate_b[l])
        h_f, h_b, _, _ = mlstm(p, (off_aq, off_ak, off_av), gc_x, gr_x, rope_tabs, state_f, state_b)
        ya = mlstm_out(h_f, h_b, p, off_ao, a_hnorm_g[l])
        yb = short_conv(p, off_bb, off_bc, off_bx, b_conv[l])
        yc = natten(p, pc, off_cq, off_ck, off_cv, c_rpb[l], g_q, g_k)
        w_a, w_b, w_c, w_o = [w[l].astype(BF16) for w in (w_a_out, w_b_out, w_c_out, w_out)]
        y = merge(ya, yb, yc, w_a, w_b, w_c, p, off_gate)
        x_new = matmul(y, w_o, residual=(x, mod_x[2]))
        xn2 = norm_mod(x_new, norm_g[l, 1], mod_x[3], mod_x[4])
        wq_b = peer_wq[l].astype(BF16)
        u_b = peer_u[l].astype(BF16)
        vt_b = peer_v[l].astype(BF16).T
        sel = peer_select(matmul(xn2, wq_b), peer_keys[l])
        x_out = peer_dense(xn2, u_b, vt_b, sel, x_new, mod_x[5])

        if not last:
            zero = _init_state(bsz, a_heads, 0.0)
            hc_f, hc_b, _, _ = mlstm(pc, (off_aq, off_ak, off_av), gc_c, gr_c, None, zero, zero)
            ya_c = mlstm_out(hc_f, hc_b, pc, off_ao, a_hnorm_g[l])
            yb_c = short_conv(pc, off_bb, off_bc, off_bx, b_conv[l])
            yc_c = ctx_attn(pc, off_cq, off_ck, off_cv, c_heads, g_q, g_k)
            y_c = merge(ya_c, yb_c, yc_c, w_a, w_b, w_c, pc, off_gate)
            hc = matmul(y_c, w_o, residual=(h_ctx, mod_c[2]))
            hcn = norm_mod(hc, norm_g[l, 1], mod_c[3], mod_c[4])
            sel_c = peer_select(matmul(hcn, wq_b), peer_keys[l])
            h_ctx = peer_dense(hcn, u_b, vt_b, sel_c, hc, mod_c[5])
        x = x_out
    return x
```

```python
import functools

import jax
import jax.numpy as jnp
import numpy as np
from jax import lax
from jax.experimental import pallas as pl
from jax.experimental.pallas import tpu as pltpu

F32 = jnp.float32
BF16 = jnp.bfloat16
HIGHEST = lax.Precision.HIGHEST

GRID_W = 64
ROPE_BASE = 10000.0
EPS = 1e-6
PEER_TOPK = 16
A_DK = 128
A_DV = 256
C_DH = 128
SCAN_CHUNK = 256
NA_QROWS = 8
NA_KROWS = 16
NEG = -1e30

VMEM_LIMIT = 56 * 1024 * 1024
LANE = 128


def _cp(*sem, flags=None):
    return pltpu.CompilerParams(dimension_semantics=sem, vmem_limit_bytes=VMEM_LIMIT, flags=flags)


def _sigmoid(x):
    return 1.0 / (1.0 + jnp.exp(-x))


def _log_sigmoid(x):
    return jnp.minimum(x, 0.0) - jnp.log1p(jnp.exp(-jnp.abs(x)))


def _pick(n, pref):
    t = min(n, pref)
    while n % t:
        t //= 2
    return t


def _adaln_kernel(c_ref, w_ref, b_ref, o_ref):
    cc = c_ref[...]
    a = cc * _sigmoid(cc)
    o_ref[...] = jnp.dot(a, w_ref[...], precision=HIGHEST, preferred_element_type=F32) + b_ref[...]


def adaln(cond, w_ada, b_ada):
    nl, d, n = w_ada.shape
    rows = cond.shape[0]
    tn = _pick(n, 1024)
    return pl.pallas_call(
        _adaln_kernel,
        grid=(nl, n // tn),
        in_specs=[
            pl.BlockSpec((rows, d), lambda l, j: (0, 0)),
            pl.BlockSpec((None, d, tn), lambda l, j: (l, 0, j)),
            pl.BlockSpec((None, 1, tn), lambda l, j: (l, 0, j)),
        ],
        out_specs=pl.BlockSpec((None, rows, tn), lambda l, j: (l, 0, j)),
        out_shape=jax.ShapeDtypeStruct((nl, rows, n), F32),
        compiler_params=_cp("parallel", "parallel"),
        name="adaln",
    )(cond, w_ada, b_ada.reshape(nl, 1, n))


def _norm_mod_kernel(x_ref, g_ref, sh_ref, sc_ref, o_ref):
    x = x_ref[...]
    y = x * lax.rsqrt(jnp.mean(x * x, axis=-1, keepdims=True) + EPS) * g_ref[...]
    o_ref[...] = (y * (1.0 + sc_ref[...]) + sh_ref[...]).astype(o_ref.dtype)


def norm_mod(x, g, shift, scale):
    b, s, d = x.shape
    tm = _pick(s, 512)
    vec = pl.BlockSpec((None, 1, d), lambda bi, i: (bi, 0, 0))
    return pl.pallas_call(
        _norm_mod_kernel,
        grid=(b, s // tm),
        in_specs=[
            pl.BlockSpec((None, tm, d), lambda bi, i: (bi, i, 0)),
            pl.BlockSpec((1, d), lambda bi, i: (0, 0)),
            vec, vec,
        ],
        out_specs=pl.BlockSpec((None, tm, d), lambda bi, i: (bi, i, 0)),
        out_shape=jax.ShapeDtypeStruct((b, s, d), BF16),
        compiler_params=_cp("parallel", "parallel"),
        name="norm_mod",
    )(x, g.reshape(1, d), shift.reshape(b, 1, d), scale.reshape(b, 1, d))


def _mm_kernel(a_ref, w_ref, o_ref):
    o_ref[...] = jnp.dot(a_ref[...], w_ref[...], preferred_element_type=F32)


def _mm_res_kernel(a_ref, w_ref, x_ref, g_ref, o_ref):
    acc = jnp.dot(a_ref[...], w_ref[...], preferred_element_type=F32)
    o_ref[...] = x_ref[...] + g_ref[...] * acc


def matmul(a, w, n_cols=None, residual=None):
    b, s, k = a.shape
    n = w.shape[1] if n_cols is None else n_cols
    tm = _pick(s, 1024)
    tn = _pick(n, 1024)
    in_specs = [
        pl.BlockSpec((None, tm, k), lambda bi, i, j: (bi, i, 0)),
        pl.BlockSpec((k, tn), lambda bi, i, j: (0, j)),
    ]
    args = [a, w]
    kern = _mm_kernel
    if residual is not None:
        x, gate = residual
        in_specs += [
            pl.BlockSpec((None, tm, tn), lambda bi, i, j: (bi, i, j)),
            pl.BlockSpec((None, 1, tn), lambda bi, i, j: (bi, 0, j)),
        ]
        args += [x, gate.reshape(b, 1, n)]
        kern = _mm_res_kernel
    return pl.pallas_call(
        kern,
        grid=(b, s // tm, n // tn),
        in_specs=in_specs,
        out_specs=pl.BlockSpec((None, tm, tn), lambda bi, i, j: (bi, i, j)),
        out_shape=jax.ShapeDtypeStruct((b, s, n), F32),
        compiler_params=_cp("parallel", "parallel", "arbitrary"),
        name="proj_res" if residual is not None else "proj",
    )(*args)


def _gates_kernel(x_ref, wf_ref, wi_ref, wft_ref, wit_ref, bfc_ref, bic_ref, bfr_ref, bir_ref,
                  gc_ref, gr_ref, *, heads):
    x = x_ref[...]
    ln = x.shape[0]
    r_i = lax.broadcasted_iota(jnp.int32, (ln, ln), 0)
    c_i = lax.broadcasted_iota(jnp.int32, (ln, ln), 1)
    tril = (c_i <= r_i).astype(F32)
    triu = (c_i >= r_i).astype(F32)

    gf = jnp.dot(x, wf_ref[...], preferred_element_type=F32) + bfc_ref[...]
    gi = jnp.dot(x, wi_ref[...], preferred_element_type=F32) + bic_ref[...]
    ls = _log_sigmoid(gf)
    pre = jnp.dot(tril, ls, precision=HIGHEST, preferred_element_type=F32)
    suf = jnp.dot(triu, ls, precision=HIGHEST, preferred_element_type=F32)
    lane = lax.broadcasted_iota(jnp.int32, gf.shape, 1)
    cum = jnp.where(lane < 2 * heads, pre, suf)
    is_a = (lane < heads) | ((lane >= 2 * heads) & (lane < 3 * heads))
    gc_ref[...] = jnp.where(is_a, cum, gi - cum)

    dn = (((1,), (1,)), ((), ()))
    gft = lax.dot_general(wft_ref[...], x, dn, preferred_element_type=F32) + bfr_ref[...]
    git = lax.dot_general(wit_ref[...], x, dn, preferred_element_type=F32) + bir_ref[...]
    lst = _log_sigmoid(gft)
    pre_t = jnp.dot(lst, triu, precision=HIGHEST, preferred_element_type=F32)
    suf_t = jnp.dot(lst, tril, precision=HIGHEST, preferred_element_type=F32)
    row = lax.broadcasted_iota(jnp.int32, gft.shape, 0)
    cum_t = jnp.where(row < 2 * heads, pre_t, suf_t)
    is_a_t = (row < heads) | ((row >= 2 * heads) & (row < 3 * heads))
    gr_ref[...] = jnp.where(is_a_t, cum_t, git - cum_t)


def gate_prep(xn, w_gates, gate_b):
    b, s, d = xn.shape
    heads = gate_b.shape[1]
    nrow = 4 * heads
    ln = SCAN_CHUNK
    wi_f, wf_f, wi_b, wf_b = [w_gates[:, g * heads:(g + 1) * heads] for g in range(4)]
    bi_f, bf_f, bi_b, bf_b = [gate_b[g] for g in range(4)]
    zw = jnp.zeros_like(wi_f)
    zb = jnp.zeros_like(bi_f)
    w_f = jnp.concatenate([wf_f, wf_f, wf_b, wf_b], axis=1)
    w_i = jnp.concatenate([zw, wi_f, zw, wi_b], axis=1)
    b_f = jnp.concatenate([bf_f, bf_f, bf_b, bf_b])
    b_i = jnp.concatenate([zb, bi_f, zb, bi_b])
    padc = LANE - nrow
    w_f_c = jnp.pad(w_f, ((0, 0), (0, padc))).astype(BF16)
    w_i_c = jnp.pad(w_i, ((0, 0), (0, padc))).astype(BF16)
    b_f_c = jnp.pad(b_f, (0, padc)).reshape(1, LANE)
    b_i_c = jnp.pad(b_i, (0, padc)).reshape(1, LANE)
    w_f_r = w_f.T.astype(BF16)
    w_i_r = w_i.T.astype(BF16)
    b_f_r = b_f.reshape(nrow, 1)
    b_i_r = b_i.reshape(nrow, 1)
    full = lambda shp: pl.BlockSpec(shp, lambda bi, i: (0, 0))
    return pl.pallas_call(
        functools.partial(_gates_kernel, heads=heads),
        grid=(b, s // ln),
        in_specs=[
            pl.BlockSpec((None, ln, d), lambda bi, i: (bi, i, 0)),
            full((d, LANE)), full((d, LANE)), full((nrow, d)), full((nrow, d)),
            full((1, LANE)), full((1, LANE)), full((nrow, 1)), full((nrow, 1)),
        ],
        out_specs=[
            pl.BlockSpec((None, ln, LANE), lambda bi, i: (bi, i, 0)),
            pl.BlockSpec((None, nrow, ln), lambda bi, i: (bi, 0, i)),
        ],
        out_shape=[
            jax.ShapeDtypeStruct((b, s, LANE), F32),
            jax.ShapeDtypeStruct((b, nrow, s), F32),
        ],
        compiler_params=_cp("parallel", "parallel"),
        name="gate_prep",
    )(xn, w_f_c, w_i_c, w_f_r, w_i_r, b_f_c, b_i_c, b_f_r, b_i_r)


def _rope(t, cos, sin):
    lane = lax.broadcasted_iota(jnp.int32, t.shape, 1)
    quarter = t.shape[1] // 4
    partner = jnp.where((lane & (2 * quarter - 1)) < quarter,
                        pltpu.roll(t, t.shape[1] - quarter, axis=1),
                        pltpu.roll(t, quarter, axis=1))
    return t * cos + partner * sin


def _mlstm_dir(q, k, v, a_col, r_col, r_row, b_last, mask, c_ref, m_ref, h_ref):
    ln = q.shape[0]
    dv = v.shape[1]
    m = m_ref[...][:, 0:1]
    v_ext = jnp.concatenate([v, jnp.ones((ln, LANE), F32)], axis=1).astype(BF16)
    dlog = jnp.where(mask, a_col + r_row, NEG)
    m_row = jnp.maximum(a_col + m, jnp.max(dlog, axis=1, keepdims=True))
    w_inter = jnp.exp(a_col + m - m_row)
    dmat = jnp.exp(dlog - m_row)
    s = lax.dot_general(q, k.astype(BF16), (((1,), (1,)), ((), ())), preferred_element_type=F32)
    sc = (s * dmat).astype(BF16)
    c_ext = c_ref[...]
    num = w_inter * jnp.dot(q, c_ext.astype(BF16), preferred_element_type=F32)
    num = num + jnp.dot(sc, v_ext, preferred_element_type=F32)
    den = num[:, dv:]
    denom = jnp.maximum(jnp.abs(den), jnp.exp(-m_row))
    inv = 1.0 / denom
    h_ref[...] = num[:, :dv] * jnp.concatenate([inv] * (dv // LANE), axis=1)
    r_max = jnp.max(r_row, axis=1, keepdims=True)
    m_new = b_last + jnp.maximum(m, r_max)
    decay = jnp.exp(b_last + m - m_new)
    ke = (k * jnp.exp(b_last + r_col - m_new)).astype(BF16)
    upd = lax.dot_general(ke, v_ext, (((0,), (0,)), ((), ())), preferred_element_type=F32)
    c_ref[...] = decay * c_ext + upd
    m_ref[...] = jnp.broadcast_to(m_new, m_ref.shape)


def _mlstm_kernel(qf_ref, kf_ref, vf_ref, gcf_ref, grf_ref, qb_ref, kb_ref, vb_ref, gcb_ref, grb_ref,
                  cosf_ref, sinf_ref, cosb_ref, sinb_ref, c0f_ref, m0f_ref, c0b_ref, m0b_ref,
                  hf_ref, hb_ref, cff_ref, mff_ref, cfb_ref, mfb_ref,
                  cf_s, mf_s, cb_s, mb_s, *, heads, hpb, use_rope):
    hg = pl.program_id(1)
    i = pl.program_id(2)

    @pl.when(i == 0)
    def _():
        cf_s[...] = c0f_ref[...]
        mf_s[...] = m0f_ref[...]
        cb_s[...] = c0b_ref[...]
        mb_s[...] = m0b_ref[...]

    ln = qf_ref.shape[0]
    r_i = lax.broadcasted_iota(jnp.int32, (ln, ln), 0)
    c_i = lax.broadcasted_iota(jnp.int32, (ln, ln), 1)
    lane = lax.broadcasted_iota(jnp.int32, (ln, LANE), 1)
    rowi = lax.broadcasted_iota(jnp.int32, (4 * heads, ln), 0)
    scale = A_DK ** -0.5

    def col(g_ref, idx):
        return jnp.sum(jnp.where(lane == idx, g_ref[...], 0.0), axis=1, keepdims=True)

    def rowv(g_ref, idx):
        return jnp.sum(jnp.where(rowi == idx, g_ref[...], 0.0), axis=0, keepdims=True)

    def prep(q_ref, k_ref, cos_ref, sin_ref, hh):
        q = q_ref[:, hh * A_DK:(hh + 1) * A_DK]
        k = k_ref[:, hh * A_DK:(hh + 1) * A_DK]
        if use_rope:
            q = _rope(q, cos_ref[...], sin_ref[...])
            k = _rope(k, cos_ref[...], sin_ref[...])
        return (q * scale).astype(BF16), k

    for hh in range(hpb):
        h = hg * hpb + hh
        vs = slice(hh * A_DV, (hh + 1) * A_DV)
        q, k = prep(qf_ref, kf_ref, cosf_ref, sinf_ref, hh)
        a_col = col(gcf_ref, h)
        r_col = col(gcf_ref, heads + h)
        r_row = rowv(grf_ref, heads + h)
        b_last = a_col[ln - 1:ln, :]
        _mlstm_dir(q, k, vf_ref[:, vs], a_col, r_col, r_row, b_last, c_i <= r_i,
                   cf_s.at[hh], mf_s.at[hh], hf_ref.at[:, vs])
        q, k = prep(qb_ref, kb_ref, cosb_ref, sinb_ref, hh)
        a_col = col(gcb_ref, 2 * heads + h)
        r_col = col(gcb_ref, 3 * heads + h)
        r_row = rowv(grb_ref, 3 * heads + h)
        b_last = a_col[0:1, :]
        _mlstm_dir(q, k, vb_ref[:, vs], a_col, r_col, r_row, b_last, c_i >= r_i,
                   cb_s.at[hh], mb_s.at[hh], hb_ref.at[:, vs])

    cff_ref[...] = cf_s[...]
    mff_ref[...] = mf_s[...]
    cfb_ref[...] = cb_s[...]
    mfb_ref[...] = mb_s[...]


def mlstm(p, offs, gc, gr, rope_tabs, state_f, state_b):
    b, s, _ = p.shape
    heads = gr.shape[1] // 4
    ln = SCAN_CHUNK
    nc = s // ln
    oq, ok, ov = [o // A_DK for o in offs[:2]] + [offs[2] // A_DV]
    hpb = 1
    use_rope = rope_tabs is not None
    if not use_rope:
        rope_tabs = (jnp.zeros((s, A_DK), F32),) * 2
    cos, sin = rope_tabs

    fwd = lambda i: i
    bwd = lambda i: nc - 1 - i

    def specs(ix):
        return [
            pl.BlockSpec((None, ln, hpb * A_DK), lambda bi, h, i: (bi, ix(i), oq // hpb + h)),
            pl.BlockSpec((None, ln, hpb * A_DK), lambda bi, h, i: (bi, ix(i), ok // hpb + h)),
            pl.BlockSpec((None, ln, hpb * A_DV), lambda bi, h, i: (bi, ix(i), ov // hpb + h)),
            pl.BlockSpec((None, ln, LANE), lambda bi, h, i: (bi, ix(i), 0)),
            pl.BlockSpec((None, 4 * heads, ln), lambda bi, h, i: (bi, 0, ix(i))),
        ]

    def tab(ix):
        return pl.BlockSpec((ln, A_DK), lambda bi, h, i: (ix(i), 0))

    dce = A_DV + LANE
    c_spec = pl.BlockSpec((None, hpb, A_DK, dce), lambda bi, h, i: (bi, h, 0, 0))
    m_spec = pl.BlockSpec((None, hpb, 1, LANE), lambda bi, h, i: (bi, h, 0, 0))
    outs = pl.pallas_call(
        functools.partial(_mlstm_kernel, heads=heads, hpb=hpb, use_rope=use_rope),
        grid=(b, heads // hpb, nc),
        in_specs=specs(fwd) + specs(bwd) + [tab(fwd), tab(fwd), tab(bwd), tab(bwd),
                                            c_spec, m_spec, c_spec, m_spec],
        out_specs=[
            pl.BlockSpec((None, ln, hpb * A_DV), lambda bi, h, i: (bi, i, h)),
            pl.BlockSpec((None, ln, hpb * A_DV), lambda bi, h, i: (bi, nc - 1 - i, h)),
            c_spec, m_spec, c_spec, m_spec,
        ],
        out_shape=[
            jax.ShapeDtypeStruct((b, s, heads * A_DV), F32),
            jax.ShapeDtypeStruct((b, s, heads * A_DV), F32),
            jax.ShapeDtypeStruct((b, heads, A_DK, dce), F32),
            jax.ShapeDtypeStruct((b, heads, 1, LANE), F32),
            jax.ShapeDtypeStruct((b, heads, A_DK, dce), F32),
            jax.ShapeDtypeStruct((b, heads, 1, LANE), F32),
        ],
        scratch_shapes=[
            pltpu.VMEM((hpb, A_DK, dce), F32), pltpu.VMEM((hpb, 1, LANE), F32),
            pltpu.VMEM((hpb, A_DK, dce), F32), pltpu.VMEM((hpb, 1, LANE), F32),
        ],
        compiler_params=_cp("parallel", "parallel", "arbitrary"),
        name="mlstm",
    )(p, p, p, gc, gr, p, p, p, gc, gr, cos, sin, cos, sin,
      state_f[0], state_f[1], state_b[0], state_b[1])
    h_f, h_b, cf, mf, cb, mb = outs
    return h_f, h_b, (cf, mf), (cb, mb)


def _mlstm_out_kernel(hf_ref, hb_ref, o_ref, g_ref, y_ref):
    hs = hf_ref[...] + hb_ref[...]
    hs = hs * lax.rsqrt(jnp.mean(hs * hs, axis=-1, keepdims=True) + EPS) * g_ref[...]
    y_ref[...] = (_sigmoid(o_ref[...]) * hs).astype(y_ref.dtype)


def mlstm_out(h_f, h_b, p, off_o, hnorm_g):
    b, s, w = h_f.shape
    heads = w // A_DV
    tm = _pick(s, 1024)
    oo = off_o // A_DV
    blk = lambda off: pl.BlockSpec((None, tm, A_DV), lambda bi, i, h: (bi, i, off + h))
    return pl.pallas_call(
        _mlstm_out_kernel,
        grid=(b, s // tm, heads),
        in_specs=[blk(0), blk(0), blk(oo), pl.BlockSpec((1, A_DV), lambda bi, i, h: (0, h))],
        out_specs=blk(0),
        out_shape=jax.ShapeDtypeStruct((b, s, w), BF16),
        compiler_params=_cp("parallel", "parallel", "parallel"),
        name="mlstm_out",
    )(h_f, h_b, p, hnorm_g.reshape(1, w))


def _conv_kernel(bb_ref, bc_ref, bx_ref, pc_ref, px_ref, nc_ref, nx_ref, w_ref, y_ref):
    i = pl.program_id(1)
    last = pl.num_programs(1) - 1
    u = bc_ref[...] * bx_ref[...]
    tm = u.shape[0]
    hr = pc_ref.shape[0]
    u_prev = pc_ref[...][hr - 1:hr, :] * px_ref[...][hr - 1:hr, :]
    u_next = nc_ref[...][0:1, :] * nx_ref[...][0:1, :]
    u_prev = jnp.where(i == 0, 0.0, u_prev)
    u_next = jnp.where(i == last, 0.0, u_next)
    row = lax.broadcasted_iota(jnp.int32, u.shape, 0)
    dn = jnp.where(row == 0, u_prev, pltpu.roll(u, 1, axis=0))
    up = jnp.where(row == tm - 1, u_next, pltpu.roll(u, tm - 1, axis=0))
    w = w_ref[...]
    y = dn * w[0:1, :] + u * w[1:2, :] + up * w[2:3, :]
    y_ref[...] = (bb_ref[...] * y).astype(y_ref.dtype)


def short_conv(p, off_b, off_c, off_x, w_conv):
    b, s, _ = p.shape
    kw, width = w_conv.shape
    tn = _pick(width, 512)
    tm = _pick(s, 512)
    hr = 8
    nhb = s // hr
    ob, oc, ox = off_b // tn, off_c // tn, off_x // tn
    main = lambda o: pl.BlockSpec((None, tm, tn), lambda bi, i, j: (bi, i, o + j))
    prev = lambda o: pl.BlockSpec(
        (None, hr, tn), lambda bi, i, j: (bi, jnp.maximum(i * (tm // hr) - 1, 0), o + j))
    nxt = lambda o: pl.BlockSpec(
        (None, hr, tn), lambda bi, i, j: (bi, jnp.minimum((i + 1) * (tm // hr), nhb - 1), o + j))
    return pl.pallas_call(
        _conv_kernel,
        grid=(b, s // tm, width // tn),
        in_specs=[main(ob), main(oc), main(ox), prev(oc), prev(ox), nxt(oc), nxt(ox),
                  pl.BlockSpec((kw, tn), lambda bi, i, j: (0, j))],
        out_specs=pl.BlockSpec((None, tm, tn), lambda bi, i, j: (bi, i, j)),
        out_shape=jax.ShapeDtypeStruct((b, s, width), BF16),
        compiler_params=_cp("parallel", "parallel", "parallel"),
        name="short_conv",
    )(p, p, p, p, p, p, p, w_conv)


def _rms(t, g):
    return t * lax.rsqrt(jnp.mean(t * t, axis=-1, keepdims=True) + EPS) * g


def _natten_kernel(q_ref, k_ref, v_ref, kc_ref, vc_ref, tab_ref, gq_ref, gk_ref, o_ref, *, rows, win_r):
    j = pl.program_id(2)
    nk = NA_KROWS * GRID_W
    w0 = jnp.clip(NA_QROWS * j - (NA_KROWS - NA_QROWS) // 2, 0, rows - NA_KROWS)
    start = pl.multiple_of(w0 * GRID_W, GRID_W * 4)
    gk = gk_ref[...]
    qn = (_rms(q_ref[...], gq_ref[...]) * (C_DH ** -0.5)).astype(BF16)
    kw = _rms(k_ref[pl.ds(start, nk), :], gk).astype(BF16)
    vw = v_ref[pl.ds(start, nk), :].astype(BF16)
    kc = _rms(kc_ref[...], gk).astype(BF16)
    vc = vc_ref[...].astype(BF16)

    n_tab = tab_ref.shape[0]
    odd = (lax.broadcasted_iota(jnp.int32, (GRID_W, 2 * GRID_W), 1) >= GRID_W).astype(jnp.int32)
    off = w0 - NA_QROWS * j
    bias_rows = []
    for qr in range(NA_QROWS):
        lo = jnp.clip(NA_QROWS * j + qr - win_r // 2, 0, rows - win_r) - w0
        tiles = []
        for kp in range(NA_KROWS // 2):
            ke = 2 * kp
            idx = jnp.clip(off + ke - qr + win_r, 0, n_tab - 1)
            krow = odd + ke
            visible = (krow >= lo) & (krow < lo + win_r)
            tiles.append(jnp.where(visible, tab_ref[idx], NEG))
        bias_rows.append(jnp.concatenate(tiles, axis=1))
    bias = jnp.concatenate(bias_rows, axis=0)

    dn = (((1,), (1,)), ((), ()))
    s_loc = lax.dot_general(qn, kw, dn, preferred_element_type=F32) + bias
    s_ctx = lax.dot_general(qn, kc, dn, preferred_element_type=F32)
    m = jnp.maximum(jnp.max(s_loc, axis=1, keepdims=True), jnp.max(s_ctx, axis=1, keepdims=True))
    p_loc = jnp.exp(s_loc - m)
    p_ctx = jnp.exp(s_ctx - m)
    l = jnp.sum(p_loc, axis=1, keepdims=True) + jnp.sum(p_ctx, axis=1, keepdims=True)
    o = jnp.dot(p_loc.astype(BF16), vw, preferred_element_type=F32)
    o = o + jnp.dot(p_ctx.astype(BF16), vc, preferred_element_type=F32)
    o_ref[...] = (o / l).astype(o_ref.dtype)


def _natten_tiles(rpb):
    heads, nr, ncb = rpb.shape
    win_c = (ncb + 1) // 2
    qc = np.arange(GRID_W)[:, None]
    kc = np.arange(GRID_W)[None, :]
    col0 = np.clip(qc - win_c // 2, 0, GRID_W - win_c)
    col_ok = (kc >= col0) & (kc < col0 + win_c)
    dc = np.clip(kc - qc + win_c - 1, 0, ncb - 1)
    hot_c = (dc[:, :, None] == np.arange(ncb)).astype(np.float32)
    t = jnp.einsum('hrc,qkc->hrqk', rpb.astype(F32), hot_c, precision=HIGHEST)
    t = jnp.where(jnp.asarray(col_ok), t, NEG)
    neg = jnp.full((heads, 1, GRID_W, GRID_W), NEG, F32)
    t = jnp.concatenate([neg, t, neg], axis=1)
    return jnp.concatenate([t[:, :-1], t[:, 1:]], axis=-1)


def natten(p, pc, off_q, off_k, off_v, rpb, g_q, g_k):
    b, s, _ = p.shape
    ctx = pc.shape[1]
    heads, nr, _ = rpb.shape
    rows = s // GRID_W
    nblk = rows // NA_QROWS
    nq = NA_QROWS * GRID_W
    oq, ok, ov = off_q // C_DH, off_k // C_DH, off_v // C_DH
    tiles = _natten_tiles(rpb)
    return pl.pallas_call(
        functools.partial(_natten_kernel, rows=rows, win_r=(nr + 1) // 2),
        grid=(b, heads, nblk),
        in_specs=[
            pl.BlockSpec((None, nq, C_DH), lambda bi, h, j: (bi, j, oq + h)),
            pl.BlockSpec((None, s, C_DH), lambda bi, h, j: (bi, 0, ok + h)),
            pl.BlockSpec((None, s, C_DH), lambda bi, h, j: (bi, 0, ov + h)),
            pl.BlockSpec((None, ctx, C_DH), lambda bi, h, j: (bi, 0, ok + h)),
            pl.BlockSpec((None, ctx, C_DH), lambda bi, h, j: (bi, 0, ov + h)),
            pl.BlockSpec((None, nr + 1, GRID_W, 2 * GRID_W), lambda bi, h, j: (h, 0, 0, 0)),
            pl.BlockSpec((1, C_DH), lambda bi, h, j: (0, 0)),
            pl.BlockSpec((1, C_DH), lambda bi, h, j: (0, 0)),
        ],
        out_specs=pl.BlockSpec((None, nq, C_DH), lambda bi, h, j: (bi, j, h)),
        out_shape=jax.ShapeDtypeStruct((b, s, heads * C_DH), BF16),
        compiler_params=_cp("parallel", "parallel", "arbitrary"),
        name="natten",
    )(p, p, p, pc, pc, tiles, g_q.reshape(1, C_DH), g_k.reshape(1, C_DH))


def _ctx_attn_kernel(q_ref, k_ref, v_ref, gq_ref, gk_ref, o_ref):
    qn = (_rms(q_ref[...], gq_ref[...]) * (C_DH ** -0.5)).astype(BF16)
    kn = _rms(k_ref[...], gk_ref[...]).astype(BF16)
    s = lax.dot_general(qn, kn, (((1,), (1,)), ((), ())), preferred_element_type=F32)
    m = jnp.max(s, axis=1, keepdims=True)
    pr = jnp.exp(s - m)
    l = jnp.sum(pr, axis=1, keepdims=True)
    o = jnp.dot(pr.astype(BF16), v_ref[...].astype(BF16), preferred_element_type=F32)
    o_ref[...] = (o / l).astype(o_ref.dtype)


def ctx_attn(pc, off_q, off_k, off_v, heads, g_q, g_k):
    b, ctx, _ = pc.shape
    blk = lambda o: pl.BlockSpec((None, ctx, C_DH), lambda bi, h: (bi, 0, o // C_DH + h))
    gsp = pl.BlockSpec((1, C_DH), lambda bi, h: (0, 0))
    return pl.pallas_call(
        _ctx_attn_kernel,
        grid=(b, heads),
        in_specs=[blk(off_q), blk(off_k), blk(off_v), gsp, gsp],
        out_specs=blk(0),
        out_shape=jax.ShapeDtypeStruct((b, ctx, heads * C_DH), BF16),
        compiler_params=_cp("parallel", "parallel"),
        name="ctx_attn",
    )(pc, pc, pc, g_q.reshape(1, C_DH), g_k.reshape(1, C_DH))


def _merge_kernel(ya_ref, yb_ref, yc_ref, wa_ref, wb_ref, wc_ref, ga_ref, gb_ref, gc_ref, o_ref):
    y = _sigmoid(ga_ref[...]) * jnp.dot(ya_ref[...], wa_ref[...], preferred_element_type=F32)
    y = y + _sigmoid(gb_ref[...]) * jnp.dot(yb_ref[...], wb_ref[...], preferred_element_type=F32)
    y = y + _sigmoid(gc_ref[...]) * jnp.dot(yc_ref[...], wc_ref[...], preferred_element_type=F32)
    o_ref[...] = y.astype(o_ref.dtype)


def merge(ya, yb, yc, w_a, w_b, w_c, p, off_gate):
    b, s, k = ya.shape
    d = w_a.shape[1]
    tm = _pick(s, 512)
    tn = _pick(d, 512)
    og = off_gate // tn
    nb = d // tn
    act = pl.BlockSpec((None, tm, k), lambda bi, i, j: (bi, i, 0))
    wsp = pl.BlockSpec((k, tn), lambda bi, i, j: (0, j))
    gate = lambda g: pl.BlockSpec((None, tm, tn), lambda bi, i, j: (bi, i, og + g * nb + j))
    return pl.pallas_call(
        _merge_kernel,
        grid=(b, s // tm, nb),
        in_specs=[act, act, act, wsp, wsp, wsp, gate(0), gate(1), gate(2)],
        out_specs=pl.BlockSpec((None, tm, tn), lambda bi, i, j: (bi, i, j)),
        out_shape=jax.ShapeDtypeStruct((b, s, d), BF16),
        compiler_params=_cp("parallel", "parallel", "arbitrary"),
        name="merge",
    )(ya, yb, yc, w_a, w_b, w_c, p, p, p)


def _top_values(cur, k):
    n = cur.shape[0]
    idx = lax.broadcasted_iota(jnp.int32, cur.shape, 0).astype(F32)
    vals = []
    for _ in range(k):
        mx = jnp.max(cur, axis=0, keepdims=True)
        vals.append(mx)
        first = jnp.min(jnp.where(cur == mx, idx, float(n)), axis=0, keepdims=True)
        cur = jnp.where(idx == first, -jnp.inf, cur)
    return vals


SUBLANES = 8


def _sort_network(n):
    pairs = []
    p = 1
    while p < n:
        k = p
        while k >= 1:
            for j in range(k % p, n - k, 2 * k):
                for i in range(min(k, n - j - k)):
                    if (i + j) // (2 * p) == (i + j + k) // (2 * p):
                        pairs.append((i + j, i + j + k))
            k //= 2
        p *= 2
    return pairs


def _exchange(v, i, j):
    hi, lo = jnp.maximum(v[i], v[j]), jnp.minimum(v[i], v[j])
    v[i], v[j] = hi, lo


def _topk_sorted(st):
    k = PEER_TOPK
    assert st.shape[0] == k * SUBLANES
    v = [st[i * SUBLANES:(i + 1) * SUBLANES, :] for i in range(k)]
    for i, j in _sort_network(k):
        _exchange(v, i, j)
    shift = SUBLANES // 2
    while shift >= 1:
        v = [jnp.maximum(v[i], pltpu.roll(v[k - 1 - i], shift, axis=0)) for i in range(k)]
        step = k // 2
        while step >= 1:
            for i in range(k):
                if i & step == 0:
                    _exchange(v, i, i + step)
            step //= 2
        shift //= 2
    return v


def _stack_rows(vals, like):
    row = lax.broadcasted_iota(jnp.int32, like.shape, 0)
    out = jnp.full(like.shape, -jnp.inf, F32)
    for r, val in enumerate(vals):
        out = jnp.where(row == r, val, out)
    return out


def _pair_candidates(v0, v1):
    k = PEER_TOPK
    like = v0[0]
    row = lax.broadcasted_iota(jnp.int32, like.shape, 0)
    v1_rows = [_stack_rows(v1[g:g + SUBLANES], like) for g in range(0, k, SUBLANES)]
    blocks, singles = [], []
    for i in range(k):
        cnt = min(k, (k + 1) // (i + 1))
        if cnt == 1:
            singles.append(v0[i])
            continue
        for g in range(-(-cnt // SUBLANES)):
            valid = min(SUBLANES, cnt - g * SUBLANES)
            blk = v0[i] + v1_rows[g]
            blocks.append(blk if valid == SUBLANES else jnp.where(row < valid, blk, -jnp.inf))
    for g in range(0, len(singles), SUBLANES):
        blocks.append(_stack_rows(singles[g:g + SUBLANES], like) + v1[0])
    return jnp.concatenate(blocks, axis=0)


def _peer_select_kernel(q_ref, keys_ref, s0_ref, e0_ref, c_ref, e1_ref, *, heads):
    dn = (((1,), (1,)), ((), ()))
    half = keys_ref.shape[2]
    k = PEER_TOPK
    for h in range(heads):
        sides = []
        for side in range(2):
            hp = 2 * h + side
            qh = q_ref[:, hp * half:(hp + 1) * half].astype(BF16)
            st = lax.dot_general(keys_ref[hp], qh, dn, preferred_element_type=F32)
            sides.append((st, _topk_sorted(st)))
        (s0, v0), (s1, v1) = sides
        best = _top_values(_pair_candidates(v0, v1), k + 1)
        thr = 0.5 * (best[k - 1] + best[k])
        z = jnp.ones_like(thr)
        for kk in range(1, k):
            z = z + jnp.exp(best[kk] - best[0])
        inv_z = 1.0 / z
        s0_ref[h] = s0
        e0_ref[h] = jnp.where(s0 >= v0[k - 1][0:1, :], jnp.exp(s0 - v0[0][0:1, :]), 0.0) * inv_z
        c_ref[h] = thr - s1
        e1_ref[h] = jnp.where(s1 >= v1[k - 1][0:1, :], jnp.exp(s1 - v1[0][0:1, :]), 0.0)


def peer_select(q, keys):
    b, s, _ = q.shape
    heads, _, nkeys, half = keys.shape
    tt = _pick(s, 256)
    keys_b = keys.reshape(heads * 2, nkeys, half).astype(BF16)
    big = pl.BlockSpec((None, heads, nkeys, tt), lambda bi, i: (bi, 0, 0, i))
    big_shape = jax.ShapeDtypeStruct((b, heads, nkeys, s), F32)
    return pl.pallas_call(
        functools.partial(_peer_select_kernel, heads=heads),
        grid=(b, s // tt),
        in_specs=[
            pl.BlockSpec((None, tt, q.shape[2]), lambda bi, i: (bi, i, 0)),
            pl.BlockSpec((heads * 2, nkeys, half), lambda bi, i: (0, 0, 0)),
        ],
        out_specs=[big, big, big, big],
        out_shape=[big_shape] * 4,
        compiler_params=_cp("parallel", "parallel"),
        name="peer_select",
    )(q, keys_b)


def _gelu(x):
    return 0.5 * x * (1.0 + lax.erf(x * (2.0 ** -0.5)))


def _peer_dense_kernel(x_ref, u_ref, vt_ref, s0_ref, e0_ref, c_ref, e1_ref, res_ref, g_ref,
                       o_ref, st_s, wg_s, acc_s, *, heads, a_blk):
    c = pl.program_id(2)
    last = pl.num_programs(2) - 1
    nkeys = c_ref.shape[1]
    tt = x_ref.shape[0]

    @pl.when(c == 0)
    def _():
        acc_s[...] = jnp.zeros_like(acc_s)
        wg_s[...] = jnp.zeros_like(wg_s)

    pair = 2 * nkeys
    nsl = a_blk // 2
    d = vt_ref.shape[0]
    for k in range(nsl):
        rows = slice(k * pair, (k + 1) * pair)
        st_s[k] = lax.dot_general(u_ref[rows, :], x_ref[...], (((1,), (1,)), ((), ())),
                                  preferred_element_type=F32)
        acc_s[...] += jnp.dot(vt_ref[:, rows], wg_s[rows, :], preferred_element_type=F32)
    bg = 32
    for k in range(nsl):
        for tl in range(tt // LANE):
            ls = slice(tl * LANE, (tl + 1) * LANE)
            for g in range(nkeys // bg):
                bs = slice(g * bg, (g + 1) * bg)
                w = [jnp.zeros((bg, LANE), F32) for _ in range(2)]
                for h in range(heads):
                    cv = c_ref[h, bs, ls]
                    ev = e1_ref[h, bs, ls]
                    for a2 in range(2):
                        a = 2 * k + a2
                        s0r = s0_ref[h, a:a + 1, ls]
                        e0r = e0_ref[h, a:a + 1, ls]
                        w[a2] = w[a2] + jnp.where(s0r >= cv, ev * e0r, 0.0)
                for a2 in range(2):
                    r0 = a2 * nkeys + g * bg
                    act = _gelu(st_s[k, r0:r0 + bg, ls])
                    wg_s[k * pair + r0:k * pair + r0 + bg, ls] = (w[a2] * act).astype(wg_s.dtype)

    @pl.when(c == last)
    def _():
        o_ref[...] = res_ref[...] + g_ref[...] * acc_s[...].T


def peer_dense(xn, u_b, vt_b, sel, res, gate):
    b, s, d = xn.shape
    s0t, e0t, ct, e1t = sel
    heads, nkeys = s0t.shape[1], s0t.shape[2]
    n_exp = u_b.shape[0]
    a_blk = 8
    ec = a_blk * nkeys
    nchunk = n_exp // ec
    tt = _pick(s, 512)
    once = pl.Buffered(1)
    tok = lambda: pl.BlockSpec((None, tt, d), lambda bi, i, c: (bi, i, 0), pipeline_mode=once)
    cur = lambda c: jnp.minimum(c, nchunk - 1)
    prev = lambda c: jnp.maximum(c - 1, 0)
    a_sel = pl.BlockSpec((None, heads, a_blk, tt), lambda bi, i, c: (bi, 0, cur(c), i))
    b_sel = pl.BlockSpec((None, heads, nkeys, tt), lambda bi, i, c: (bi, 0, 0, i), pipeline_mode=once)
    return pl.pallas_call(
        functools.partial(_peer_dense_kernel, heads=heads, a_blk=a_blk),
        grid=(b, s // tt, nchunk + 1),
        in_specs=[tok(),
                  pl.BlockSpec((ec, d), lambda bi, i, c: (cur(c), 0)),
                  pl.BlockSpec((d, ec), lambda bi, i, c: (0, prev(c))),
                  a_sel, a_sel, b_sel, b_sel, tok(),
                  pl.BlockSpec((None, 1, d), lambda bi, i, c: (bi, 0, 0))],
        out_specs=pl.BlockSpec((None, tt, d), lambda bi, i, c: (bi, i, 0)),
        out_shape=jax.ShapeDtypeStruct((b, s, d), F32),
        scratch_shapes=[pltpu.VMEM((a_blk // 2, 2 * nkeys, tt), F32), pltpu.VMEM((ec, tt), BF16),
                        pltpu.VMEM((d, tt), F32)],
        compiler_params=_cp("parallel", "parallel", "arbitrary"),
        name="peer_dense",
    )(xn, u_b, vt_b, s0t, e0t, ct, e1t, res, gate.reshape(b, 1, d))


def _rope_tables(s, dk):
    half = dk // 2
    nf = half // 2
    inv = ROPE_BASE ** (-jnp.arange(nf, dtype=F32) / nf)
    pos = jnp.arange(s)
    ang_r = (pos // GRID_W).astype(F32)[:, None] * inv
    ang_c = (pos % GRID_W).astype(F32)[:, None] * inv
    cos = jnp.concatenate([jnp.cos(ang_r)] * 2 + [jnp.cos(ang_c)] * 2, axis=1)
    sin = jnp.concatenate([-jnp.sin(ang_r), jnp.sin(ang_r), -jnp.sin(ang_c), jnp.sin(ang_c)], axis=1)
    return cos, sin


def _init_state(b, heads, m_value):
    return (jnp.zeros((b, heads, A_DK, A_DV + LANE), F32),
            jnp.full((b, heads, 1, LANE), m_value, F32))


def kernel(x, c, ctx, c_ctx, w_ada, b_ada, norm_g, w_in, a_gate_b, a_hnorm_g, b_conv, c_qk_g, c_rpb,
           w_a_out, w_b_out, w_c_out, w_out, peer_wq, peer_keys, peer_u, peer_v):
    bsz, seq, d = x.shape
    depth = w_ada.shape[0]
    a_heads = a_gate_b.shape[2]
    c_heads = c_rpb.shape[1]

    off_ak = 0
    off_av = off_ak + a_heads * A_DK
    gate_lo = off_av + a_heads * A_DV
    gate_hi = gate_lo + 4 * a_heads
    off_ck = gate_lo
    off_cv = off_ck + c_heads * C_DH
    kv_width = off_cv + c_heads * C_DH
    off_aq = kv_width
    off_ao = off_aq + a_heads * A_DK
    off_bb = off_ao + a_heads * A_DV
    off_bc = off_bb + d
    off_bx = off_bc + d
    off_cq = off_bx + d
    off_gate = off_cq + c_heads * C_DH

    rows = -(-(bsz + 1) // 8) * 8
    cond = jnp.concatenate([c, c_ctx[None, :], jnp.zeros((rows - bsz - 1, d), F32)], axis=0)
    mods = adaln(cond, w_ada, b_ada)

    rope_tabs = _rope_tables(seq, A_DK)
    h_ctx = ctx
    for l in range(depth):
        last = l == depth - 1
        mod_x = [mods[l, :bsz, k * d:(k + 1) * d] for k in range(6)]
        mod_c = [jnp.broadcast_to(mods[l, bsz, k * d:(k + 1) * d], (bsz, d)) for k in range(6)]
        w_proj = jnp.concatenate([w_in[l, :, :gate_lo], w_in[l, :, gate_hi:]], axis=1).astype(BF16)
        w_gates = w_in[l, :, gate_lo:gate_hi]
        g_q, g_k = c_qk_g[l, 0], c_qk_g[l, 1]

        cn = norm_mod(h_ctx, norm_g[l, 0], mod_c[0], mod_c[1])
        pc = matmul(cn, w_proj, n_cols=kv_width if last else None)
        gc_c, gr_c = gate_prep(cn, w_gates, a_gate_b[l])
        offs_c = (off_ak, off_ak, off_av) if last else (off_aq, off_ak, off_av)
        ninf = _init_state(bsz, a_heads, -jnp.inf)
        _, _, state_f, state_b = mlstm(pc, offs_c, gc_c, gr_c, None, ninf, ninf)

        xn = norm_mod(x, norm_g[l, 0], mod_x[0], mod_x[1])
        p = matmul(xn, w_proj)
        gc_x, gr_x = gate_prep(xn, w_gates, a_gate_b[l])
        h_f, h_b, _, _ = mlstm(p, (off_aq, off_ak, off_av), gc_x, gr_x, rope_tabs, state_f, state_b)
        ya = mlstm_out(h_f, h_b, p, off_ao, a_hnorm_g[l])
        yb = short_conv(p, off_bb, off_bc, off_bx, b_conv[l])
        yc = natten(p, pc, off_cq, off_ck, off_cv, c_rpb[l], g_q, g_k)
        w_a, w_b, w_c, w_o = [w[l].astype(BF16) for w in (w_a_out, w_b_out, w_c_out, w_out)]
        y = merge(ya, yb, yc, w_a, w_b, w_c, p, off_gate)
        x_new = matmul(y, w_o, residual=(x, mod_x[2]))
        xn2 = norm_mod(x_new, norm_g[l, 1], mod_x[3], mod_x[4])
        wq_b = peer_wq[l].astype(BF16)
        u_b = peer_u[l].astype(BF16)
        vt_b = peer_v[l].astype(BF16).T
        sel = peer_select(matmul(xn2, wq_b), peer_keys[l])
        x_out = peer_dense(xn2, u_b, vt_b, sel, x_new, mod_x[5])

        if not last:
            zero = _init_state(bsz, a_heads, 0.0)
            hc_f, hc_b, _, _ = mlstm(pc, (off_aq, off_ak, off_av), gc_c, gr_c, None, zero, zero)
            ya_c = mlstm_out(hc_f, hc_b, pc, off_ao, a_hnorm_g[l])
            yb_c = short_conv(pc, off_bb, off_bc, off_bx, b_conv[l])
            yc_c = ctx_attn(pc, off_cq, off_ck, off_cv, c_heads, g_q, g_k)
            y_c = merge(ya_c, yb_c, yc_c, w_a, w_b, w_c, pc, off_gate)
            hc = matmul(y_c, w_o, residual=(h_ctx, mod_c[2]))
            hcn = norm_mod(hc, norm_g[l, 1], mod_c[3], mod_c[4])
            sel_c = peer_select(matmul(hcn, wq_b), peer_keys[l])
            h_ctx = peer_dense(hcn, u_b, vt_b, sel_c, hc, mod_c[5])
        x = x_out
    return x
```

```python
import functools

import jax
import jax.numpy as jnp
import numpy as np
from jax import lax
from jax.experimental import pallas as pl
from jax.experimental.pallas import tpu as pltpu

F32 = jnp.float32
BF16 = jnp.bfloat16
HIGHEST = lax.Precision.HIGHEST

GRID_W = 64
ROPE_BASE = 10000.0
EPS = 1e-6
PEER_TOPK = 16
A_DK = 128
A_DV = 256
C_DH = 128
SCAN_CHUNK = 256
NA_QROWS = 8
NA_KROWS = 16
NEG = -1e30

VMEM_LIMIT = 56 * 1024 * 1024
LANE = 128


def _cp(*sem, flags=None):
    return pltpu.CompilerParams(dimension_semantics=sem, vmem_limit_bytes=VMEM_LIMIT, flags=flags)


def _sigmoid(x):
    return 1.0 / (1.0 + jnp.exp(-x))


def _log_sigmoid(x):
    return jnp.minimum(x, 0.0) - jnp.log1p(jnp.exp(-jnp.abs(x)))


def _pick(n, pref):
    t = min(n, pref)
    while n % t:
        t //= 2
    return t


def _adaln_kernel(c_ref, w_ref, b_ref, o_ref):
    cc = c_ref[...]
    a = cc * _sigmoid(cc)
    o_ref[...] = jnp.dot(a, w_ref[...], precision=HIGHEST, preferred_element_type=F32) + b_ref[...]


def adaln(cond, w_ada, b_ada):
    nl, d, n = w_ada.shape
    rows = cond.shape[0]
    tn = _pick(n, 1024)
    return pl.pallas_call(
        _adaln_kernel,
        grid=(nl, n // tn),
        in_specs=[
            pl.BlockSpec((rows, d), lambda l, j: (0, 0)),
            pl.BlockSpec((None, d, tn), lambda l, j: (l, 0, j)),
            pl.BlockSpec((None, 1, tn), lambda l, j: (l, 0, j)),
        ],
        out_specs=pl.BlockSpec((None, rows, tn), lambda l, j: (l, 0, j)),
        out_shape=jax.ShapeDtypeStruct((nl, rows, n), F32),
        compiler_params=_cp("parallel", "parallel"),
        name="adaln",
    )(cond, w_ada, b_ada.reshape(nl, 1, n))


def _norm_mod_kernel(x_ref, g_ref, sh_ref, sc_ref, o_ref):
    x = x_ref[...]
    y = x * lax.rsqrt(jnp.mean(x * x, axis=-1, keepdims=True) + EPS) * g_ref[...]
    o_ref[...] = (y * (1.0 + sc_ref[...]) + sh_ref[...]).astype(o_ref.dtype)


def norm_mod(x, g, shift, scale):
    b, s, d = x.shape
    tm = _pick(s, 512)
    vec = pl.BlockSpec((None, 1, d), lambda bi, i: (bi, 0, 0))
    return pl.pallas_call(
        _norm_mod_kernel,
        grid=(b, s // tm),
        in_specs=[
            pl.BlockSpec((None, tm, d), lambda bi, i: (bi, i, 0)),
            pl.BlockSpec((1, d), lambda bi, i: (0, 0)),
            vec, vec,
        ],
        out_specs=pl.BlockSpec((None, tm, d), lambda bi, i: (bi, i, 0)),
        out_shape=jax.ShapeDtypeStruct((b, s, d), BF16),
        compiler_params=_cp("parallel", "parallel"),
        name="norm_mod",
    )(x, g.reshape(1, d), shift.reshape(b, 1, d), scale.reshape(b, 1, d))


def _mm_kernel(a_ref, w_ref, o_ref):
    o_ref[...] = jnp.dot(a_ref[...], w_ref[...], preferred_element_type=F32)


def _mm_res_kernel(a_ref, w_ref, x_ref, g_ref, o_ref):
    acc = jnp.dot(a_ref[...], w_ref[...], preferred_element_type=F32)
    o_ref[...] = x_ref[...] + g_ref[...] * acc


def matmul(a, w, n_cols=None, residual=None):
    b, s, k = a.shape
    n = w.shape[1] if n_cols is None else n_cols
    tm = _pick(s, 1024)
    tn = _pick(n, 1024)
    in_specs = [
        pl.BlockSpec((None, tm, k), lambda bi, i, j: (bi, i, 0)),
        pl.BlockSpec((k, tn), lambda bi, i, j: (0, j)),
    ]
    args = [a, w]
    kern = _mm_kernel
    if residual is not None:
        x, gate = residual
        in_specs += [
            pl.BlockSpec((None, tm, tn), lambda bi, i, j: (bi, i, j)),
            pl.BlockSpec((None, 1, tn), lambda bi, i, j: (bi, 0, j)),
        ]
        args += [x, gate.reshape(b, 1, n)]
        kern = _mm_res_kernel
    return pl.pallas_call(
        kern,
        grid=(b, s // tm, n // tn),
        in_specs=in_specs,
        out_specs=pl.BlockSpec((None, tm, tn), lambda bi, i, j: (bi, i, j)),
        out_shape=jax.ShapeDtypeStruct((b, s, n), F32),
        compiler_params=_cp("parallel", "parallel", "arbitrary"),
        name="proj_res" if residual is not None else "proj",
    )(*args)


def _gates_kernel(x_ref, wf_ref, wi_ref, wft_ref, wit_ref, bfc_ref, bic_ref, bfr_ref, bir_ref,
                  gc_ref, gr_ref, *, heads):
    x = x_ref[...]
    ln = x.shape[0]
    r_i = lax.broadcasted_iota(jnp.int32, (ln, ln), 0)
    c_i = lax.broadcasted_iota(jnp.int32, (ln, ln), 1)
    tril = (c_i <= r_i).astype(F32)
    triu = (c_i >= r_i).astype(F32)

    gf = jnp.dot(x, wf_ref[...], preferred_element_type=F32) + bfc_ref[...]
    gi = jnp.dot(x, wi_ref[...], preferred_element_type=F32) + bic_ref[...]
    ls = _log_sigmoid(gf)
    pre = jnp.dot(tril, ls, precision=HIGHEST, preferred_element_type=F32)
    suf = jnp.dot(triu, ls, precision=HIGHEST, preferred_element_type=F32)
    lane = lax.broadcasted_iota(jnp.int32, gf.shape, 1)
    cum = jnp.where(lane < 2 * heads, pre, suf)
    is_a = (lane < heads) | ((lane >= 2 * heads) & (lane < 3 * heads))
    gc_ref[...] = jnp.where(is_a, cum, gi - cum)

    dn = (((1,), (1,)), ((), ()))
    gft = lax.dot_general(wft_ref[...], x, dn, preferred_element_type=F32) + bfr_ref[...]
    git = lax.dot_general(wit_ref[...], x, dn, preferred_element_type=F32) + bir_ref[...]
    lst = _log_sigmoid(gft)
    pre_t = jnp.dot(lst, triu, precision=HIGHEST, preferred_element_type=F32)
    suf_t = jnp.dot(lst, tril, precision=HIGHEST, preferred_element_type=F32)
    row = lax.broadcasted_iota(jnp.int32, gft.shape, 0)
    cum_t = jnp.where(row < 2 * heads, pre_t, suf_t)
    is_a_t = (row < heads) | ((row >= 2 * heads) & (row < 3 * heads))
    gr_ref[...] = jnp.where(is_a_t, cum_t, git - cum_t)


def gate_prep(xn, w_gates, gate_b):
    b, s, d = xn.shape
    heads = gate_b.shape[1]
    nrow = 4 * heads
    ln = SCAN_CHUNK
    wi_f, wf_f, wi_b, wf_b = [w_gates[:, g * heads:(g + 1) * heads] for g in range(4)]
    bi_f, bf_f, bi_b, bf_b = [gate_b[g] for g in range(4)]
    zw = jnp.zeros_like(wi_f)
    zb = jnp.zeros_like(bi_f)
    w_f = jnp.concatenate([wf_f, wf_f, wf_b, wf_b], axis=1)
    w_i = jnp.concatenate([zw, wi_f, zw, wi_b], axis=1)
    b_f = jnp.concatenate([bf_f, bf_f, bf_b, bf_b])
    b_i = jnp.concatenate([zb, bi_f, zb, bi_b])
    padc = LANE - nrow
    w_f_c = jnp.pad(w_f, ((0, 0), (0, padc))).astype(BF16)
    w_i_c = jnp.pad(w_i, ((0, 0), (0, padc))).astype(BF16)
    b_f_c = jnp.pad(b_f, (0, padc)).reshape(1, LANE)
    b_i_c = jnp.pad(b_i, (0, padc)).reshape(1, LANE)
    w_f_r = w_f.T.astype(BF16)
    w_i_r = w_i.T.astype(BF16)
    b_f_r = b_f.reshape(nrow, 1)
    b_i_r = b_i.reshape(nrow, 1)
    full = lambda shp: pl.BlockSpec(shp, lambda bi, i: (0, 0))
    return pl.pallas_call(
        functools.partial(_gates_kernel, heads=heads),
        grid=(b, s // ln),
        in_specs=[
            pl.BlockSpec((None, ln, d), lambda bi, i: (bi, i, 0)),
            full((d, LANE)), full((d, LANE)), full((nrow, d)), full((nrow, d)),
            full((1, LANE)), full((1, LANE)), full((nrow, 1)), full((nrow, 1)),
        ],
        out_specs=[
            pl.BlockSpec((None, ln, LANE), lambda bi, i: (bi, i, 0)),
            pl.BlockSpec((None, nrow, ln), lambda bi, i: (bi, 0, i)),
        ],
        out_shape=[
            jax.ShapeDtypeStruct((b, s, LANE), F32),
            jax.ShapeDtypeStruct((b, nrow, s), F32),
        ],
        compiler_params=_cp("parallel", "parallel"),
        name="gate_prep",
    )(xn, w_f_c, w_i_c, w_f_r, w_i_r, b_f_c, b_i_c, b_f_r, b_i_r)


def _rope(t, cos, sin):
    lane = lax.broadcasted_iota(jnp.int32, t.shape, 1)
    quarter = t.shape[1] // 4
    partner = jnp.where((lane & (2 * quarter - 1)) < quarter,
                        pltpu.roll(t, t.shape[1] - quarter, axis=1),
                        pltpu.roll(t, quarter, axis=1))
    return t * cos + partner * sin


def _mlstm_dir(q, k, v, a_col, r_col, r_row, b_last, mask, c_ref, m_ref, h_ref):
    ln = q.shape[0]
    dv = v.shape[1]
    m = m_ref[...][:, 0:1]
    v_ext = jnp.concatenate([v, jnp.ones((ln, LANE), F32)], axis=1).astype(BF16)
    dlog = jnp.where(mask, a_col + r_row, NEG)
    m_row = jnp.maximum(a_col + m, jnp.max(dlog, axis=1, keepdims=True))
    w_inter = jnp.exp(a_col + m - m_row)
    dmat = jnp.exp(dlog - m_row)
    s = lax.dot_general(q, k.astype(BF16), (((1,), (1,)), ((), ())), preferred_element_type=F32)
    sc = (s * dmat).astype(BF16)
    c_ext = c_ref[...]
    num = w_inter * jnp.dot(q, c_ext.astype(BF16), preferred_element_type=F32)
    num = num + jnp.dot(sc, v_ext, preferred_element_type=F32)
    den = num[:, dv:]
    denom = jnp.maximum(jnp.abs(den), jnp.exp(-m_row))
    inv = 1.0 / denom
    h_ref[...] = num[:, :dv] * jnp.concatenate([inv] * (dv // LANE), axis=1)
    r_max = jnp.max(r_row, axis=1, keepdims=True)
    m_new = b_last + jnp.maximum(m, r_max)
    decay = jnp.exp(b_last + m - m_new)
    ke = (k * jnp.exp(b_last + r_col - m_new)).astype(BF16)
    upd = lax.dot_general(ke, v_ext, (((0,), (0,)), ((), ())), preferred_element_type=F32)
    c_ref[...] = decay * c_ext + upd
    m_ref[...] = jnp.broadcast_to(m_new, m_ref.shape)


def _mlstm_kernel(qf_ref, kf_ref, vf_ref, gcf_ref, grf_ref, qb_ref, kb_ref, vb_ref, gcb_ref, grb_ref,
                  cosf_ref, sinf_ref, cosb_ref, sinb_ref, c0f_ref, m0f_ref, c0b_ref, m0b_ref,
                  hf_ref, hb_ref, cff_ref, mff_ref, cfb_ref, mfb_ref,
                  cf_s, mf_s, cb_s, mb_s, *, heads, hpb, use_rope):
    hg = pl.program_id(1)
    i = pl.program_id(2)

    @pl.when(i == 0)
    def _():
        cf_s[...] = c0f_ref[...]
        mf_s[...] = m0f_ref[...]
        cb_s[...] = c0b_ref[...]
        mb_s[...] = m0b_ref[...]

    ln = qf_ref.shape[0]
    r_i = lax.broadcasted_iota(jnp.int32, (ln, ln), 0)
    c_i = lax.broadcasted_iota(jnp.int32, (ln, ln), 1)
    lane = lax.broadcasted_iota(jnp.int32, (ln, LANE), 1)
    rowi = lax.broadcasted_iota(jnp.int32, (4 * heads, ln), 0)
    scale = A_DK ** -0.5

    def col(g_ref, idx):
        return jnp.sum(jnp.where(lane == idx, g_ref[...], 0.0), axis=1, keepdims=True)

    def rowv(g_ref, idx):
        return jnp.sum(jnp.where(rowi == idx, g_ref[...], 0.0), axis=0, keepdims=True)

    def prep(q_ref, k_ref, cos_ref, sin_ref, hh):
        q = q_ref[:, hh * A_DK:(hh + 1) * A_DK]
        k = k_ref[:, hh * A_DK:(hh + 1) * A_DK]
        if use_rope:
            q = _rope(q, cos_ref[...], sin_ref[...])
            k = _rope(k, cos_ref[...], sin_ref[...])
        return (q * scale).astype(BF16), k

    for hh in range(hpb):
        h = hg * hpb + hh
        vs = slice(hh * A_DV, (hh + 1) * A_DV)
        q, k = prep(qf_ref, kf_ref, cosf_ref, sinf_ref, hh)
        a_col = col(gcf_ref, h)
        r_col = col(gcf_ref, heads + h)
        r_row = rowv(grf_ref, heads + h)
        b_last = a_col[ln - 1:ln, :]
        _mlstm_dir(q, k, vf_ref[:, vs], a_col, r_col, r_row, b_last, c_i <= r_i,
                   cf_s.at[hh], mf_s.at[hh], hf_ref.at[:, vs])
        q, k = prep(qb_ref, kb_ref, cosb_ref, sinb_ref, hh)
        a_col = col(gcb_ref, 2 * heads + h)
        r_col = col(gcb_ref, 3 * heads + h)
        r_row = rowv(grb_ref, 3 * heads + h)
        b_last = a_col[0:1, :]
        _mlstm_dir(q, k, vb_ref[:, vs], a_col, r_col, r_row, b_last, c_i >= r_i,
                   cb_s.at[hh], mb_s.at[hh], hb_ref.at[:, vs])

    cff_ref[...] = cf_s[...]
    mff_ref[...] = mf_s[...]
    cfb_ref[...] = cb_s[...]
    mfb_ref[...] = mb_s[...]


def mlstm(p, offs, gc, gr, rope_tabs, state_f, state_b):
    b, s, _ = p.shape
    heads = gr.shape[1] // 4
    ln = SCAN_CHUNK
    nc = s // ln
    oq, ok, ov = [o // A_DK for o in offs[:2]] + [offs[2] // A_DV]
    hpb = 2 if all(n % 2 == 0 for n in (heads, oq, ok, ov)) else 1
    use_rope = rope_tabs is not None
    if not use_rope:
        rope_tabs = (jnp.zeros((s, A_DK), F32),) * 2
    cos, sin = rope_tabs

    fwd = lambda i: i
    bwd = lambda i: nc - 1 - i

    def specs(ix):
        return [
            pl.BlockSpec((None, ln, hpb * A_DK), lambda bi, h, i: (bi, ix(i), oq // hpb + h)),
            pl.BlockSpec((None, ln, hpb * A_DK), lambda bi, h, i: (bi, ix(i), ok // hpb + h)),
            pl.BlockSpec((None, ln, hpb * A_DV), lambda bi, h, i: (bi, ix(i), ov // hpb + h)),
            pl.BlockSpec((None, ln, LANE), lambda bi, h, i: (bi, ix(i), 0)),
            pl.BlockSpec((None, 4 * heads, ln), lambda bi, h, i: (bi, 0, ix(i))),
        ]

    def tab(ix):
        return pl.BlockSpec((ln, A_DK), lambda bi, h, i: (ix(i), 0))

    dce = A_DV + LANE
    c_spec = pl.BlockSpec((None, hpb, A_DK, dce), lambda bi, h, i: (bi, h, 0, 0))
    m_spec = pl.BlockSpec((None, hpb, 1, LANE), lambda bi, h, i: (bi, h, 0, 0))
    outs = pl.pallas_call(
        functools.partial(_mlstm_kernel, heads=heads, hpb=hpb, use_rope=use_rope),
        grid=(b, heads // hpb, nc),
        in_specs=specs(fwd) + specs(bwd) + [tab(fwd), tab(fwd), tab(bwd), tab(bwd),
                                            c_spec, m_spec, c_spec, m_spec],
        out_specs=[
            pl.BlockSpec((None, ln, hpb * A_DV), lambda bi, h, i: (bi, i, h)),
            pl.BlockSpec((None, ln, hpb * A_DV), lambda bi, h, i: (bi, nc - 1 - i, h)),
            c_spec, m_spec, c_spec, m_spec,
        ],
        out_shape=[
            jax.ShapeDtypeStruct((b, s, heads * A_DV), F32),
            jax.ShapeDtypeStruct((b, s, heads * A_DV), F32),
            jax.ShapeDtypeStruct((b, heads, A_DK, dce), F32),
            jax.ShapeDtypeStruct((b, heads, 1, LANE), F32),
            jax.ShapeDtypeStruct((b, heads, A_DK, dce), F32),
            jax.ShapeDtypeStruct((b, heads, 1, LANE), F32),
        ],
        scratch_shapes=[
            pltpu.VMEM((hpb, A_DK, dce), F32), pltpu.VMEM((hpb, 1, LANE), F32),
            pltpu.VMEM((hpb, A_DK, dce), F32), pltpu.VMEM((hpb, 1, LANE), F32),
        ],
        compiler_params=_cp("parallel", "parallel", "arbitrary"),
        name="mlstm",
    )(p, p, p, gc, gr, p, p, p, gc, gr, cos, sin, cos, sin,
      state_f[0], state_f[1], state_b[0], state_b[1])
    h_f, h_b, cf, mf, cb, mb = outs
    return h_f, h_b, (cf, mf), (cb, mb)


def _mlstm_out_kernel(hf_ref, hb_ref, o_ref, g_ref, y_ref):
    hs = hf_ref[...] + hb_ref[...]
    hs = hs * lax.rsqrt(jnp.mean(hs * hs, axis=-1, keepdims=True) + EPS) * g_ref[...]
    y_ref[...] = (_sigmoid(o_ref[...]) * hs).astype(y_ref.dtype)


def mlstm_out(h_f, h_b, p, off_o, hnorm_g):
    b, s, w = h_f.shape
    heads = w // A_DV
    tm = _pick(s, 1024)
    oo = off_o // A_DV
    blk = lambda off: pl.BlockSpec((None, tm, A_DV), lambda bi, i, h: (bi, i, off + h))
    return pl.pallas_call(
        _mlstm_out_kernel,
        grid=(b, s // tm, heads),
        in_specs=[blk(0), blk(0), blk(oo), pl.BlockSpec((1, A_DV), lambda bi, i, h: (0, h))],
        out_specs=blk(0),
        out_shape=jax.ShapeDtypeStruct((b, s, w), BF16),
        compiler_params=_cp("parallel", "parallel", "parallel"),
        name="mlstm_out",
    )(h_f, h_b, p, hnorm_g.reshape(1, w))


def _conv_kernel(bb_ref, bc_ref, bx_ref, pc_ref, px_ref, nc_ref, nx_ref, w_ref, y_ref):
    i = pl.program_id(1)
    last = pl.num_programs(1) - 1
    u = bc_ref[...] * bx_ref[...]
    tm = u.shape[0]
    hr = pc_ref.shape[0]
    u_prev = pc_ref[...][hr - 1:hr, :] * px_ref[...][hr - 1:hr, :]
    u_next = nc_ref[...][0:1, :] * nx_ref[...][0:1, :]
    u_prev = jnp.where(i == 0, 0.0, u_prev)
    u_next = jnp.where(i == last, 0.0, u_next)
    row = lax.broadcasted_iota(jnp.int32, u.shape, 0)
    dn = jnp.where(row == 0, u_prev, pltpu.roll(u, 1, axis=0))
    up = jnp.where(row == tm - 1, u_next, pltpu.roll(u, tm - 1, axis=0))
    w = w_ref[...]
    y = dn * w[0:1, :] + u * w[1:2, :] + up * w[2:3, :]
    y_ref[...] = (bb_ref[...] * y).astype(y_ref.dtype)


def short_conv(p, off_b, off_c, off_x, w_conv):
    b, s, _ = p.shape
    kw, width = w_conv.shape
    tn = _pick(width, 512)
    tm = _pick(s, 512)
    hr = 8
    nhb = s // hr
    ob, oc, ox = off_b // tn, off_c // tn, off_x // tn
    main = lambda o: pl.BlockSpec((None, tm, tn), lambda bi, i, j: (bi, i, o + j))
    prev = lambda o: pl.BlockSpec(
        (None, hr, tn), lambda bi, i, j: (bi, jnp.maximum(i * (tm // hr) - 1, 0), o + j))
    nxt = lambda o: pl.BlockSpec(
        (None, hr, tn), lambda bi, i, j: (bi, jnp.minimum((i + 1) * (tm // hr), nhb - 1), o + j))
    return pl.pallas_call(
        _conv_kernel,
        grid=(b, s // tm, width // tn),
        in_specs=[main(ob), main(oc), main(ox), prev(oc), prev(ox), nxt(oc), nxt(ox),
                  pl.BlockSpec((kw, tn), lambda bi, i, j: (0, j))],
        out_specs=pl.BlockSpec((None, tm, tn), lambda bi, i, j: (bi, i, j)),
        out_shape=jax.ShapeDtypeStruct((b, s, width), BF16),
        compiler_params=_cp("parallel", "parallel", "parallel"),
        name="short_conv",
    )(p, p, p, p, p, p, p, w_conv)


def _rms(t, g):
    return t * lax.rsqrt(jnp.mean(t * t, axis=-1, keepdims=True) + EPS) * g


def _natten_kernel(q_ref, k_ref, v_ref, kc_ref, vc_ref, tab_ref, gq_ref, gk_ref, o_ref, *, rows, win_r):
    j = pl.program_id(2)
    nk = NA_KROWS * GRID_W
    w0 = jnp.clip(NA_QROWS * j - (NA_KROWS - NA_QROWS) // 2, 0, rows - NA_KROWS)
    start = pl.multiple_of(w0 * GRID_W, GRID_W * 4)
    gk = gk_ref[...]
    qn = (_rms(q_ref[...], gq_ref[...]) * (C_DH ** -0.5)).astype(BF16)
    kw = _rms(k_ref[pl.ds(start, nk), :], gk).astype(BF16)
    vw = v_ref[pl.ds(start, nk), :].astype(BF16)
    kc = _rms(kc_ref[...], gk).astype(BF16)
    vc = vc_ref[...].astype(BF16)

    n_tab = tab_ref.shape[0]
    odd = (lax.broadcasted_iota(jnp.int32, (GRID_W, 2 * GRID_W), 1) >= GRID_W).astype(jnp.int32)
    off = w0 - NA_QROWS * j
    bias_rows = []
    for qr in range(NA_QROWS):
        lo = jnp.clip(NA_QROWS * j + qr - win_r // 2, 0, rows - win_r) - w0
        tiles = []
        for kp in range(NA_KROWS // 2):
            ke = 2 * kp
            idx = jnp.clip(off + ke - qr + win_r, 0, n_tab - 1)
            krow = odd + ke
            visible = (krow >= lo) & (krow < lo + win_r)
            tiles.append(jnp.where(visible, tab_ref[idx], NEG))
        bias_rows.append(jnp.concatenate(tiles, axis=1))
    bias = jnp.concatenate(bias_rows, axis=0)

    dn = (((1,), (1,)), ((), ()))
    s_loc = lax.dot_general(qn, kw, dn, preferred_element_type=F32) + bias
    s_ctx = lax.dot_general(qn, kc, dn, preferred_element_type=F32)
    m = jnp.maximum(jnp.max(s_loc, axis=1, keepdims=True), jnp.max(s_ctx, axis=1, keepdims=True))
    p_loc = jnp.exp(s_loc - m)
    p_ctx = jnp.exp(s_ctx - m)
    l = jnp.sum(p_loc, axis=1, keepdims=True) + jnp.sum(p_ctx, axis=1, keepdims=True)
    o = jnp.dot(p_loc.astype(BF16), vw, preferred_element_type=F32)
    o = o + jnp.dot(p_ctx.astype(BF16), vc, preferred_element_type=F32)
    o_ref[...] = (o / l).astype(o_ref.dtype)


def _natten_tiles(rpb):
    heads, nr, ncb = rpb.shape
    win_c = (ncb + 1) // 2
    qc = np.arange(GRID_W)[:, None]
    kc = np.arange(GRID_W)[None, :]
    col0 = np.clip(qc - win_c // 2, 0, GRID_W - win_c)
    col_ok = (kc >= col0) & (kc < col0 + win_c)
    dc = np.clip(kc - qc + win_c - 1, 0, ncb - 1)
    hot_c = (dc[:, :, None] == np.arange(ncb)).astype(np.float32)
    t = jnp.einsum('hrc,qkc->hrqk', rpb.astype(F32), hot_c, precision=HIGHEST)
    t = jnp.where(jnp.asarray(col_ok), t, NEG)
    neg = jnp.full((heads, 1, GRID_W, GRID_W), NEG, F32)
    t = jnp.concatenate([neg, t, neg], axis=1)
    return jnp.concatenate([t[:, :-1], t[:, 1:]], axis=-1)


def natten(p, pc, off_q, off_k, off_v, rpb, g_q, g_k):
    b, s, _ = p.shape
    ctx = pc.shape[1]
    heads, nr, _ = rpb.shape
    rows = s // GRID_W
    nblk = rows // NA_QROWS
    nq = NA_QROWS * GRID_W
    oq, ok, ov = off_q // C_DH, off_k // C_DH, off_v // C_DH
    tiles = _natten_tiles(rpb)
    return pl.pallas_call(
        functools.partial(_natten_kernel, rows=rows, win_r=(nr + 1) // 2),
        grid=(b, heads, nblk),
        in_specs=[
            pl.BlockSpec((None, nq, C_DH), lambda bi, h, j: (bi, j, oq + h)),
            pl.BlockSpec((None, s, C_DH), lambda bi, h, j: (bi, 0, ok + h)),
            pl.BlockSpec((None, s, C_DH), lambda bi, h, j: (bi, 0, ov + h)),
            pl.BlockSpec((None, ctx, C_DH), lambda bi, h, j: (bi, 0, ok + h)),
            pl.BlockSpec((None, ctx, C_DH), lambda bi, h, j: (bi, 0, ov + h)),
            pl.BlockSpec((None, nr + 1, GRID_W, 2 * GRID_W), lambda bi, h, j: (h, 0, 0, 0)),
            pl.BlockSpec((1, C_DH), lambda bi, h, j: (0, 0)),
            pl.BlockSpec((1, C_DH), lambda bi, h, j: (0, 0)),
        ],
        out_specs=pl.BlockSpec((None, nq, C_DH), lambda bi, h, j: (bi, j, h)),
        out_shape=jax.ShapeDtypeStruct((b, s, heads * C_DH), BF16),
        compiler_params=_cp("parallel", "parallel", "arbitrary"),
        name="natten",
    )(p, p, p, pc, pc, tiles, g_q.reshape(1, C_DH), g_k.reshape(1, C_DH))


def _ctx_attn_kernel(q_ref, k_ref, v_ref, gq_ref, gk_ref, o_ref):
    qn = (_rms(q_ref[...], gq_ref[...]) * (C_DH ** -0.5)).astype(BF16)
    kn = _rms(k_ref[...], gk_ref[...]).astype(BF16)
    s = lax.dot_general(qn, kn, (((1,), (1,)), ((), ())), preferred_element_type=F32)
    m = jnp.max(s, axis=1, keepdims=True)
    pr = jnp.exp(s - m)
    l = jnp.sum(pr, axis=1, keepdims=True)
    o = jnp.dot(pr.astype(BF16), v_ref[...].astype(BF16), preferred_element_type=F32)
    o_ref[...] = (o / l).astype(o_ref.dtype)


def ctx_attn(pc, off_q, off_k, off_v, heads, g_q, g_k):
    b, ctx, _ = pc.shape
    blk = lambda o: pl.BlockSpec((None, ctx, C_DH), lambda bi, h: (bi, 0, o // C_DH + h))
    gsp = pl.BlockSpec((1, C_DH), lambda bi, h: (0, 0))
    return pl.pallas_call(
        _ctx_attn_kernel,
        grid=(b, heads),
        in_specs=[blk(off_q), blk(off_k), blk(off_v), gsp, gsp],
        out_specs=blk(0),
        out_shape=jax.ShapeDtypeStruct((b, ctx, heads * C_DH), BF16),
        compiler_params=_cp("parallel", "parallel"),
        name="ctx_attn",
    )(pc, pc, pc, g_q.reshape(1, C_DH), g_k.reshape(1, C_DH))


def _merge_kernel(ya_ref, yb_ref, yc_ref, wa_ref, wb_ref, wc_ref, ga_ref, gb_ref, gc_ref, o_ref):
    y = _sigmoid(ga_ref[...]) * jnp.dot(ya_ref[...], wa_ref[...], preferred_element_type=F32)
    y = y + _sigmoid(gb_ref[...]) * jnp.dot(yb_ref[...], wb_ref[...], preferred_element_type=F32)
    y = y + _sigmoid(gc_ref[...]) * jnp.dot(yc_ref[...], wc_ref[...], preferred_element_type=F32)
    o_ref[...] = y.astype(o_ref.dtype)


def merge(ya, yb, yc, w_a, w_b, w_c, p, off_gate):
    b, s, k = ya.shape
    d = w_a.shape[1]
    tm = _pick(s, 512)
    tn = _pick(d, 512)
    og = off_gate // tn
    nb = d // tn
    act = pl.BlockSpec((None, tm, k), lambda bi, i, j: (bi, i, 0))
    wsp = pl.BlockSpec((k, tn), lambda bi, i, j: (0, j))
    gate = lambda g: pl.BlockSpec((None, tm, tn), lambda bi, i, j: (bi, i, og + g * nb + j))
    return pl.pallas_call(
        _merge_kernel,
        grid=(b, s // tm, nb),
        in_specs=[act, act, act, wsp, wsp, wsp, gate(0), gate(1), gate(2)],
        out_specs=pl.BlockSpec((None, tm, tn), lambda bi, i, j: (bi, i, j)),
        out_shape=jax.ShapeDtypeStruct((b, s, d), BF16),
        compiler_params=_cp("parallel", "parallel", "arbitrary"),
        name="merge",
    )(ya, yb, yc, w_a, w_b, w_c, p, p, p)


def _top_values(cur, k):
    n = cur.shape[0]
    idx = lax.broadcasted_iota(jnp.int32, cur.shape, 0).astype(F32)
    vals = []
    for _ in range(k):
        mx = jnp.max(cur, axis=0, keepdims=True)
        vals.append(mx)
        first = jnp.min(jnp.where(cur == mx, idx, float(n)), axis=0, keepdims=True)
        cur = jnp.where(idx == first, -jnp.inf, cur)
    return vals


SUBLANES = 8


def _sort_network(n):
    pairs = []
    p = 1
    while p < n:
        k = p
        while k >= 1:
            for j in range(k % p, n - k, 2 * k):
                for i in range(min(k, n - j - k)):
                    if (i + j) // (2 * p) == (i + j + k) // (2 * p):
                        pairs.append((i + j, i + j + k))
            k //= 2
        p *= 2
    return pairs


def _exchange(v, i, j):
    hi, lo = jnp.maximum(v[i], v[j]), jnp.minimum(v[i], v[j])
    v[i], v[j] = hi, lo


def _topk_sorted(st):
    k = PEER_TOPK
    assert st.shape[0] == k * SUBLANES
    v = [st[i * SUBLANES:(i + 1) * SUBLANES, :] for i in range(k)]
    for i, j in _sort_network(k):
        _exchange(v, i, j)
    shift = SUBLANES // 2
    while shift >= 1:
        v = [jnp.maximum(v[i], pltpu.roll(v[k - 1 - i], shift, axis=0)) for i in range(k)]
        step = k // 2
        while step >= 1:
            for i in range(k):
                if i & step == 0:
                    _exchange(v, i, i + step)
            step //= 2
        shift //= 2
    return v


def _stack_rows(vals, like):
    row = lax.broadcasted_iota(jnp.int32, like.shape, 0)
    out = jnp.full(like.shape, -jnp.inf, F32)
    for r, val in enumerate(vals):
        out = jnp.where(row == r, val, out)
    return out


def _pair_candidates(v0, v1):
    k = PEER_TOPK
    like = v0[0]
    row = lax.broadcasted_iota(jnp.int32, like.shape, 0)
    v1_rows = [_stack_rows(v1[g:g + SUBLANES], like) for g in range(0, k, SUBLANES)]
    blocks, singles = [], []
    for i in range(k):
        cnt = min(k, (k + 1) // (i + 1))
        if cnt == 1:
            singles.append(v0[i])
            continue
        for g in range(-(-cnt // SUBLANES)):
            valid = min(SUBLANES, cnt - g * SUBLANES)
            blk = v0[i] + v1_rows[g]
            blocks.append(blk if valid == SUBLANES else jnp.where(row < valid, blk, -jnp.inf))
    for g in range(0, len(singles), SUBLANES):
        blocks.append(_stack_rows(singles[g:g + SUBLANES], like) + v1[0])
    return jnp.concatenate(blocks, axis=0)


def _peer_select_kernel(q_ref, keys_ref, s0_ref, e0_ref, c_ref, e1_ref, *, heads):
    dn = (((1,), (1,)), ((), ()))
    half = keys_ref.shape[2]
    k = PEER_TOPK
    for h in range(heads):
        sides = []
        for side in range(2):
            hp = 2 * h + side
            qh = q_ref[:, hp * half:(hp + 1) * half].astype(BF16)
            st = lax.dot_general(keys_ref[hp], qh, dn, preferred_element_type=F32)
            sides.append((st, _topk_sorted(st)))
        (s0, v0), (s1, v1) = sides
        best = _top_values(_pair_candidates(v0, v1), k + 1)
        thr = 0.5 * (best[k - 1] + best[k])
        z = jnp.ones_like(thr)
        for kk in range(1, k):
            z = z + jnp.exp(best[kk] - best[0])
        inv_z = 1.0 / z
        s0_ref[h] = s0
        e0_ref[h] = jnp.where(s0 >= v0[k - 1][0:1, :], jnp.exp(s0 - v0[0][0:1, :]), 0.0) * inv_z
        c_ref[h] = thr - s1
        e1_ref[h] = jnp.where(s1 >= v1[k - 1][0:1, :], jnp.exp(s1 - v1[0][0:1, :]), 0.0)


def peer_select(q, keys):
    b, s, _ = q.shape
    heads, _, nkeys, half = keys.shape
    tt = _pick(s, 256)
    keys_b = keys.reshape(heads * 2, nkeys, half).astype(BF16)
    big = pl.BlockSpec((None, heads, nkeys, tt), lambda bi, i: (bi, 0, 0, i))
    big_shape = jax.ShapeDtypeStruct((b, heads, nkeys, s), F32)
    return pl.pallas_call(
        functools.partial(_peer_select_kernel, heads=heads),
        grid=(b, s // tt),
        in_specs=[
            pl.BlockSpec((None, tt, q.shape[2]), lambda bi, i: (bi, i, 0)),
            pl.BlockSpec((heads * 2, nkeys, half), lambda bi, i: (0, 0, 0)),
        ],
        out_specs=[big, big, big, big],
        out_shape=[big_shape] * 4,
        compiler_params=_cp("parallel", "parallel"),
        name="peer_select",
    )(q, keys_b)


def _gelu(x):
    return 0.5 * x * (1.0 + lax.erf(x * (2.0 ** -0.5)))


def _cast_t_kernel(x_ref, o_ref):
    o_ref[...] = x_ref[...].T.astype(o_ref.dtype)


def cast_transpose(x):
    n, d = x.shape
    tn = _pick(n, 512)
    return pl.pallas_call(
        _cast_t_kernel,
        grid=(n // tn,),
        in_specs=[pl.BlockSpec((tn, d), lambda i: (i, 0))],
        out_specs=pl.BlockSpec((d, tn), lambda i: (0, i)),
        out_shape=jax.ShapeDtypeStruct((d, n), BF16),
        compiler_params=_cp("parallel"),
        name="cast_transpose",
    )(x)


def _peer_dense_kernel(x_ref, u_ref, vt_ref, s0_ref, e0_ref, c_ref, e1_ref, res_ref, g_ref,
                       o_ref, st_s, wg_s, acc_s, *, heads, a_blk, nchunk):
    t = pl.program_id(1)
    applied = jnp.maximum(t - 1, 0) % nchunk
    nkeys = c_ref.shape[1]
    tt = x_ref.shape[0]

    @pl.when(t == 0)
    def _():
        wg_s[...] = jnp.zeros_like(wg_s)

    @pl.when(applied == 0)
    def _():
        acc_s[...] = jnp.zeros_like(acc_s)

    pair = 2 * nkeys
    nsl = a_blk // 2
    for k in range(nsl):
        rows = slice(k * pair, (k + 1) * pair)
        st_s[k] = lax.dot_general(u_ref[rows, :], x_ref[...], (((1,), (1,)), ((), ())),
                                  preferred_element_type=F32)
        acc_s[...] += jnp.dot(vt_ref[:, rows], wg_s[rows, :], preferred_element_type=F32)
    bg = 32
    for k in range(nsl):
        for tl in range(tt // LANE):
            ls = slice(tl * LANE, (tl + 1) * LANE)
            for g in range(nkeys // bg):
                bs = slice(g * bg, (g + 1) * bg)
                w = [jnp.zeros((bg, LANE), F32) for _ in range(2)]
                for h in range(heads):
                    cv = c_ref[h, bs, ls]
                    ev = e1_ref[h, bs, ls]
                    for a2 in range(2):
                        a = 2 * k + a2
                        s0r = s0_ref[h, a:a + 1, ls]
                        e0r = e0_ref[h, a:a + 1, ls]
                        w[a2] = w[a2] + jnp.where(s0r >= cv, ev * e0r, 0.0)
                for a2 in range(2):
                    r0 = a2 * nkeys + g * bg
                    act = _gelu(st_s[k, r0:r0 + bg, ls])
                    wg_s[k * pair + r0:k * pair + r0 + bg, ls] = (w[a2] * act).astype(wg_s.dtype)

    @pl.when((applied == nchunk - 1) & (t > 0))
    def _():
        o_ref[...] = res_ref[...] + g_ref[...] * acc_s[...].T


def peer_dense(xn, u_b, vt_b, sel, res, gate):
    b, s, d = xn.shape
    s0t, e0t, ct, e1t = sel
    heads, nkeys = s0t.shape[1], s0t.shape[2]
    n_exp = u_b.shape[0]
    a_blk = 8
    ec = a_blk * nkeys
    nchunk = n_exp // ec
    tt = _pick(s, 512)
    n_pairs = (s // tt) * nchunk
    once = pl.Buffered(1)
    scored = lambda t: jnp.minimum(t, n_pairs - 1)
    applied = lambda t: jnp.maximum(t - 1, 0)
    a_sel = pl.BlockSpec((None, heads, a_blk, tt),
                         lambda bi, t: (bi, 0, scored(t) % nchunk, scored(t) // nchunk))
    b_sel = pl.BlockSpec((None, heads, nkeys, tt), lambda bi, t: (bi, 0, 0, scored(t) // nchunk),
                         pipeline_mode=once)
    return pl.pallas_call(
        functools.partial(_peer_dense_kernel, heads=heads, a_blk=a_blk, nchunk=nchunk),
        grid=(b, n_pairs + 1),
        in_specs=[pl.BlockSpec((None, tt, d), lambda bi, t: (bi, scored(t) // nchunk, 0), pipeline_mode=once),
                  pl.BlockSpec((ec, d), lambda bi, t: (scored(t) % nchunk, 0)),
                  pl.BlockSpec((d, ec), lambda bi, t: (0, applied(t) % nchunk)),
                  a_sel, a_sel, b_sel, b_sel,
                  pl.BlockSpec((None, tt, d), lambda bi, t: (bi, applied(t) // nchunk, 0), pipeline_mode=once),
                  pl.BlockSpec((None, 1, d), lambda bi, t: (bi, 0, 0))],
        out_specs=pl.BlockSpec((None, tt, d), lambda bi, t: (bi, applied(t) // nchunk, 0)),
        out_shape=jax.ShapeDtypeStruct((b, s, d), F32),
        scratch_shapes=[pltpu.VMEM((a_blk // 2, 2 * nkeys, tt), F32), pltpu.VMEM((ec, tt), BF16),
                        pltpu.VMEM((d, tt), F32)],
        compiler_params=_cp("parallel", "arbitrary"),
        name="peer_dense",
    )(xn, u_b, vt_b, s0t, e0t, ct, e1t, res, gate.reshape(b, 1, d))


def _rope_tables(s, dk):
    half = dk // 2
    nf = half // 2
    inv = ROPE_BASE ** (-jnp.arange(nf, dtype=F32) / nf)
    pos = jnp.arange(s)
    ang_r = (pos // GRID_W).astype(F32)[:, None] * inv
    ang_c = (pos % GRID_W).astype(F32)[:, None] * inv
    cos = jnp.concatenate([jnp.cos(ang_r)] * 2 + [jnp.cos(ang_c)] * 2, axis=1)
    sin = jnp.concatenate([-jnp.sin(ang_r), jnp.sin(ang_r), -jnp.sin(ang_c), jnp.sin(ang_c)], axis=1)
    return cos, sin


def _init_state(b, heads, m_value):
    return (jnp.zeros((b, heads, A_DK, A_DV + LANE), F32),
            jnp.full((b, heads, 1, LANE), m_value, F32))


def kernel(x, c, ctx, c_ctx, w_ada, b_ada, norm_g, w_in, a_gate_b, a_hnorm_g, b_conv, c_qk_g, c_rpb,
           w_a_out, w_b_out, w_c_out, w_out, peer_wq, peer_keys, peer_u, peer_v):
    bsz, seq, d = x.shape
    depth = w_ada.shape[0]
    a_heads = a_gate_b.shape[2]
    c_heads = c_rpb.shape[1]

    off_ak = 0
    off_av = off_ak + a_heads * A_DK
    gate_lo = off_av + a_heads * A_DV
    gate_hi = gate_lo + 4 * a_heads
    off_ck = gate_lo
    off_cv = off_ck + c_heads * C_DH
    kv_width = off_cv + c_heads * C_DH
    off_aq = kv_width
    off_ao = off_aq + a_heads * A_DK
    off_bb = off_ao + a_heads * A_DV
    off_bc = off_bb + d
    off_bx = off_bc + d
    off_cq = off_bx + d
    off_gate = off_cq + c_heads * C_DH

    rows = -(-(bsz + 1) // 8) * 8
    cond = jnp.concatenate([c, c_ctx[None, :], jnp.zeros((rows - bsz - 1, d), F32)], axis=0)
    mods = adaln(cond, w_ada, b_ada)

    rope_tabs = _rope_tables(seq, A_DK)
    h_ctx = ctx
    for l in range(depth):
        last = l == depth - 1
        mod_x = [mods[l, :bsz, k * d:(k + 1) * d] for k in range(6)]
        mod_c = [jnp.broadcast_to(mods[l, bsz, k * d:(k + 1) * d], (bsz, d)) for k in range(6)]
        w_proj = jnp.concatenate([w_in[l, :, :gate_lo], w_in[l, :, gate_hi:]], axis=1).astype(BF16)
        w_gates = w_in[l, :, gate_lo:gate_hi]
        g_q, g_k = c_qk_g[l, 0], c_qk_g[l, 1]

        cn = norm_mod(h_ctx, norm_g[l, 0], mod_c[0], mod_c[1])
        pc = matmul(cn, w_proj, n_cols=kv_width if last else None)
        gc_c, gr_c = gate_prep(cn, w_gates, a_gate_b[l])
        offs_c = (off_ak, off_ak, off_av) if last else (off_aq, off_ak, off_av)
        ninf = _init_state(bsz, a_heads, -jnp.inf)
        _, _, state_f, state_b = mlstm(pc, offs_c, gc_c, gr_c, None, ninf, ninf)

        xn = norm_mod(x, norm_g[l, 0], mod_x[0], mod_x[1])
        p = matmul(xn, w_proj)
        gc_x, gr_x = gate_prep(xn, w_gates, a_gate_b[l])
        h_f, h_b, _, _ = mlstm(p, (off_aq, off_ak, off_av), gc_x, gr_x, rope_tabs, state_f, state_b)
        ya = mlstm_out(h_f, h_b, p, off_ao, a_hnorm_g[l])
        yb = short_conv(p, off_bb, off_bc, off_bx, b_conv[l])
        yc = natten(p, pc, off_cq, off_ck, off_cv, c_rpb[l], g_q, g_k)
        w_a, w_b, w_c, w_o = [w[l].astype(BF16) for w in (w_a_out, w_b_out, w_c_out, w_out)]
        y = merge(ya, yb, yc, w_a, w_b, w_c, p, off_gate)
        x_new = matmul(y, w_o, residual=(x, mod_x[2]))
        xn2 = norm_mod(x_new, norm_g[l, 1], mod_x[3], mod_x[4])
        wq_b = peer_wq[l].astype(BF16)
        u_b = peer_u[l].astype(BF16)
        vt_b = cast_transpose(peer_v[l])
        sel = peer_select(matmul(xn2, wq_b), peer_keys[l])
        x_out = peer_dense(xn2, u_b, vt_b, sel, x_new, mod_x[5])

        if not last:
            zero = _init_state(bsz, a_heads, 0.0)
            hc_f, hc_b, _, _ = mlstm(pc, (off_aq, off_ak, off_av), gc_c, gr_c, None, zero, zero)
            ya_c = mlstm_out(hc_f, hc_b, pc, off_ao, a_hnorm_g[l])
            yb_c = short_conv(pc, off_bb, off_bc, off_bx, b_conv[l])
            yc_c = ctx_attn(pc, off_cq, off_ck, off_cv, c_heads, g_q, g_k)
            y_c = merge(ya_c, yb_c, yc_c, w_a, w_b, w_c, pc, off_gate)
            hc = matmul(y_c, w_o, residual=(h_ctx, mod_c[2]))
            hcn = norm_mod(hc, norm_g[l, 1], mod_c[3], mod_c[4])
            sel_c = peer_select(matmul(hcn, wq_b), peer_keys[l])
            h_ctx = peer_dense(hcn, u_b, vt_b, sel_c, hc, mod_c[5])
        x = x_out
    return x
```

```python
import functools

import jax
import jax.numpy as jnp
import numpy as np
from jax import lax
from jax.experimental import pallas as pl
from jax.experimental.pallas import tpu as pltpu

F32 = jnp.float32
BF16 = jnp.bfloat16
HIGHEST = lax.Precision.HIGHEST

GRID_W = 64
ROPE_BASE = 10000.0
EPS = 1e-6
PEER_TOPK = 16
A_DK = 128
A_DV = 256
C_DH = 128
SCAN_CHUNK = 256
NA_QROWS = 8
NA_KROWS = 16
NA_QPARTS = 2
NEG = -1e30

VMEM_LIMIT = 56 * 1024 * 1024
LANE = 128


def _cp(*sem, flags=None):
    return pltpu.CompilerParams(dimension_semantics=sem, vmem_limit_bytes=VMEM_LIMIT, flags=flags)


def _sigmoid(x):
    return 1.0 / (1.0 + jnp.exp(-x))


def _log_sigmoid(x):
    return jnp.minimum(x, 0.0) - jnp.log1p(jnp.exp(-jnp.abs(x)))


def _pick(n, pref):
    t = min(n, pref)
    while n % t:
        t //= 2
    return t


def _adaln_kernel(c_ref, w_ref, b_ref, o_ref):
    cc = c_ref[...]
    a = cc * _sigmoid(cc)
    o_ref[...] = jnp.dot(a, w_ref[...], precision=HIGHEST, preferred_element_type=F32) + b_ref[...]


def adaln(cond, w_ada, b_ada):
    nl, d, n = w_ada.shape
    rows = cond.shape[0]
    tn = _pick(n, 1024)
    return pl.pallas_call(
        _adaln_kernel,
        grid=(nl, n // tn),
        in_specs=[
            pl.BlockSpec((rows, d), lambda l, j: (0, 0)),
            pl.BlockSpec((None, d, tn), lambda l, j: (l, 0, j)),
            pl.BlockSpec((None, 1, tn), lambda l, j: (l, 0, j)),
        ],
        out_specs=pl.BlockSpec((None, rows, tn), lambda l, j: (l, 0, j)),
        out_shape=jax.ShapeDtypeStruct((nl, rows, n), F32),
        compiler_params=_cp("parallel", "parallel"),
        name="adaln",
    )(cond, w_ada, b_ada.reshape(nl, 1, n))


def _norm_mod_kernel(x_ref, g_ref, sh_ref, sc_ref, o_ref):
    x = x_ref[...]
    y = x * lax.rsqrt(jnp.mean(x * x, axis=-1, keepdims=True) + EPS) * g_ref[...]
    o_ref[...] = (y * (1.0 + sc_ref[...]) + sh_ref[...]).astype(o_ref.dtype)


def norm_mod(x, g, shift, scale):
    b, s, d = x.shape
    tm = _pick(s, 512)
    vec = pl.BlockSpec((None, 1, d), lambda bi, i: (bi, 0, 0))
    return pl.pallas_call(
        _norm_mod_kernel,
        grid=(b, s // tm),
        in_specs=[
            pl.BlockSpec((None, tm, d), lambda bi, i: (bi, i, 0)),
            pl.BlockSpec((1, d), lambda bi, i: (0, 0)),
            vec, vec,
        ],
        out_specs=pl.BlockSpec((None, tm, d), lambda bi, i: (bi, i, 0)),
        out_shape=jax.ShapeDtypeStruct((b, s, d), BF16),
        compiler_params=_cp("parallel", "parallel"),
        name="norm_mod",
    )(x, g.reshape(1, d), shift.reshape(b, 1, d), scale.reshape(b, 1, d))


def _mm_kernel(a_ref, w_ref, o_ref):
    o_ref[...] = jnp.dot(a_ref[...], w_ref[...], preferred_element_type=F32)


def _mm_res_kernel(a_ref, w_ref, x_ref, g_ref, o_ref):
    acc = jnp.dot(a_ref[...], w_ref[...], preferred_element_type=F32)
    o_ref[...] = x_ref[...] + g_ref[...] * acc


def matmul(a, w, n_cols=None, residual=None):
    b, s, k = a.shape
    n = w.shape[1] if n_cols is None else n_cols
    tm = _pick(s, 1024)
    tn = _pick(n, 1024)
    in_specs = [
        pl.BlockSpec((None, tm, k), lambda bi, i, j: (bi, i, 0)),
        pl.BlockSpec((k, tn), lambda bi, i, j: (0, j)),
    ]
    args = [a, w]
    kern = _mm_kernel
    if residual is not None:
        x, gate = residual
        in_specs += [
            pl.BlockSpec((None, tm, tn), lambda bi, i, j: (bi, i, j)),
            pl.BlockSpec((None, 1, tn), lambda bi, i, j: (bi, 0, j)),
        ]
        args += [x, gate.reshape(b, 1, n)]
        kern = _mm_res_kernel
    return pl.pallas_call(
        kern,
        grid=(b, s // tm, n // tn),
        in_specs=in_specs,
        out_specs=pl.BlockSpec((None, tm, tn), lambda bi, i, j: (bi, i, j)),
        out_shape=jax.ShapeDtypeStruct((b, s, n), F32),
        compiler_params=_cp("parallel", "parallel", "arbitrary"),
        name="proj_res" if residual is not None else "proj",
    )(*args)


def _gates_kernel(x_ref, wf_ref, wi_ref, wft_ref, wit_ref, bfc_ref, bic_ref, bfr_ref, bir_ref,
                  gc_ref, gr_ref, *, heads):
    x = x_ref[...]
    ln = x.shape[0]
    r_i = lax.broadcasted_iota(jnp.int32, (ln, ln), 0)
    c_i = lax.broadcasted_iota(jnp.int32, (ln, ln), 1)
    tril = (c_i <= r_i).astype(F32)
    triu = (c_i >= r_i).astype(F32)

    gf = jnp.dot(x, wf_ref[...], preferred_element_type=F32) + bfc_ref[...]
    gi = jnp.dot(x, wi_ref[...], preferred_element_type=F32) + bic_ref[...]
    ls = _log_sigmoid(gf)
    pre = jnp.dot(tril, ls, precision=HIGHEST, preferred_element_type=F32)
    suf = jnp.dot(triu, ls, precision=HIGHEST, preferred_element_type=F32)
    lane = lax.broadcasted_iota(jnp.int32, gf.shape, 1)
    cum = jnp.where(lane < 2 * heads, pre, suf)
    is_a = (lane < heads) | ((lane >= 2 * heads) & (lane < 3 * heads))
    gc_ref[...] = jnp.where(is_a, cum, gi - cum)

    dn = (((1,), (1,)), ((), ()))
    gft = lax.dot_general(wft_ref[...], x, dn, preferred_element_type=F32) + bfr_ref[...]
    git = lax.dot_general(wit_ref[...], x, dn, preferred_element_type=F32) + bir_ref[...]
    lst = _log_sigmoid(gft)
    pre_t = jnp.dot(lst, triu, precision=HIGHEST, preferred_element_type=F32)
    suf_t = jnp.dot(lst, tril, precision=HIGHEST, preferred_element_type=F32)
    row = lax.broadcasted_iota(jnp.int32, gft.shape, 0)
    cum_t = jnp.where(row < 2 * heads, pre_t, suf_t)
    is_a_t = (row < heads) | ((row >= 2 * heads) & (row < 3 * heads))
    gr_ref[...] = jnp.where(is_a_t, cum_t, git - cum_t)


def gate_prep(xn, w_gates, gate_b):
    b, s, d = xn.shape
    heads = gate_b.shape[1]
    nrow = 4 * heads
    ln = SCAN_CHUNK
    wi_f, wf_f, wi_b, wf_b = [w_gates[:, g * heads:(g + 1) * heads] for g in range(4)]
    bi_f, bf_f, bi_b, bf_b = [gate_b[g] for g in range(4)]
    zw = jnp.zeros_like(wi_f)
    zb = jnp.zeros_like(bi_f)
    w_f = jnp.concatenate([wf_f, wf_f, wf_b, wf_b], axis=1)
    w_i = jnp.concatenate([zw, wi_f, zw, wi_b], axis=1)
    b_f = jnp.concatenate([bf_f, bf_f, bf_b, bf_b])
    b_i = jnp.concatenate([zb, bi_f, zb, bi_b])
    padc = LANE - nrow
    w_f_c = jnp.pad(w_f, ((0, 0), (0, padc))).astype(BF16)
    w_i_c = jnp.pad(w_i, ((0, 0), (0, padc))).astype(BF16)
    b_f_c = jnp.pad(b_f, (0, padc)).reshape(1, LANE)
    b_i_c = jnp.pad(b_i, (0, padc)).reshape(1, LANE)
    w_f_r = w_f.T.astype(BF16)
    w_i_r = w_i.T.astype(BF16)
    b_f_r = b_f.reshape(nrow, 1)
    b_i_r = b_i.reshape(nrow, 1)
    full = lambda shp: pl.BlockSpec(shp, lambda bi, i: (0, 0))
    return pl.pallas_call(
        functools.partial(_gates_kernel, heads=heads),
        grid=(b, s // ln),
        in_specs=[
            pl.BlockSpec((None, ln, d), lambda bi, i: (bi, i, 0)),
            full((d, LANE)), full((d, LANE)), full((nrow, d)), full((nrow, d)),
            full((1, LANE)), full((1, LANE)), full((nrow, 1)), full((nrow, 1)),
        ],
        out_specs=[
            pl.BlockSpec((None, ln, LANE), lambda bi, i: (bi, i, 0)),
            pl.BlockSpec((None, nrow, ln), lambda bi, i: (bi, 0, i)),
        ],
        out_shape=[
            jax.ShapeDtypeStruct((b, s, LANE), F32),
            jax.ShapeDtypeStruct((b, nrow, s), F32),
        ],
        compiler_params=_cp("parallel", "parallel"),
        name="gate_prep",
    )(xn, w_f_c, w_i_c, w_f_r, w_i_r, b_f_c, b_i_c, b_f_r, b_i_r)


def _rope(t, cos, sin):
    lane = lax.broadcasted_iota(jnp.int32, t.shape, 1)
    quarter = t.shape[1] // 4
    partner = jnp.where((lane & (2 * quarter - 1)) < quarter,
                        pltpu.roll(t, t.shape[1] - quarter, axis=1),
                        pltpu.roll(t, quarter, axis=1))
    return t * cos + partner * sin


def _mlstm_dir(q, k, v, a_col, r_col, r_row, b_last, mask, c_ref, m_ref, h_ref):
    ln = q.shape[0]
    dv = v.shape[1]
    m = m_ref[...][:, 0:1]
    v_ext = jnp.concatenate([v, jnp.ones((ln, LANE), F32)], axis=1).astype(BF16)
    dlog = jnp.where(mask, a_col + r_row, NEG)
    m_row = jnp.maximum(a_col + m, jnp.max(dlog, axis=1, keepdims=True))
    w_inter = jnp.exp(a_col + m - m_row)
    dmat = jnp.exp(dlog - m_row)
    s = lax.dot_general(q, k.astype(BF16), (((1,), (1,)), ((), ())), preferred_element_type=F32)
    sc = (s * dmat).astype(BF16)
    c_ext = c_ref[...]
    num = w_inter * jnp.dot(q, c_ext.astype(BF16), preferred_element_type=F32)
    num = num + jnp.dot(sc, v_ext, preferred_element_type=F32)
    den = num[:, dv:]
    denom = jnp.maximum(jnp.abs(den), jnp.exp(-m_row))
    inv = 1.0 / denom
    h_ref[...] = num[:, :dv] * jnp.concatenate([inv] * (dv // LANE), axis=1)
    r_max = jnp.max(r_row, axis=1, keepdims=True)
    m_new = b_last + jnp.maximum(m, r_max)
    decay = jnp.exp(b_last + m - m_new)
    ke = (k * jnp.exp(b_last + r_col - m_new)).astype(BF16)
    upd = lax.dot_general(ke, v_ext, (((0,), (0,)), ((), ())), preferred_element_type=F32)
    c_ref[...] = decay * c_ext + upd
    m_ref[...] = jnp.broadcast_to(m_new, m_ref.shape)


def _mlstm_kernel(qf_ref, kf_ref, vf_ref, gcf_ref, grf_ref, qb_ref, kb_ref, vb_ref, gcb_ref, grb_ref,
                  cosf_ref, sinf_ref, cosb_ref, sinb_ref, c0f_ref, m0f_ref, c0b_ref, m0b_ref,
                  hf_ref, hb_ref, cff_ref, mff_ref, cfb_ref, mfb_ref,
                  cf_s, mf_s, cb_s, mb_s, *, heads, hpb, use_rope):
    hg = pl.program_id(1)
    i = pl.program_id(2)

    @pl.when(i == 0)
    def _():
        cf_s[...] = c0f_ref[...]
        mf_s[...] = m0f_ref[...]
        cb_s[...] = c0b_ref[...]
        mb_s[...] = m0b_ref[...]

    ln = qf_ref.shape[0]
    r_i = lax.broadcasted_iota(jnp.int32, (ln, ln), 0)
    c_i = lax.broadcasted_iota(jnp.int32, (ln, ln), 1)
    lane = lax.broadcasted_iota(jnp.int32, (ln, LANE), 1)
    rowi = lax.broadcasted_iota(jnp.int32, (4 * heads, ln), 0)
    scale = A_DK ** -0.5

    def col(g_ref, idx):
        return jnp.sum(jnp.where(lane == idx, g_ref[...], 0.0), axis=1, keepdims=True)

    def rowv(g_ref, idx):
        return jnp.sum(jnp.where(rowi == idx, g_ref[...], 0.0), axis=0, keepdims=True)

    def prep(q_ref, k_ref, cos_ref, sin_ref, hh):
        q = q_ref[:, hh * A_DK:(hh + 1) * A_DK]
        k = k_ref[:, hh * A_DK:(hh + 1) * A_DK]
        if use_rope:
            q = _rope(q, cos_ref[...], sin_ref[...])
            k = _rope(k, cos_ref[...], sin_ref[...])
        return (q * scale).astype(BF16), k

    for hh in range(hpb):
        h = hg * hpb + hh
        vs = slice(hh * A_DV, (hh + 1) * A_DV)
        q, k = prep(qf_ref, kf_ref, cosf_ref, sinf_ref, hh)
        a_col = col(gcf_ref, h)
        r_col = col(gcf_ref, heads + h)
        r_row = rowv(grf_ref, heads + h)
        b_last = a_col[ln - 1:ln, :]
        _mlstm_dir(q, k, vf_ref[:, vs], a_col, r_col, r_row, b_last, c_i <= r_i,
                   cf_s.at[hh], mf_s.at[hh], hf_ref.at[:, vs])
        q, k = prep(qb_ref, kb_ref, cosb_ref, sinb_ref, hh)
        a_col = col(gcb_ref, 2 * heads + h)
        r_col = col(gcb_ref, 3 * heads + h)
        r_row = rowv(grb_ref, 3 * heads + h)
        b_last = a_col[0:1, :]
        _mlstm_dir(q, k, vb_ref[:, vs], a_col, r_col, r_row, b_last, c_i >= r_i,
                   cb_s.at[hh], mb_s.at[hh], hb_ref.at[:, vs])

    cff_ref[...] = cf_s[...]
    mff_ref[...] = mf_s[...]
    cfb_ref[...] = cb_s[...]
    mfb_ref[...] = mb_s[...]


def mlstm(p, offs, gc, gr, rope_tabs, state_f, state_b):
    b, s, _ = p.shape
    heads = gr.shape[1] // 4
    ln = SCAN_CHUNK
    nc = s // ln
    oq, ok, ov = [o // A_DK for o in offs[:2]] + [offs[2] // A_DV]
    hpb = 2 if all(n % 2 == 0 for n in (heads, oq, ok, ov)) else 1
    use_rope = rope_tabs is not None
    if not use_rope:
        rope_tabs = (jnp.zeros((s, A_DK), F32),) * 2
    cos, sin = rope_tabs

    fwd = lambda i: i
    bwd = lambda i: nc - 1 - i

    def specs(ix):
        return [
            pl.BlockSpec((None, ln, hpb * A_DK), lambda bi, h, i: (bi, ix(i), oq // hpb + h)),
            pl.BlockSpec((None, ln, hpb * A_DK), lambda bi, h, i: (bi, ix(i), ok // hpb + h)),
            pl.BlockSpec((None, ln, hpb * A_DV), lambda bi, h, i: (bi, ix(i), ov // hpb + h)),
            pl.BlockSpec((None, ln, LANE), lambda bi, h, i: (bi, ix(i), 0)),
            pl.BlockSpec((None, 4 * heads, ln), lambda bi, h, i: (bi, 0, ix(i))),
        ]

    def tab(ix):
        return pl.BlockSpec((ln, A_DK), lambda bi, h, i: (ix(i), 0))

    dce = A_DV + LANE
    c_spec = pl.BlockSpec((None, hpb, A_DK, dce), lambda bi, h, i: (bi, h, 0, 0))
    m_spec = pl.BlockSpec((None, hpb, 1, LANE), lambda bi, h, i: (bi, h, 0, 0))
    outs = pl.pallas_call(
        functools.partial(_mlstm_kernel, heads=heads, hpb=hpb, use_rope=use_rope),
        grid=(b, heads // hpb, nc),
        in_specs=specs(fwd) + specs(bwd) + [tab(fwd), tab(fwd), tab(bwd), tab(bwd),
                                            c_spec, m_spec, c_spec, m_spec],
        out_specs=[
            pl.BlockSpec((None, ln, hpb * A_DV), lambda bi, h, i: (bi, i, h)),
            pl.BlockSpec((None, ln, hpb * A_DV), lambda bi, h, i: (bi, nc - 1 - i, h)),
            c_spec, m_spec, c_spec, m_spec,
        ],
        out_shape=[
            jax.ShapeDtypeStruct((b, s, heads * A_DV), F32),
            jax.ShapeDtypeStruct((b, s, heads * A_DV), F32),
            jax.ShapeDtypeStruct((b, heads, A_DK, dce), F32),
            jax.ShapeDtypeStruct((b, heads, 1, LANE), F32),
            jax.ShapeDtypeStruct((b, heads, A_DK, dce), F32),
            jax.ShapeDtypeStruct((b, heads, 1, LANE), F32),
        ],
        scratch_shapes=[
            pltpu.VMEM((hpb, A_DK, dce), F32), pltpu.VMEM((hpb, 1, LANE), F32),
            pltpu.VMEM((hpb, A_DK, dce), F32), pltpu.VMEM((hpb, 1, LANE), F32),
        ],
        compiler_params=_cp("parallel", "parallel", "arbitrary"),
        name="mlstm",
    )(p, p, p, gc, gr, p, p, p, gc, gr, cos, sin, cos, sin,
      state_f[0], state_f[1], state_b[0], state_b[1])
    h_f, h_b, cf, mf, cb, mb = outs
    return h_f, h_b, (cf, mf), (cb, mb)


def _mlstm_out_kernel(hf_ref, hb_ref, o_ref, g_ref, y_ref):
    hs = hf_ref[...] + hb_ref[...]
    hs = hs * lax.rsqrt(jnp.mean(hs * hs, axis=-1, keepdims=True) + EPS) * g_ref[...]
    y_ref[...] = (_sigmoid(o_ref[...]) * hs).astype(y_ref.dtype)


def mlstm_out(h_f, h_b, p, off_o, hnorm_g):
    b, s, w = h_f.shape
    heads = w // A_DV
    tm = _pick(s, 1024)
    oo = off_o // A_DV
    blk = lambda off: pl.BlockSpec((None, tm, A_DV), lambda bi, i, h: (bi, i, off + h))
    return pl.pallas_call(
        _mlstm_out_kernel,
        grid=(b, s // tm, heads),
        in_specs=[blk(0), blk(0), blk(oo), pl.BlockSpec((1, A_DV), lambda bi, i, h: (0, h))],
        out_specs=blk(0),
        out_shape=jax.ShapeDtypeStruct((b, s, w), BF16),
        compiler_params=_cp("parallel", "parallel", "parallel"),
        name="mlstm_out",
    )(h_f, h_b, p, hnorm_g.reshape(1, w))


def _conv_kernel(bb_ref, bc_ref, bx_ref, pc_ref, px_ref, nc_ref, nx_ref, w_ref, y_ref):
    i = pl.program_id(1)
    last = pl.num_programs(1) - 1
    u = bc_ref[...] * bx_ref[...]
    tm = u.shape[0]
    hr = pc_ref.shape[0]
    u_prev = pc_ref[...][hr - 1:hr, :] * px_ref[...][hr - 1:hr, :]
    u_next = nc_ref[...][0:1, :] * nx_ref[...][0:1, :]
    u_prev = jnp.where(i == 0, 0.0, u_prev)
    u_next = jnp.where(i == last, 0.0, u_next)
    row = lax.broadcasted_iota(jnp.int32, u.shape, 0)
    dn = jnp.where(row == 0, u_prev, pltpu.roll(u, 1, axis=0))
    up = jnp.where(row == tm - 1, u_next, pltpu.roll(u, tm - 1, axis=0))
    w = w_ref[...]
    y = dn * w[0:1, :] + u * w[1:2, :] + up * w[2:3, :]
    y_ref[...] = (bb_ref[...] * y).astype(y_ref.dtype)


def short_conv(p, off_b, off_c, off_x, w_conv):
    b, s, _ = p.shape
    kw, width = w_conv.shape
    tn = _pick(width, 512)
    tm = _pick(s, 512)
    hr = 8
    nhb = s // hr
    ob, oc, ox = off_b // tn, off_c // tn, off_x // tn
    main = lambda o: pl.BlockSpec((None, tm, tn), lambda bi, i, j: (bi, i, o + j))
    prev = lambda o: pl.BlockSpec(
        (None, hr, tn), lambda bi, i, j: (bi, jnp.maximum(i * (tm // hr) - 1, 0), o + j))
    nxt = lambda o: pl.BlockSpec(
        (None, hr, tn), lambda bi, i, j: (bi, jnp.minimum((i + 1) * (tm // hr), nhb - 1), o + j))
    return pl.pallas_call(
        _conv_kernel,
        grid=(b, s // tm, width // tn),
        in_specs=[main(ob), main(oc), main(ox), prev(oc), prev(ox), nxt(oc), nxt(ox),
                  pl.BlockSpec((kw, tn), lambda bi, i, j: (0, j))],
        out_specs=pl.BlockSpec((None, tm, tn), lambda bi, i, j: (bi, i, j)),
        out_shape=jax.ShapeDtypeStruct((b, s, width), BF16),
        compiler_params=_cp("parallel", "parallel", "parallel"),
        name="short_conv",
    )(p, p, p, p, p, p, p, w_conv)


def _rms(t, g):
    return t * lax.rsqrt(jnp.mean(t * t, axis=-1, keepdims=True) + EPS) * g


def _natten_kernel(q_ref, k_ref, v_ref, kc_ref, vc_ref, tab_ref, gq_ref, gk_ref, o_ref, *, rows, win_r):
    j = pl.program_id(2)
    nk = NA_KROWS * GRID_W
    w0 = jnp.clip(NA_QROWS * j - (NA_KROWS - NA_QROWS) // 2, 0, rows - NA_KROWS)
    start = pl.multiple_of(w0 * GRID_W, GRID_W * 4)
    gk = gk_ref[...]
    kw = _rms(k_ref[pl.ds(start, nk), :], gk).astype(BF16)
    vw = v_ref[pl.ds(start, nk), :].astype(BF16)
    kc = _rms(kc_ref[...], gk).astype(BF16)
    vc = vc_ref[...].astype(BF16)

    n_tab = tab_ref.shape[0]
    odd = (lax.broadcasted_iota(jnp.int32, (GRID_W, 2 * GRID_W), 1) >= GRID_W).astype(jnp.int32)
    off = w0 - NA_QROWS * j
    dn = (((1,), (1,)), ((), ()))
    for part in range(NA_QPARTS):
        qrows = range(part * (NA_QROWS // NA_QPARTS), (part + 1) * (NA_QROWS // NA_QPARTS))
        rs = slice(qrows[0] * GRID_W, (qrows[-1] + 1) * GRID_W)
        qn = (_rms(q_ref[rs, :], gq_ref[...]) * (C_DH ** -0.5)).astype(BF16)
        bias_rows = []
        for qr in qrows:
            lo = jnp.clip(NA_QROWS * j + qr - win_r // 2, 0, rows - win_r) - w0
            tiles = []
            for kp in range(NA_KROWS // 2):
                ke = 2 * kp
                idx = jnp.clip(off + ke - qr + win_r, 0, n_tab - 1)
                krow = odd + ke
                visible = (krow >= lo) & (krow < lo + win_r)
                tiles.append(jnp.where(visible, tab_ref[idx], NEG))
            bias_rows.append(jnp.concatenate(tiles, axis=1))
        bias = jnp.concatenate(bias_rows, axis=0)
        s_loc = lax.dot_general(qn, kw, dn, preferred_element_type=F32) + bias
        s_ctx = lax.dot_general(qn, kc, dn, preferred_element_type=F32)
        m = jnp.maximum(jnp.max(s_loc, axis=1, keepdims=True), jnp.max(s_ctx, axis=1, keepdims=True))
        p_loc = jnp.exp(s_loc - m)
        p_ctx = jnp.exp(s_ctx - m)
        l = jnp.sum(p_loc, axis=1, keepdims=True) + jnp.sum(p_ctx, axis=1, keepdims=True)
        o = jnp.dot(p_loc.astype(BF16), vw, preferred_element_type=F32)
        o = o + jnp.dot(p_ctx.astype(BF16), vc, preferred_element_type=F32)
        o_ref[rs, :] = (o / l).astype(o_ref.dtype)


def _natten_tiles(rpb):
    heads, nr, ncb = rpb.shape
    win_c = (ncb + 1) // 2
    qc = np.arange(GRID_W)[:, None]
    kc = np.arange(GRID_W)[None, :]
    col0 = np.clip(qc - win_c // 2, 0, GRID_W - win_c)
    col_ok = (kc >= col0) & (kc < col0 + win_c)
    dc = np.clip(kc - qc + win_c - 1, 0, ncb - 1)
    hot_c = (dc[:, :, None] == np.arange(ncb)).astype(np.float32)
    t = jnp.einsum('hrc,qkc->hrqk', rpb.astype(F32), hot_c, precision=HIGHEST)
    t = jnp.where(jnp.asarray(col_ok), t, NEG)
    neg = jnp.full((heads, 1, GRID_W, GRID_W), NEG, F32)
    t = jnp.concatenate([neg, t, neg], axis=1)
    return jnp.concatenate([t[:, :-1], t[:, 1:]], axis=-1)


def natten(p, pc, off_q, off_k, off_v, rpb, g_q, g_k):
    b, s, _ = p.shape
    ctx = pc.shape[1]
    heads, nr, _ = rpb.shape
    rows = s // GRID_W
    nblk = rows // NA_QROWS
    nq = NA_QROWS * GRID_W
    oq, ok, ov = off_q // C_DH, off_k // C_DH, off_v // C_DH
    tiles = _natten_tiles(rpb)
    return pl.pallas_call(
        functools.partial(_natten_kernel, rows=rows, win_r=(nr + 1) // 2),
        grid=(b, heads, nblk),
        in_specs=[
            pl.BlockSpec((None, nq, C_DH), lambda bi, h, j: (bi, j, oq + h)),
            pl.BlockSpec((None, s, C_DH), lambda bi, h, j: (bi, 0, ok + h)),
            pl.BlockSpec((None, s, C_DH), lambda bi, h, j: (bi, 0, ov + h)),
            pl.BlockSpec((None, ctx, C_DH), lambda bi, h, j: (bi, 0, ok + h)),
            pl.BlockSpec((None, ctx, C_DH), lambda bi, h, j: (bi, 0, ov + h)),
            pl.BlockSpec((None, nr + 1, GRID_W, 2 * GRID_W), lambda bi, h, j: (h, 0, 0, 0)),
            pl.BlockSpec((1, C_DH), lambda bi, h, j: (0, 0)),
            pl.BlockSpec((1, C_DH), lambda bi, h, j: (0, 0)),
        ],
        out_specs=pl.BlockSpec((None, nq, C_DH), lambda bi, h, j: (bi, j, h)),
        out_shape=jax.ShapeDtypeStruct((b, s, heads * C_DH), BF16),
        compiler_params=_cp("parallel", "parallel", "arbitrary"),
        name="natten",
    )(p, p, p, pc, pc, tiles, g_q.reshape(1, C_DH), g_k.reshape(1, C_DH))


def _ctx_attn_kernel(q_ref, k_ref, v_ref, gq_ref, gk_ref, o_ref):
    qn = (_rms(q_ref[...], gq_ref[...]) * (C_DH ** -0.5)).astype(BF16)
    kn = _rms(k_ref[...], gk_ref[...]).astype(BF16)
    s = lax.dot_general(qn, kn, (((1,), (1,)), ((), ())), preferred_element_type=F32)
    m = jnp.max(s, axis=1, keepdims=True)
    pr = jnp.exp(s - m)
    l = jnp.sum(pr, axis=1, keepdims=True)
    o = jnp.dot(pr.astype(BF16), v_ref[...].astype(BF16), preferred_element_type=F32)
    o_ref[...] = (o / l).astype(o_ref.dtype)


def ctx_attn(pc, off_q, off_k, off_v, heads, g_q, g_k):
    b, ctx, _ = pc.shape
    blk = lambda o: pl.BlockSpec((None, ctx, C_DH), lambda bi, h: (bi, 0, o // C_DH + h))
    gsp = pl.BlockSpec((1, C_DH), lambda bi, h: (0, 0))
    return pl.pallas_call(
        _ctx_attn_kernel,
        grid=(b, heads),
        in_specs=[blk(off_q), blk(off_k), blk(off_v), gsp, gsp],
        out_specs=blk(0),
        out_shape=jax.ShapeDtypeStruct((b, ctx, heads * C_DH), BF16),
        compiler_params=_cp("parallel", "parallel"),
        name="ctx_attn",
    )(pc, pc, pc, g_q.reshape(1, C_DH), g_k.reshape(1, C_DH))


def _merge_kernel(ya_ref, yb_ref, yc_ref, wa_ref, wb_ref, wc_ref, ga_ref, gb_ref, gc_ref, o_ref):
    y = _sigmoid(ga_ref[...]) * jnp.dot(ya_ref[...], wa_ref[...], preferred_element_type=F32)
    y = y + _sigmoid(gb_ref[...]) * jnp.dot(yb_ref[...], wb_ref[...], preferred_element_type=F32)
    y = y + _sigmoid(gc_ref[...]) * jnp.dot(yc_ref[...], wc_ref[...], preferred_element_type=F32)
    o_ref[...] = y.astype(o_ref.dtype)


def merge(ya, yb, yc, w_a, w_b, w_c, p, off_gate):
    b, s, k = ya.shape
    d = w_a.shape[1]
    tm = _pick(s, 512)
    tn = _pick(d, 512)
    og = off_gate // tn
    nb = d // tn
    act = pl.BlockSpec((None, tm, k), lambda bi, i, j: (bi, i, 0))
    wsp = pl.BlockSpec((k, tn), lambda bi, i, j: (0, j))
    gate = lambda g: pl.BlockSpec((None, tm, tn), lambda bi, i, j: (bi, i, og + g * nb + j))
    return pl.pallas_call(
        _merge_kernel,
        grid=(b, s // tm, nb),
        in_specs=[act, act, act, wsp, wsp, wsp, gate(0), gate(1), gate(2)],
        out_specs=pl.BlockSpec((None, tm, tn), lambda bi, i, j: (bi, i, j)),
        out_shape=jax.ShapeDtypeStruct((b, s, d), BF16),
        compiler_params=_cp("parallel", "parallel", "arbitrary"),
        name="merge",
    )(ya, yb, yc, w_a, w_b, w_c, p, p, p)


def _top_values(cur, k):
    n = cur.shape[0]
    idx = lax.broadcasted_iota(jnp.int32, cur.shape, 0).astype(F32)
    vals = []
    for _ in range(k):
        mx = jnp.max(cur, axis=0, keepdims=True)
        vals.append(mx)
        first = jnp.min(jnp.where(cur == mx, idx, float(n)), axis=0, keepdims=True)
        cur = jnp.where(idx == first, -jnp.inf, cur)
    return vals


SUBLANES = 8


def _sort_network(n):
    pairs = []
    p = 1
    while p < n:
        k = p
        while k >= 1:
            for j in range(k % p, n - k, 2 * k):
                for i in range(min(k, n - j - k)):
                    if (i + j) // (2 * p) == (i + j + k) // (2 * p):
                        pairs.append((i + j, i + j + k))
            k //= 2
        p *= 2
    return pairs


def _exchange(v, i, j):
    hi, lo = jnp.maximum(v[i], v[j]), jnp.minimum(v[i], v[j])
    v[i], v[j] = hi, lo


def _topk_sorted(st):
    k = PEER_TOPK
    assert st.shape[0] == k * SUBLANES
    v = [st[i * SUBLANES:(i + 1) * SUBLANES, :] for i in range(k)]
    for i, j in _sort_network(k):
        _exchange(v, i, j)
    shift = SUBLANES // 2
    while shift >= 1:
        v = [jnp.maximum(v[i], pltpu.roll(v[k - 1 - i], shift, axis=0)) for i in range(k)]
        step = k // 2
        while step >= 1:
            for i in range(k):
                if i & step == 0:
                    _exchange(v, i, i + step)
            step //= 2
        shift //= 2
    return v


def _stack_rows(vals, like):
    row = lax.broadcasted_iota(jnp.int32, like.shape, 0)
    out = jnp.full(like.shape, -jnp.inf, F32)
    for r, val in enumerate(vals):
        out = jnp.where(row == r, val, out)
    return out


def _pair_candidates(v0, v1):
    k = PEER_TOPK
    like = v0[0]
    row = lax.broadcasted_iota(jnp.int32, like.shape, 0)
    v1_rows = [_stack_rows(v1[g:g + SUBLANES], like) for g in range(0, k, SUBLANES)]
    blocks, singles = [], []
    for i in range(k):
        cnt = min(k, (k + 1) // (i + 1))
        if cnt == 1:
            singles.append(v0[i])
            continue
        for g in range(-(-cnt // SUBLANES)):
            valid = min(SUBLANES, cnt - g * SUBLANES)
            blk = v0[i] + v1_rows[g]
            blocks.append(blk if valid == SUBLANES else jnp.where(row < valid, blk, -jnp.inf))
    for g in range(0, len(singles), SUBLANES):
        blocks.append(_stack_rows(singles[g:g + SUBLANES], like) + v1[0])
    return jnp.concatenate(blocks, axis=0)


def _peer_select_kernel(q_ref, keys_ref, s0_ref, e0_ref, c_ref, e1_ref, *, heads):
    dn = (((1,), (1,)), ((), ()))
    half = keys_ref.shape[2]
    k = PEER_TOPK
    for h in range(heads):
        sides = []
        for side in range(2):
            hp = 2 * h + side
            qh = q_ref[:, hp * half:(hp + 1) * half].astype(BF16)
            st = lax.dot_general(keys_ref[hp], qh, dn, preferred_element_type=F32)
            sides.append((st, _topk_sorted(st)))
        (s0, v0), (s1, v1) = sides
        best = _top_values(_pair_candidates(v0, v1), k + 1)
        thr = 0.5 * (best[k - 1] + best[k])
        z = jnp.ones_like(thr)
        for kk in range(1, k):
            z = z + jnp.exp(best[kk] - best[0])
        inv_z = 1.0 / z
        s0_ref[h] = s0
        e0_ref[h] = jnp.where(s0 >= v0[k - 1][0:1, :], jnp.exp(s0 - v0[0][0:1, :]), 0.0) * inv_z
        c_ref[h] = thr - s1
        e1_ref[h] = jnp.where(s1 >= v1[k - 1][0:1, :], jnp.exp(s1 - v1[0][0:1, :]), 0.0)


def peer_select(q, keys):
    b, s, _ = q.shape
    heads, _, nkeys, half = keys.shape
    tt = _pick(s, 256)
    keys_b = keys.reshape(heads * 2, nkeys, half).astype(BF16)
    big = pl.BlockSpec((None, heads, nkeys, tt), lambda bi, i: (bi, 0, 0, i))
    big_shape = jax.ShapeDtypeStruct((b, heads, nkeys, s), F32)
    return pl.pallas_call(
        functools.partial(_peer_select_kernel, heads=heads),
        grid=(b, s // tt),
        in_specs=[
            pl.BlockSpec((None, tt, q.shape[2]), lambda bi, i: (bi, i, 0)),
            pl.BlockSpec((heads * 2, nkeys, half), lambda bi, i: (0, 0, 0)),
        ],
        out_specs=[big, big, big, big],
        out_shape=[big_shape] * 4,
        compiler_params=_cp("parallel", "parallel"),
        name="peer_select",
    )(q, keys_b)


def _gelu(x):
    return 0.5 * x * (1.0 + lax.erf(x * (2.0 ** -0.5)))


def _drop_cols_kernel(w_ref, o_ref, *, lo, hi):
    o_ref[:, :lo] = w_ref[:, :lo].astype(o_ref.dtype)
    o_ref[:, lo:] = w_ref[:, hi:].astype(o_ref.dtype)


def drop_cols_cast(w, layer, lo, hi):
    _, k, n = w.shape
    tk = _pick(k, 128)
    return pl.pallas_call(
        functools.partial(_drop_cols_kernel, lo=lo, hi=hi),
        grid=(k // tk,),
        in_specs=[pl.BlockSpec((None, tk, n), lambda i: (layer, i, 0))],
        out_specs=pl.BlockSpec((tk, n - (hi - lo)), lambda i: (i, 0)),
        out_shape=jax.ShapeDtypeStruct((k, n - (hi - lo)), BF16),
        compiler_params=_cp("parallel"),
        name="drop_cols_cast",
    )(w)


def _cast_t_kernel(x_ref, o_ref):
    o_ref[...] = x_ref[...].T.astype(o_ref.dtype)


def cast_transpose(x):
    n, d = x.shape
    tn = _pick(n, 512)
    return pl.pallas_call(
        _cast_t_kernel,
        grid=(n // tn,),
        in_specs=[pl.BlockSpec((tn, d), lambda i: (i, 0))],
        out_specs=pl.BlockSpec((d, tn), lambda i: (0, i)),
        out_shape=jax.ShapeDtypeStruct((d, n), BF16),
        compiler_params=_cp("parallel"),
        name="cast_transpose",
    )(x)


def _peer_dense_kernel(x_ref, u_ref, vt_ref, s0_ref, e0_ref, c_ref, e1_ref, res_ref, g_ref,
                       o_ref, st_s, wg_s, acc_s, *, heads, a_blk, nchunk):
    t = pl.program_id(1)
    applied = jnp.maximum(t - 1, 0) % nchunk
    nkeys = c_ref.shape[1]
    tt = x_ref.shape[0]

    @pl.when(t == 0)
    def _():
        wg_s[...] = jnp.zeros_like(wg_s)

    @pl.when(applied == 0)
    def _():
        acc_s[...] = jnp.zeros_like(acc_s)

    pair = 2 * nkeys
    nsl = a_blk // 2
    for k in range(nsl):
        rows = slice(k * pair, (k + 1) * pair)
        st_s[k] = lax.dot_general(u_ref[rows, :], x_ref[...], (((1,), (1,)), ((), ())),
                                  preferred_element_type=F32)
        acc_s[...] += jnp.dot(vt_ref[:, rows], wg_s[rows, :], preferred_element_type=F32)
    bg = 32
    for k in range(nsl):
        for tl in range(tt // LANE):
            ls = slice(tl * LANE, (tl + 1) * LANE)
            for g in range(nkeys // bg):
                bs = slice(g * bg, (g + 1) * bg)
                w = [jnp.zeros((bg, LANE), F32) for _ in range(2)]
                for h in range(heads):
                    cv = c_ref[h, bs, ls]
                    ev = e1_ref[h, bs, ls]
                    for a2 in range(2):
                        a = 2 * k + a2
                        s0r = s0_ref[h, a:a + 1, ls]
                        e0r = e0_ref[h, a:a + 1, ls]
                        w[a2] = w[a2] + jnp.where(s0r >= cv, ev * e0r, 0.0)
                for a2 in range(2):
                    r0 = a2 * nkeys + g * bg
                    act = _gelu(st_s[k, r0:r0 + bg, ls])
                    wg_s[k * pair + r0:k * pair + r0 + bg, ls] = (w[a2] * act).astype(wg_s.dtype)

    @pl.when((applied == nchunk - 1) & (t > 0))
    def _():
        o_ref[...] = res_ref[...] + g_ref[...] * acc_s[...].T


def peer_dense(xn, u_b, vt_b, sel, res, gate):
    b, s, d = xn.shape
    s0t, e0t, ct, e1t = sel
    heads, nkeys = s0t.shape[1], s0t.shape[2]
    n_exp = u_b.shape[0]
    a_blk = 8
    ec = a_blk * nkeys
    nchunk = n_exp // ec
    tt = _pick(s, 512)
    n_pairs = (s // tt) * nchunk
    once = pl.Buffered(1)
    scored = lambda t: jnp.minimum(t, n_pairs - 1)
    applied = lambda t: jnp.maximum(t - 1, 0)
    a_sel = pl.BlockSpec((None, heads, a_blk, tt),
                         lambda bi, t: (bi, 0, scored(t) % nchunk, scored(t) // nchunk))
    b_sel = pl.BlockSpec((None, heads, nkeys, tt), lambda bi, t: (bi, 0, 0, scored(t) // nchunk),
                         pipeline_mode=once)
    return pl.pallas_call(
        functools.partial(_peer_dense_kernel, heads=heads, a_blk=a_blk, nchunk=nchunk),
        grid=(b, n_pairs + 1),
        in_specs=[pl.BlockSpec((None, tt, d), lambda bi, t: (bi, scored(t) // nchunk, 0), pipeline_mode=once),
                  pl.BlockSpec((ec, d), lambda bi, t: (scored(t) % nchunk, 0)),
                  pl.BlockSpec((d, ec), lambda bi, t: (0, applied(t) % nchunk)),
                  a_sel, a_sel, b_sel, b_sel,
                  pl.BlockSpec((None, tt, d), lambda bi, t: (bi, applied(t) // nchunk, 0), pipeline_mode=once),
                  pl.BlockSpec((None, 1, d), lambda bi, t: (bi, 0, 0))],
        out_specs=pl.BlockSpec((None, tt, d), lambda bi, t: (bi, applied(t) // nchunk, 0)),
        out_shape=jax.ShapeDtypeStruct((b, s, d), F32),
        scratch_shapes=[pltpu.VMEM((a_blk // 2, 2 * nkeys, tt), F32), pltpu.VMEM((ec, tt), BF16),
                        pltpu.VMEM((d, tt), F32)],
        compiler_params=_cp("parallel", "arbitrary"),
        name="peer_dense",
    )(xn, u_b, vt_b, s0t, e0t, ct, e1t, res, gate.reshape(b, 1, d))


def _rope_tables(s, dk):
    half = dk // 2
    nf = half // 2
    inv = ROPE_BASE ** (-jnp.arange(nf, dtype=F32) / nf)
    pos = jnp.arange(s)
    ang_r = (pos // GRID_W).astype(F32)[:, None] * inv
    ang_c = (pos % GRID_W).astype(F32)[:, None] * inv
    cos = jnp.concatenate([jnp.cos(ang_r)] * 2 + [jnp.cos(ang_c)] * 2, axis=1)
    sin = jnp.concatenate([-jnp.sin(ang_r), jnp.sin(ang_r), -jnp.sin(ang_c), jnp.sin(ang_c)], axis=1)
    return cos, sin


def _init_state(b, heads, m_value):
    return (jnp.zeros((b, heads, A_DK, A_DV + LANE), F32),
            jnp.full((b, heads, 1, LANE), m_value, F32))


def kernel(x, c, ctx, c_ctx, w_ada, b_ada, norm_g, w_in, a_gate_b, a_hnorm_g, b_conv, c_qk_g, c_rpb,
           w_a_out, w_b_out, w_c_out, w_out, peer_wq, peer_keys, peer_u, peer_v):
    bsz, seq, d = x.shape
    depth = w_ada.shape[0]
    a_heads = a_gate_b.shape[2]
    c_heads = c_rpb.shape[1]

    off_ak = 0
    off_av = off_ak + a_heads * A_DK
    gate_lo = off_av + a_heads * A_DV
    gate_hi = gate_lo + 4 * a_heads
    off_ck = gate_lo
    off_cv = off_ck + c_heads * C_DH
    kv_width = off_cv + c_heads * C_DH
    off_aq = kv_width
    off_ao = off_aq + a_heads * A_DK
    off_bb = off_ao + a_heads * A_DV
    off_bc = off_bb + d
    off_bx = off_bc + d
    off_cq = off_bx + d
    off_gate = off_cq + c_heads * C_DH

    rows = -(-(bsz + 1) // 8) * 8
    cond = jnp.concatenate([c, c_ctx[None, :], jnp.zeros((rows - bsz - 1, d), F32)], axis=0)
    mods = adaln(cond, w_ada, b_ada)

    rope_tabs = _rope_tables(seq, A_DK)
    h_ctx = ctx
    for l in range(depth):
        last = l == depth - 1
        mod_x = [mods[l, :bsz, k * d:(k + 1) * d] for k in range(6)]
        mod_c = [jnp.broadcast_to(mods[l, bsz, k * d:(k + 1) * d], (bsz, d)) for k in range(6)]
        w_proj = drop_cols_cast(w_in, l, gate_lo, gate_hi)
        w_gates = w_in[l, :, gate_lo:gate_hi]
        g_q, g_k = c_qk_g[l, 0], c_qk_g[l, 1]

        cn = norm_mod(h_ctx, norm_g[l, 0], mod_c[0], mod_c[1])
        pc = matmul(cn, w_proj, n_cols=kv_width if last else None)
        gc_c, gr_c = gate_prep(cn, w_gates, a_gate_b[l])
        offs_c = (off_ak, off_ak, off_av) if last else (off_aq, off_ak, off_av)
        ninf = _init_state(bsz, a_heads, -jnp.inf)
        _, _, state_f, state_b = mlstm(pc, offs_c, gc_c, gr_c, None, ninf, ninf)

        xn = norm_mod(x, norm_g[l, 0], mod_x[0], mod_x[1])
        p = matmul(xn, w_proj)
        gc_x, gr_x = gate_prep(xn, w_gates, a_gate_b[l])
        h_f, h_b, _, _ = mlstm(p, (off_aq, off_ak, off_av), gc_x, gr_x, rope_tabs, state_f, state_b)
        ya = mlstm_out(h_f, h_b, p, off_ao, a_hnorm_g[l])
        yb = short_conv(p, off_bb, off_bc, off_bx, b_conv[l])
        yc = natten(p, pc, off_cq, off_ck, off_cv, c_rpb[l], g_q, g_k)
        w_a, w_b, w_c, w_o = [w[l].astype(BF16) for w in (w_a_out, w_b_out, w_c_out, w_out)]
        y = merge(ya, yb, yc, w_a, w_b, w_c, p, off_gate)
        x_new = matmul(y, w_o, residual=(x, mod_x[2]))
        xn2 = norm_mod(x_new, norm_g[l, 1], mod_x[3], mod_x[4])
        wq_b = peer_wq[l].astype(BF16)
        u_b = peer_u[l].astype(BF16)
        vt_b = cast_transpose(peer_v[l])
        sel = peer_select(matmul(xn2, wq_b), peer_keys[l])
        x_out = peer_dense(xn2, u_b, vt_b, sel, x_new, mod_x[5])

        if not last:
            zero = _init_state(bsz, a_heads, 0.0)
            hc_f, hc_b, _, _ = mlstm(pc, (off_aq, off_ak, off_av), gc_c, gr_c, None, zero, zero)
            ya_c = mlstm_out(hc_f, hc_b, pc, off_ao, a_hnorm_g[l])
            yb_c = short_conv(pc, off_bb, off_bc, off_bx, b_conv[l])
            yc_c = ctx_attn(pc, off_cq, off_ck, off_cv, c_heads, g_q, g_k)
            y_c = merge(ya_c, yb_c, yc_c, w_a, w_b, w_c, pc, off_gate)
            hc = matmul(y_c, w_o, residual=(h_ctx, mod_c[2]))
            hcn = norm_mod(hc, norm_g[l, 1], mod_c[3], mod_c[4])
            sel_c = peer_select(matmul(hcn, wq_b), peer_keys[l])
            h_ctx = peer_dense(hcn, u_b, vt_b, sel_c, hc, mod_c[5])
        x = x_out
    return x
```

```python
import functools

import jax
import jax.numpy as jnp
import numpy as np
from jax import lax
from jax.experimental import pallas as pl
from jax.experimental.pallas import tpu as pltpu

F32 = jnp.float32
BF16 = jnp.bfloat16
HIGHEST = lax.Precision.HIGHEST

GRID_W = 64
ROPE_BASE = 10000.0
EPS = 1e-6
PEER_TOPK = 16
A_DK = 128
A_DV = 256
C_DH = 128
SCAN_CHUNK = 256
NA_QROWS = 8
NA_KROWS = 16
NA_QPARTS = 2
NEG = -1e30

VMEM_LIMIT = 56 * 1024 * 1024
LANE = 128


def _cp(*sem, flags=None):
    return pltpu.CompilerParams(dimension_semantics=sem, vmem_limit_bytes=VMEM_LIMIT, flags=flags)


def _sigmoid(x):
    return 1.0 / (1.0 + jnp.exp(-x))


def _log_sigmoid(x):
    return jnp.minimum(x, 0.0) - jnp.log1p(jnp.exp(-jnp.abs(x)))


def _pick(n, pref):
    t = min(n, pref)
    while n % t:
        t //= 2
    return t


def _adaln_kernel(c_ref, w_ref, b_ref, o_ref):
    cc = c_ref[...]
    a = cc * _sigmoid(cc)
    o_ref[...] = jnp.dot(a, w_ref[...], precision=HIGHEST, preferred_element_type=F32) + b_ref[...]


def adaln(cond, w_ada, b_ada):
    nl, d, n = w_ada.shape
    rows = cond.shape[0]
    tn = _pick(n, 1024)
    return pl.pallas_call(
        _adaln_kernel,
        grid=(nl, n // tn),
        in_specs=[
            pl.BlockSpec((rows, d), lambda l, j: (0, 0)),
            pl.BlockSpec((None, d, tn), lambda l, j: (l, 0, j)),
            pl.BlockSpec((None, 1, tn), lambda l, j: (l, 0, j)),
        ],
        out_specs=pl.BlockSpec((None, rows, tn), lambda l, j: (l, 0, j)),
        out_shape=jax.ShapeDtypeStruct((nl, rows, n), F32),
        compiler_params=_cp("parallel", "parallel"),
        name="adaln",
    )(cond, w_ada, b_ada.reshape(nl, 1, n))


def _norm_mod_kernel(x_ref, g_ref, sh_ref, sc_ref, o_ref):
    x = x_ref[...]
    y = x * lax.rsqrt(jnp.mean(x * x, axis=-1, keepdims=True) + EPS) * g_ref[...]
    o_ref[...] = (y * (1.0 + sc_ref[...]) + sh_ref[...]).astype(o_ref.dtype)


def norm_mod(x, g, shift, scale):
    b, s, d = x.shape
    tm = _pick(s, 512)
    vec = pl.BlockSpec((None, 1, d), lambda bi, i: (bi, 0, 0))
    return pl.pallas_call(
        _norm_mod_kernel,
        grid=(b, s // tm),
        in_specs=[
            pl.BlockSpec((None, tm, d), lambda bi, i: (bi, i, 0)),
            pl.BlockSpec((1, d), lambda bi, i: (0, 0)),
            vec, vec,
        ],
        out_specs=pl.BlockSpec((None, tm, d), lambda bi, i: (bi, i, 0)),
        out_shape=jax.ShapeDtypeStruct((b, s, d), BF16),
        compiler_params=_cp("parallel", "parallel"),
        name="norm_mod",
    )(x, g.reshape(1, d), shift.reshape(b, 1, d), scale.reshape(b, 1, d))


def _mm_kernel(a_ref, w_ref, o_ref):
    o_ref[...] = jnp.dot(a_ref[...], w_ref[...], preferred_element_type=F32)


def _mm_res_kernel(a_ref, w_ref, x_ref, g_ref, o_ref):
    acc = jnp.dot(a_ref[...], w_ref[...], preferred_element_type=F32)
    o_ref[...] = x_ref[...] + g_ref[...] * acc


def matmul(a, w, n_cols=None, residual=None):
    b, s, k = a.shape
    n = w.shape[1] if n_cols is None else n_cols
    tm = _pick(s, 1024)
    tn = _pick(n, 1024)
    in_specs = [
        pl.BlockSpec((None, tm, k), lambda bi, i, j: (bi, i, 0)),
        pl.BlockSpec((k, tn), lambda bi, i, j: (0, j)),
    ]
    args = [a, w]
    kern = _mm_kernel
    if residual is not None:
        x, gate = residual
        in_specs += [
            pl.BlockSpec((None, tm, tn), lambda bi, i, j: (bi, i, j)),
            pl.BlockSpec((None, 1, tn), lambda bi, i, j: (bi, 0, j)),
        ]
        args += [x, gate.reshape(b, 1, n)]
        kern = _mm_res_kernel
    return pl.pallas_call(
        kern,
        grid=(b, s // tm, n // tn),
        in_specs=in_specs,
        out_specs=pl.BlockSpec((None, tm, tn), lambda bi, i, j: (bi, i, j)),
        out_shape=jax.ShapeDtypeStruct((b, s, n), F32),
        compiler_params=_cp("parallel", "parallel", "arbitrary"),
        name="proj_res" if residual is not None else "proj",
    )(*args)


def _gates_kernel(x_ref, wf_ref, wi_ref, wft_ref, wit_ref, bfc_ref, bic_ref, bfr_ref, bir_ref,
                  gc_ref, gr_ref, *, heads):
    x = x_ref[...]
    ln = x.shape[0]
    r_i = lax.broadcasted_iota(jnp.int32, (ln, ln), 0)
    c_i = lax.broadcasted_iota(jnp.int32, (ln, ln), 1)
    tril = (c_i <= r_i).astype(F32)
    triu = (c_i >= r_i).astype(F32)

    gf = jnp.dot(x, wf_ref[...], preferred_element_type=F32) + bfc_ref[...]
    gi = jnp.dot(x, wi_ref[...], preferred_element_type=F32) + bic_ref[...]
    ls = _log_sigmoid(gf)
    pre = jnp.dot(tril, ls, precision=HIGHEST, preferred_element_type=F32)
    suf = jnp.dot(triu, ls, precision=HIGHEST, preferred_element_type=F32)
    lane = lax.broadcasted_iota(jnp.int32, gf.shape, 1)
    cum = jnp.where(lane < 2 * heads, pre, suf)
    is_a = (lane < heads) | ((lane >= 2 * heads) & (lane < 3 * heads))
    gc_ref[...] = jnp.where(is_a, cum, gi - cum)

    dn = (((1,), (1,)), ((), ()))
    gft = lax.dot_general(wft_ref[...], x, dn, preferred_element_type=F32) + bfr_ref[...]
    git = lax.dot_general(wit_ref[...], x, dn, preferred_element_type=F32) + bir_ref[...]
    lst = _log_sigmoid(gft)
    pre_t = jnp.dot(lst, triu, precision=HIGHEST, preferred_element_type=F32)
    suf_t = jnp.dot(lst, tril, precision=HIGHEST, preferred_element_type=F32)
    row = lax.broadcasted_iota(jnp.int32, gft.shape, 0)
    cum_t = jnp.where(row < 2 * heads, pre_t, suf_t)
    is_a_t = (row < heads) | ((row >= 2 * heads) & (row < 3 * heads))
    gr_ref[...] = jnp.where(is_a_t, cum_t, git - cum_t)


def gate_prep(xn, w_gates, gate_b):
    b, s, d = xn.shape
    heads = gate_b.shape[1]
    nrow = 4 * heads
    ln = SCAN_CHUNK
    wi_f, wf_f, wi_b, wf_b = [w_gates[:, g * heads:(g + 1) * heads] for g in range(4)]
    bi_f, bf_f, bi_b, bf_b = [gate_b[g] for g in range(4)]
    zw = jnp.zeros_like(wi_f)
    zb = jnp.zeros_like(bi_f)
    w_f = jnp.concatenate([wf_f, wf_f, wf_b, wf_b], axis=1)
    w_i = jnp.concatenate([zw, wi_f, zw, wi_b], axis=1)
    b_f = jnp.concatenate([bf_f, bf_f, bf_b, bf_b])
    b_i = jnp.concatenate([zb, bi_f, zb, bi_b])
    padc = LANE - nrow
    w_f_c = jnp.pad(w_f, ((0, 0), (0, padc))).astype(BF16)
    w_i_c = jnp.pad(w_i, ((0, 0), (0, padc))).astype(BF16)
    b_f_c = jnp.pad(b_f, (0, padc)).reshape(1, LANE)
    b_i_c = jnp.pad(b_i, (0, padc)).reshape(1, LANE)
    w_f_r = w_f.T.astype(BF16)
    w_i_r = w_i.T.astype(BF16)
    b_f_r = b_f.reshape(nrow, 1)
    b_i_r = b_i.reshape(nrow, 1)
    full = lambda shp: pl.BlockSpec(shp, lambda bi, i: (0, 0))
    return pl.pallas_call(
        functools.partial(_gates_kernel, heads=heads),
        grid=(b, s // ln),
        in_specs=[
            pl.BlockSpec((None, ln, d), lambda bi, i: (bi, i, 0)),
            full((d, LANE)), full((d, LANE)), full((nrow, d)), full((nrow, d)),
            full((1, LANE)), full((1, LANE)), full((nrow, 1)), full((nrow, 1)),
        ],
        out_specs=[
            pl.BlockSpec((None, ln, LANE), lambda bi, i: (bi, i, 0)),
            pl.BlockSpec((None, nrow, ln), lambda bi, i: (bi, 0, i)),
        ],
        out_shape=[
            jax.ShapeDtypeStruct((b, s, LANE), F32),
            jax.ShapeDtypeStruct((b, nrow, s), F32),
        ],
        compiler_params=_cp("parallel", "parallel"),
        name="gate_prep",
    )(xn, w_f_c, w_i_c, w_f_r, w_i_r, b_f_c, b_i_c, b_f_r, b_i_r)


def _rope(t, cos, sin):
    lane = lax.broadcasted_iota(jnp.int32, t.shape, 1)
    quarter = t.shape[1] // 4
    partner = jnp.where((lane & (2 * quarter - 1)) < quarter,
                        pltpu.roll(t, t.shape[1] - quarter, axis=1),
                        pltpu.roll(t, quarter, axis=1))
    return t * cos + partner * sin


def _mlstm_dir(q, k, v, a_col, r_col, r_row, b_last, mask, c_ref, m_ref, h_ref):
    ln = q.shape[0]
    dv = v.shape[1]
    m = m_ref[...][:, 0:1]
    v_ext = jnp.concatenate([v, jnp.ones((ln, LANE), F32)], axis=1).astype(BF16)
    dlog = jnp.where(mask, a_col + r_row, NEG)
    m_row = jnp.maximum(a_col + m, jnp.max(dlog, axis=1, keepdims=True))
    w_inter = jnp.exp(a_col + m - m_row)
    dmat = jnp.exp(dlog - m_row)
    s = lax.dot_general(q, k.astype(BF16), (((1,), (1,)), ((), ())), preferred_element_type=F32)
    sc = (s * dmat).astype(BF16)
    c_ext = c_ref[...]
    num = w_inter * jnp.dot(q, c_ext.astype(BF16), preferred_element_type=F32)
    num = num + jnp.dot(sc, v_ext, preferred_element_type=F32)
    den = num[:, dv:]
    denom = jnp.maximum(jnp.abs(den), jnp.exp(-m_row))
    inv = 1.0 / denom
    h_ref[...] = num[:, :dv] * jnp.concatenate([inv] * (dv // LANE), axis=1)
    r_max = jnp.max(r_row, axis=1, keepdims=True)
    m_new = b_last + jnp.maximum(m, r_max)
    decay = jnp.exp(b_last + m - m_new)
    ke = (k * jnp.exp(b_last + r_col - m_new)).astype(BF16)
    upd = lax.dot_general(ke, v_ext, (((0,), (0,)), ((), ())), preferred_element_type=F32)
    c_ref[...] = decay * c_ext + upd
    m_ref[...] = jnp.broadcast_to(m_new, m_ref.shape)


def _mlstm_kernel(qf_ref, kf_ref, vf_ref, gcf_ref, grf_ref, qb_ref, kb_ref, vb_ref, gcb_ref, grb_ref,
                  cosf_ref, sinf_ref, cosb_ref, sinb_ref, c0f_ref, m0f_ref, c0b_ref, m0b_ref,
                  hf_ref, hb_ref, cff_ref, mff_ref, cfb_ref, mfb_ref,
                  cf_s, mf_s, cb_s, mb_s, *, heads, hpb, use_rope):
    hg = pl.program_id(1)
    i = pl.program_id(2)

    @pl.when(i == 0)
    def _():
        cf_s[...] = c0f_ref[...]
        mf_s[...] = m0f_ref[...]
        cb_s[...] = c0b_ref[...]
        mb_s[...] = m0b_ref[...]

    ln = qf_ref.shape[0]
    r_i = lax.broadcasted_iota(jnp.int32, (ln, ln), 0)
    c_i = lax.broadcasted_iota(jnp.int32, (ln, ln), 1)
    lane = lax.broadcasted_iota(jnp.int32, (ln, LANE), 1)
    rowi = lax.broadcasted_iota(jnp.int32, (4 * heads, ln), 0)
    scale = A_DK ** -0.5

    def col(g_ref, idx):
        return jnp.sum(jnp.where(lane == idx, g_ref[...], 0.0), axis=1, keepdims=True)

    def rowv(g_ref, idx):
        return jnp.sum(jnp.where(rowi == idx, g_ref[...], 0.0), axis=0, keepdims=True)

    def prep(q_ref, k_ref, cos_ref, sin_ref, hh):
        q = q_ref[:, hh * A_DK:(hh + 1) * A_DK]
        k = k_ref[:, hh * A_DK:(hh + 1) * A_DK]
        if use_rope:
            q = _rope(q, cos_ref[...], sin_ref[...])
            k = _rope(k, cos_ref[...], sin_ref[...])
        return (q * scale).astype(BF16), k

    for hh in range(hpb):
        h = hg * hpb + hh
        vs = slice(hh * A_DV, (hh + 1) * A_DV)
        q, k = prep(qf_ref, kf_ref, cosf_ref, sinf_ref, hh)
        a_col = col(gcf_ref, h)
        r_col = col(gcf_ref, heads + h)
        r_row = rowv(grf_ref, heads + h)
        b_last = a_col[ln - 1:ln, :]
        _mlstm_dir(q, k, vf_ref[:, vs], a_col, r_col, r_row, b_last, c_i <= r_i,
                   cf_s.at[hh], mf_s.at[hh], hf_ref.at[:, vs])
        q, k = prep(qb_ref, kb_ref, cosb_ref, sinb_ref, hh)
        a_col = col(gcb_ref, 2 * heads + h)
        r_col = col(gcb_ref, 3 * heads + h)
        r_row = rowv(grb_ref, 3 * heads + h)
        b_last = a_col[0:1, :]
        _mlstm_dir(q, k, vb_ref[:, vs], a_col, r_col, r_row, b_last, c_i >= r_i,
                   cb_s.at[hh], mb_s.at[hh], hb_ref.at[:, vs])

    cff_ref[...] = cf_s[...]
    mff_ref[...] = mf_s[...]
    cfb_ref[...] = cb_s[...]
    mfb_ref[...] = mb_s[...]


def mlstm(p, offs, gc, gr, rope_tabs, state_f, state_b):
    b, s, _ = p.shape
    heads = gr.shape[1] // 4
    ln = SCAN_CHUNK
    nc = s // ln
    oq, ok, ov = [o // A_DK for o in offs[:2]] + [offs[2] // A_DV]
    hpb = 2 if all(n % 2 == 0 for n in (heads, oq, ok, ov)) else 1
    use_rope = rope_tabs is not None
    if not use_rope:
        rope_tabs = (jnp.zeros((s, A_DK), F32),) * 2
    cos, sin = rope_tabs

    fwd = lambda i: i
    bwd = lambda i: nc - 1 - i

    def specs(ix):
        return [
            pl.BlockSpec((None, ln, hpb * A_DK), lambda bi, h, i: (bi, ix(i), oq // hpb + h)),
            pl.BlockSpec((None, ln, hpb * A_DK), lambda bi, h, i: (bi, ix(i), ok // hpb + h)),
            pl.BlockSpec((None, ln, hpb * A_DV), lambda bi, h, i: (bi, ix(i), ov // hpb + h)),
            pl.BlockSpec((None, ln, LANE), lambda bi, h, i: (bi, ix(i), 0)),
            pl.BlockSpec((None, 4 * heads, ln), lambda bi, h, i: (bi, 0, ix(i))),
        ]

    def tab(ix):
        return pl.BlockSpec((ln, A_DK), lambda bi, h, i: (ix(i), 0))

    dce = A_DV + LANE
    c_spec = pl.BlockSpec((None, hpb, A_DK, dce), lambda bi, h, i: (bi, h, 0, 0))
    m_spec = pl.BlockSpec((None, hpb, 1, LANE), lambda bi, h, i: (bi, h, 0, 0))
    outs = pl.pallas_call(
        functools.partial(_mlstm_kernel, heads=heads, hpb=hpb, use_rope=use_rope),
        grid=(b, heads // hpb, nc),
        in_specs=specs(fwd) + specs(bwd) + [tab(fwd), tab(fwd), tab(bwd), tab(bwd),
                                            c_spec, m_spec, c_spec, m_spec],
        out_specs=[
            pl.BlockSpec((None, ln, hpb * A_DV), lambda bi, h, i: (bi, i, h)),
            pl.BlockSpec((None, ln, hpb * A_DV), lambda bi, h, i: (bi, nc - 1 - i, h)),
            c_spec, m_spec, c_spec, m_spec,
        ],
        out_shape=[
            jax.ShapeDtypeStruct((b, s, heads * A_DV), F32),
            jax.ShapeDtypeStruct((b, s, heads * A_DV), F32),
            jax.ShapeDtypeStruct((b, heads, A_DK, dce), F32),
            jax.ShapeDtypeStruct((b, heads, 1, LANE), F32),
            jax.ShapeDtypeStruct((b, heads, A_DK, dce), F32),
            jax.ShapeDtypeStruct((b, heads, 1, LANE), F32),
        ],
        scratch_shapes=[
            pltpu.VMEM((hpb, A_DK, dce), F32), pltpu.VMEM((hpb, 1, LANE), F32),
            pltpu.VMEM((hpb, A_DK, dce), F32), pltpu.VMEM((hpb, 1, LANE), F32),
        ],
        compiler_params=_cp("parallel", "parallel", "arbitrary"),
        name="mlstm",
    )(p, p, p, gc, gr, p, p, p, gc, gr, cos, sin, cos, sin,
      state_f[0], state_f[1], state_b[0], state_b[1])
    h_f, h_b, cf, mf, cb, mb = outs
    return h_f, h_b, (cf, mf), (cb, mb)


def _mlstm_out_kernel(hf_ref, hb_ref, o_ref, g_ref, y_ref):
    hs = hf_ref[...] + hb_ref[...]
    hs = hs * lax.rsqrt(jnp.mean(hs * hs, axis=-1, keepdims=True) + EPS) * g_ref[...]
    y_ref[...] = (_sigmoid(o_ref[...]) * hs).astype(y_ref.dtype)


def mlstm_out(h_f, h_b, p, off_o, hnorm_g):
    b, s, w = h_f.shape
    heads = w // A_DV
    tm = _pick(s, 1024)
    oo = off_o // A_DV
    blk = lambda off: pl.BlockSpec((None, tm, A_DV), lambda bi, i, h: (bi, i, off + h))
    return pl.pallas_call(
        _mlstm_out_kernel,
        grid=(b, s // tm, heads),
        in_specs=[blk(0), blk(0), blk(oo), pl.BlockSpec((1, A_DV), lambda bi, i, h: (0, h))],
        out_specs=blk(0),
        out_shape=jax.ShapeDtypeStruct((b, s, w), BF16),
        compiler_params=_cp("parallel", "parallel", "parallel"),
        name="mlstm_out",
    )(h_f, h_b, p, hnorm_g.reshape(1, w))


def _conv_kernel(bb_ref, bc_ref, bx_ref, pc_ref, px_ref, nc_ref, nx_ref, w_ref, y_ref):
    i = pl.program_id(1)
    last = pl.num_programs(1) - 1
    u = bc_ref[...] * bx_ref[...]
    tm = u.shape[0]
    hr = pc_ref.shape[0]
    u_prev = pc_ref[...][hr - 1:hr, :] * px_ref[...][hr - 1:hr, :]
    u_next = nc_ref[...][0:1, :] * nx_ref[...][0:1, :]
    u_prev = jnp.where(i == 0, 0.0, u_prev)
    u_next = jnp.where(i == last, 0.0, u_next)
    row = lax.broadcasted_iota(jnp.int32, u.shape, 0)
    dn = jnp.where(row == 0, u_prev, pltpu.roll(u, 1, axis=0))
    up = jnp.where(row == tm - 1, u_next, pltpu.roll(u, tm - 1, axis=0))
    w = w_ref[...]
    y = dn * w[0:1, :] + u * w[1:2, :] + up * w[2:3, :]
    y_ref[...] = (bb_ref[...] * y).astype(y_ref.dtype)


def short_conv(p, off_b, off_c, off_x, w_conv):
    b, s, _ = p.shape
    kw, width = w_conv.shape
    tn = _pick(width, 512)
    tm = _pick(s, 512)
    hr = 8
    nhb = s // hr
    ob, oc, ox = off_b // tn, off_c // tn, off_x // tn
    main = lambda o: pl.BlockSpec((None, tm, tn), lambda bi, i, j: (bi, i, o + j))
    prev = lambda o: pl.BlockSpec(
        (None, hr, tn), lambda bi, i, j: (bi, jnp.maximum(i * (tm // hr) - 1, 0), o + j))
    nxt = lambda o: pl.BlockSpec(
        (None, hr, tn), lambda bi, i, j: (bi, jnp.minimum((i + 1) * (tm // hr), nhb - 1), o + j))
    return pl.pallas_call(
        _conv_kernel,
        grid=(b, s // tm, width // tn),
        in_specs=[main(ob), main(oc), main(ox), prev(oc), prev(ox), nxt(oc), nxt(ox),
                  pl.BlockSpec((kw, tn), lambda bi, i, j: (0, j))],
        out_specs=pl.BlockSpec((None, tm, tn), lambda bi, i, j: (bi, i, j)),
        out_shape=jax.ShapeDtypeStruct((b, s, width), BF16),
        compiler_params=_cp("parallel", "parallel", "parallel"),
        name="short_conv",
    )(p, p, p, p, p, p, p, w_conv)


def _rms(t, g):
    return t * lax.rsqrt(jnp.mean(t * t, axis=-1, keepdims=True) + EPS) * g


def _natten_kernel(q_ref, k_ref, v_ref, kc_ref, vc_ref, tab_ref, gq_ref, gk_ref, o_ref, *, rows, win_r):
    j = pl.program_id(2)
    nk = NA_KROWS * GRID_W
    w0 = jnp.clip(NA_QROWS * j - (NA_KROWS - NA_QROWS) // 2, 0, rows - NA_KROWS)
    start = pl.multiple_of(w0 * GRID_W, GRID_W * 4)
    gk = gk_ref[...]
    kw = _rms(k_ref[pl.ds(start, nk), :], gk).astype(BF16)
    vw = v_ref[pl.ds(start, nk), :].astype(BF16)
    kc = _rms(kc_ref[...], gk).astype(BF16)
    vc = vc_ref[...].astype(BF16)

    n_tab = tab_ref.shape[0]
    odd = (lax.broadcasted_iota(jnp.int32, (GRID_W, 2 * GRID_W), 1) >= GRID_W).astype(jnp.int32)
    off = w0 - NA_QROWS * j
    dn = (((1,), (1,)), ((), ()))
    for part in range(NA_QPARTS):
        qrows = range(part * (NA_QROWS // NA_QPARTS), (part + 1) * (NA_QROWS // NA_QPARTS))
        rs = slice(qrows[0] * GRID_W, (qrows[-1] + 1) * GRID_W)
        qn = (_rms(q_ref[rs, :], gq_ref[...]) * (C_DH ** -0.5)).astype(BF16)
        bias_rows = []
        for qr in qrows:
            lo = jnp.clip(NA_QROWS * j + qr - win_r // 2, 0, rows - win_r) - w0
            tiles = []
            for kp in range(NA_KROWS // 2):
                ke = 2 * kp
                idx = jnp.clip(off + ke - qr + win_r, 0, n_tab - 1)
                krow = odd + ke
                visible = (krow >= lo) & (krow < lo + win_r)
                tiles.append(jnp.where(visible, tab_ref[idx], NEG))
            bias_rows.append(jnp.concatenate(tiles, axis=1))
        bias = jnp.concatenate(bias_rows, axis=0)
        s_loc = lax.dot_general(qn, kw, dn, preferred_element_type=F32) + bias
        s_ctx = lax.dot_general(qn, kc, dn, preferred_element_type=F32)
        m = jnp.maximum(jnp.max(s_loc, axis=1, keepdims=True), jnp.max(s_ctx, axis=1, keepdims=True))
        p_loc = jnp.exp(s_loc - m)
        p_ctx = jnp.exp(s_ctx - m)
        l = jnp.sum(p_loc, axis=1, keepdims=True) + jnp.sum(p_ctx, axis=1, keepdims=True)
        o = jnp.dot(p_loc.astype(BF16), vw, preferred_element_type=F32)
        o = o + jnp.dot(p_ctx.astype(BF16), vc, preferred_element_type=F32)
        o_ref[rs, :] = (o / l).astype(o_ref.dtype)


def _natten_tiles(rpb):
    heads, nr, ncb = rpb.shape
    win_c = (ncb + 1) // 2
    qc = np.arange(GRID_W)[:, None]
    kc = np.arange(GRID_W)[None, :]
    col0 = np.clip(qc - win_c // 2, 0, GRID_W - win_c)
    col_ok = (kc >= col0) & (kc < col0 + win_c)
    dc = np.clip(kc - qc + win_c - 1, 0, ncb - 1)
    hot_c = (dc[:, :, None] == np.arange(ncb)).astype(np.float32)
    t = jnp.einsum('hrc,qkc->hrqk', rpb.astype(F32), hot_c, precision=HIGHEST)
    t = jnp.where(jnp.asarray(col_ok), t, NEG)
    neg = jnp.full((heads, 1, GRID_W, GRID_W), NEG, F32)
    t = jnp.concatenate([neg, t, neg], axis=1)
    return jnp.concatenate([t[:, :-1], t[:, 1:]], axis=-1)


def natten(p, pc, off_q, off_k, off_v, rpb, g_q, g_k):
    b, s, _ = p.shape
    ctx = pc.shape[1]
    heads, nr, _ = rpb.shape
    rows = s // GRID_W
    nblk = rows // NA_QROWS
    nq = NA_QROWS * GRID_W
    oq, ok, ov = off_q // C_DH, off_k // C_DH, off_v // C_DH
    tiles = _natten_tiles(rpb)
    return pl.pallas_call(
        functools.partial(_natten_kernel, rows=rows, win_r=(nr + 1) // 2),
        grid=(b, heads, nblk),
        in_specs=[
            pl.BlockSpec((None, nq, C_DH), lambda bi, h, j: (bi, j, oq + h)),
            pl.BlockSpec((None, s, C_DH), lambda bi, h, j: (bi, 0, ok + h)),
            pl.BlockSpec((None, s, C_DH), lambda bi, h, j: (bi, 0, ov + h)),
            pl.BlockSpec((None, ctx, C_DH), lambda bi, h, j: (bi, 0, ok + h)),
            pl.BlockSpec((None, ctx, C_DH), lambda bi, h, j: (bi, 0, ov + h)),
            pl.BlockSpec((None, nr + 1, GRID_W, 2 * GRID_W), lambda bi, h, j: (h, 0, 0, 0)),
            pl.BlockSpec((1, C_DH), lambda bi, h, j: (0, 0)),
            pl.BlockSpec((1, C_DH), lambda bi, h, j: (0, 0)),
        ],
        out_specs=pl.BlockSpec((None, nq, C_DH), lambda bi, h, j: (bi, j, h)),
        out_shape=jax.ShapeDtypeStruct((b, s, heads * C_DH), BF16),
        compiler_params=_cp("parallel", "parallel", "arbitrary"),
        name="natten",
    )(p, p, p, pc, pc, tiles, g_q.reshape(1, C_DH), g_k.reshape(1, C_DH))


def _ctx_attn_kernel(q_ref, k_ref, v_ref, gq_ref, gk_ref, o_ref):
    qn = (_rms(q_ref[...], gq_ref[...]) * (C_DH ** -0.5)).astype(BF16)
    kn = _rms(k_ref[...], gk_ref[...]).astype(BF16)
    s = lax.dot_general(qn, kn, (((1,), (1,)), ((), ())), preferred_element_type=F32)
    m = jnp.max(s, axis=1, keepdims=True)
    pr = jnp.exp(s - m)
    l = jnp.sum(pr, axis=1, keepdims=True)
    o = jnp.dot(pr.astype(BF16), v_ref[...].astype(BF16), preferred_element_type=F32)
    o_ref[...] = (o / l).astype(o_ref.dtype)


def ctx_attn(pc, off_q, off_k, off_v, heads, g_q, g_k):
    b, ctx, _ = pc.shape
    blk = lambda o: pl.BlockSpec((None, ctx, C_DH), lambda bi, h: (bi, 0, o // C_DH + h))
    gsp = pl.BlockSpec((1, C_DH), lambda bi, h: (0, 0))
    return pl.pallas_call(
        _ctx_attn_kernel,
        grid=(b, heads),
        in_specs=[blk(off_q), blk(off_k), blk(off_v), gsp, gsp],
        out_specs=blk(0),
        out_shape=jax.ShapeDtypeStruct((b, ctx, heads * C_DH), BF16),
        compiler_params=_cp("parallel", "parallel"),
        name="ctx_attn",
    )(pc, pc, pc, g_q.reshape(1, C_DH), g_k.reshape(1, C_DH))


def _merge_kernel(ya_ref, yb_ref, yc_ref, wa_ref, wb_ref, wc_ref, ga_ref, gb_ref, gc_ref, o_ref):
    y = _sigmoid(ga_ref[...]) * jnp.dot(ya_ref[...], wa_ref[...], preferred_element_type=F32)
    y = y + _sigmoid(gb_ref[...]) * jnp.dot(yb_ref[...], wb_ref[...], preferred_element_type=F32)
    y = y + _sigmoid(gc_ref[...]) * jnp.dot(yc_ref[...], wc_ref[...], preferred_element_type=F32)
    o_ref[...] = y.astype(o_ref.dtype)


def merge(ya, yb, yc, w_a, w_b, w_c, p, off_gate):
    b, s, k = ya.shape
    d = w_a.shape[1]
    tm = _pick(s, 512)
    tn = _pick(d, 512)
    og = off_gate // tn
    nb = d // tn
    act = pl.BlockSpec((None, tm, k), lambda bi, i, j: (bi, i, 0))
    wsp = pl.BlockSpec((k, tn), lambda bi, i, j: (0, j))
    gate = lambda g: pl.BlockSpec((None, tm, tn), lambda bi, i, j: (bi, i, og + g * nb + j))
    return pl.pallas_call(
        _merge_kernel,
        grid=(b, s // tm, nb),
        in_specs=[act, act, act, wsp, wsp, wsp, gate(0), gate(1), gate(2)],
        out_specs=pl.BlockSpec((None, tm, tn), lambda bi, i, j: (bi, i, j)),
        out_shape=jax.ShapeDtypeStruct((b, s, d), BF16),
        compiler_params=_cp("parallel", "parallel", "arbitrary"),
        name="merge",
    )(ya, yb, yc, w_a, w_b, w_c, p, p, p)


def _top_values(cur, k):
    n = cur.shape[0]
    idx = lax.broadcasted_iota(jnp.int32, cur.shape, 0).astype(F32)
    vals = []
    for _ in range(k):
        mx = jnp.max(cur, axis=0, keepdims=True)
        vals.append(mx)
        first = jnp.min(jnp.where(cur == mx, idx, float(n)), axis=0, keepdims=True)
        cur = jnp.where(idx == first, -jnp.inf, cur)
    return vals


SUBLANES = 8


def _sort_network(n):
    pairs = []
    p = 1
    while p < n:
        k = p
        while k >= 1:
            for j in range(k % p, n - k, 2 * k):
                for i in range(min(k, n - j - k)):
                    if (i + j) // (2 * p) == (i + j + k) // (2 * p):
                        pairs.append((i + j, i + j + k))
            k //= 2
        p *= 2
    return pairs


def _exchange(v, i, j):
    hi, lo = jnp.maximum(v[i], v[j]), jnp.minimum(v[i], v[j])
    v[i], v[j] = hi, lo


def _topk_sorted(st):
    k = PEER_TOPK
    assert st.shape[0] == k * SUBLANES
    v = [st[i * SUBLANES:(i + 1) * SUBLANES, :] for i in range(k)]
    for i, j in _sort_network(k):
        _exchange(v, i, j)
    shift = SUBLANES // 2
    while shift >= 1:
        v = [jnp.maximum(v[i], pltpu.roll(v[k - 1 - i], shift, axis=0)) for i in range(k)]
        step = k // 2
        while step >= 1:
            for i in range(k):
                if i & step == 0:
                    _exchange(v, i, i + step)
            step //= 2
        shift //= 2
    return v


def _stack_rows(vals, like):
    row = lax.broadcasted_iota(jnp.int32, like.shape, 0)
    out = jnp.full(like.shape, -jnp.inf, F32)
    for r, val in enumerate(vals):
        out = jnp.where(row == r, val, out)
    return out


def _pair_candidates(v0, v1):
    k = PEER_TOPK
    like = v0[0]
    row = lax.broadcasted_iota(jnp.int32, like.shape, 0)
    v1_rows = [_stack_rows(v1[g:g + SUBLANES], like) for g in range(0, k, SUBLANES)]
    blocks, singles = [], []
    for i in range(k):
        cnt = min(k, (k + 1) // (i + 1))
        if cnt == 1:
            singles.append(v0[i])
            continue
        for g in range(-(-cnt // SUBLANES)):
            valid = min(SUBLANES, cnt - g * SUBLANES)
            blk = v0[i] + v1_rows[g]
            blocks.append(blk if valid == SUBLANES else jnp.where(row < valid, blk, -jnp.inf))
    for g in range(0, len(singles), SUBLANES):
        blocks.append(_stack_rows(singles[g:g + SUBLANES], like) + v1[0])
    return jnp.concatenate(blocks, axis=0)


def _peer_select_kernel(q_ref, keys_ref, s0_ref, e0_ref, c_ref, e1_ref, *, heads):
    dn = (((1,), (1,)), ((), ()))
    half = keys_ref.shape[2]
    k = PEER_TOPK
    for h in range(heads):
        sides = []
        for side in range(2):
            hp = 2 * h + side
            qh = q_ref[:, hp * half:(hp + 1) * half].astype(BF16)
            st = lax.dot_general(keys_ref[hp], qh, dn, preferred_element_type=F32)
            sides.append((st, _topk_sorted(st)))
        (s0, v0), (s1, v1) = sides
        best = _top_values(_pair_candidates(v0, v1), k + 1)
        thr = 0.5 * (best[k - 1] + best[k])
        z = jnp.ones_like(thr)
        for kk in range(1, k):
            z = z + jnp.exp(best[kk] - best[0])
        inv_z = 1.0 / z
        s0_ref[h] = s0
        e0_ref[h] = jnp.where(s0 >= v0[k - 1][0:1, :], jnp.exp(s0 - v0[0][0:1, :]), 0.0) * inv_z
        c_ref[h] = thr - s1
        e1_ref[h] = jnp.where(s1 >= v1[k - 1][0:1, :], jnp.exp(s1 - v1[0][0:1, :]), 0.0)


def peer_select(q, keys):
    b, s, _ = q.shape
    heads, _, nkeys, half = keys.shape
    tt = _pick(s, 256)
    keys_b = keys.reshape(heads * 2, nkeys, half).astype(BF16)
    big = pl.BlockSpec((None, heads, nkeys, tt), lambda bi, i: (bi, 0, 0, i))
    big_shape = jax.ShapeDtypeStruct((b, heads, nkeys, s), F32)
    return pl.pallas_call(
        functools.partial(_peer_select_kernel, heads=heads),
        grid=(b, s // tt),
        in_specs=[
            pl.BlockSpec((None, tt, q.shape[2]), lambda bi, i: (bi, i, 0)),
            pl.BlockSpec((heads * 2, nkeys, half), lambda bi, i: (0, 0, 0)),
        ],
        out_specs=[big, big, big, big],
        out_shape=[big_shape] * 4,
        compiler_params=_cp("parallel", "parallel"),
        name="peer_select",
    )(q, keys_b)


def _gelu(x):
    return 0.5 * x * (1.0 + lax.erf(x * (2.0 ** -0.5)))


def _split_cols_kernel(w_ref, o_ref, g_ref, *, lo, hi):
    o_ref[:, :lo] = w_ref[:, :lo].astype(o_ref.dtype)
    o_ref[:, lo:] = w_ref[:, hi:].astype(o_ref.dtype)
    g_ref[...] = w_ref[:, lo:lo + LANE]


def split_cols_cast(w, layer, lo, hi):
    _, k, n = w.shape
    tk = _pick(k, 128)
    rest, win = pl.pallas_call(
        functools.partial(_split_cols_kernel, lo=lo, hi=hi),
        grid=(k // tk,),
        in_specs=[pl.BlockSpec((None, tk, n), lambda i: (layer, i, 0))],
        out_specs=[pl.BlockSpec((tk, n - (hi - lo)), lambda i: (i, 0)),
                   pl.BlockSpec((tk, LANE), lambda i: (i, 0))],
        out_shape=[jax.ShapeDtypeStruct((k, n - (hi - lo)), BF16),
                   jax.ShapeDtypeStruct((k, LANE), F32)],
        compiler_params=_cp("parallel"),
        name="split_cols_cast",
    )(w)
    return rest, win[:, :hi - lo]


def _cast_kernel(x_ref, o_ref):
    o_ref[...] = x_ref[...].astype(o_ref.dtype)


def cast_bf16(x, layer):
    _, n, d = x.shape
    tn = _pick(n, 1024)
    return pl.pallas_call(
        _cast_kernel,
        grid=(n // tn,),
        in_specs=[pl.BlockSpec((None, tn, d), lambda i: (layer, i, 0))],
        out_specs=pl.BlockSpec((tn, d), lambda i: (i, 0)),
        out_shape=jax.ShapeDtypeStruct((n, d), BF16),
        compiler_params=_cp("parallel"),
        name="cast_bf16",
    )(x)


def _cast_t_kernel(x_ref, o_ref):
    o_ref[...] = x_ref[...].T.astype(o_ref.dtype)


def cast_transpose(x, layer):
    _, n, d = x.shape
    tn = _pick(n, 512)
    return pl.pallas_call(
        _cast_t_kernel,
        grid=(n // tn,),
        in_specs=[pl.BlockSpec((None, tn, d), lambda i: (layer, i, 0))],
        out_specs=pl.BlockSpec((d, tn), lambda i: (0, i)),
        out_shape=jax.ShapeDtypeStruct((d, n), BF16),
        compiler_params=_cp("parallel"),
        name="cast_transpose",
    )(x)


def _peer_dense_kernel(x_ref, u_ref, vt_ref, s0_ref, e0_ref, c_ref, e1_ref, res_ref, g_ref,
                       o_ref, st_s, wg_s, acc_s, *, heads, a_blk, nchunk):
    t = pl.program_id(1)
    applied = jnp.maximum(t - 1, 0) % nchunk
    nkeys = c_ref.shape[1]
    tt = x_ref.shape[0]

    @pl.when(t == 0)
    def _():
        wg_s[...] = jnp.zeros_like(wg_s)

    @pl.when(applied == 0)
    def _():
        acc_s[...] = jnp.zeros_like(acc_s)

    pair = 2 * nkeys
    nsl = a_blk // 2
    for k in range(nsl):
        rows = slice(k * pair, (k + 1) * pair)
        st_s[k] = lax.dot_general(u_ref[rows, :], x_ref[...], (((1,), (1,)), ((), ())),
                                  preferred_element_type=F32)
        acc_s[...] += jnp.dot(vt_ref[:, rows], wg_s[rows, :], preferred_element_type=F32)
    bg = 32
    for k in range(nsl):
        for tl in range(tt // LANE):
            ls = slice(tl * LANE, (tl + 1) * LANE)
            for g in range(nkeys // bg):
                bs = slice(g * bg, (g + 1) * bg)
                w = [jnp.zeros((bg, LANE), F32) for _ in range(2)]
                for h in range(heads):
                    cv = c_ref[h, bs, ls]
                    ev = e1_ref[h, bs, ls]
                    for a2 in range(2):
                        a = 2 * k + a2
                        s0r = s0_ref[h, a:a + 1, ls]
                        e0r = e0_ref[h, a:a + 1, ls]
                        w[a2] = w[a2] + jnp.where(s0r >= cv, ev * e0r, 0.0)
                for a2 in range(2):
                    r0 = a2 * nkeys + g * bg
                    act = _gelu(st_s[k, r0:r0 + bg, ls])
                    wg_s[k * pair + r0:k * pair + r0 + bg, ls] = (w[a2] * act).astype(wg_s.dtype)

    @pl.when((applied == nchunk - 1) & (t > 0))
    def _():
        o_ref[...] = res_ref[...] + g_ref[...] * acc_s[...].T


def peer_dense(xn, u_b, vt_b, sel, res, gate):
    b, s, d = xn.shape
    s0t, e0t, ct, e1t = sel
    heads, nkeys = s0t.shape[1], s0t.shape[2]
    n_exp = u_b.shape[0]
    a_blk = 8
    ec = a_blk * nkeys
    nchunk = n_exp // ec
    tt = _pick(s, 512)
    n_pairs = (s // tt) * nchunk
    once = pl.Buffered(1)
    scored = lambda t: jnp.minimum(t, n_pairs - 1)
    applied = lambda t: jnp.maximum(t - 1, 0)
    a_sel = pl.BlockSpec((None, heads, a_blk, tt),
                         lambda bi, t: (bi, 0, scored(t) % nchunk, scored(t) // nchunk))
    b_sel = pl.BlockSpec((None, heads, nkeys, tt), lambda bi, t: (bi, 0, 0, scored(t) // nchunk),
                         pipeline_mode=once)
    return pl.pallas_call(
        functools.partial(_peer_dense_kernel, heads=heads, a_blk=a_blk, nchunk=nchunk),
        grid=(b, n_pairs + 1),
        in_specs=[pl.BlockSpec((None, tt, d), lambda bi, t: (bi, scored(t) // nchunk, 0), pipeline_mode=once),
                  pl.BlockSpec((ec, d), lambda bi, t: (scored(t) % nchunk, 0)),
                  pl.BlockSpec((d, ec), lambda bi, t: (0, applied(t) % nchunk)),
                  a_sel, a_sel, b_sel, b_sel,
                  pl.BlockSpec((None, tt, d), lambda bi, t: (bi, applied(t) // nchunk, 0), pipeline_mode=once),
                  pl.BlockSpec((None, 1, d), lambda bi, t: (bi, 0, 0))],
        out_specs=pl.BlockSpec((None, tt, d), lambda bi, t: (bi, applied(t) // nchunk, 0)),
        out_shape=jax.ShapeDtypeStruct((b, s, d), F32),
        scratch_shapes=[pltpu.VMEM((a_blk // 2, 2 * nkeys, tt), F32), pltpu.VMEM((ec, tt), BF16),
                        pltpu.VMEM((d, tt), F32)],
        compiler_params=_cp("parallel", "arbitrary"),
        name="peer_dense",
    )(xn, u_b, vt_b, s0t, e0t, ct, e1t, res, gate.reshape(b, 1, d))


def _rope_tables(s, dk):
    half = dk // 2
    nf = half // 2
    inv = ROPE_BASE ** (-jnp.arange(nf, dtype=F32) / nf)
    pos = jnp.arange(s)
    ang_r = (pos // GRID_W).astype(F32)[:, None] * inv
    ang_c = (pos % GRID_W).astype(F32)[:, None] * inv
    cos = jnp.concatenate([jnp.cos(ang_r)] * 2 + [jnp.cos(ang_c)] * 2, axis=1)
    sin = jnp.concatenate([-jnp.sin(ang_r), jnp.sin(ang_r), -jnp.sin(ang_c), jnp.sin(ang_c)], axis=1)
    return cos, sin


def _init_state(b, heads, m_value):
    return (jnp.zeros((b, heads, A_DK, A_DV + LANE), F32),
            jnp.full((b, heads, 1, LANE), m_value, F32))


def kernel(x, c, ctx, c_ctx, w_ada, b_ada, norm_g, w_in, a_gate_b, a_hnorm_g, b_conv, c_qk_g, c_rpb,
           w_a_out, w_b_out, w_c_out, w_out, peer_wq, peer_keys, peer_u, peer_v):
    bsz, seq, d = x.shape
    depth = w_ada.shape[0]
    a_heads = a_gate_b.shape[2]
    c_heads = c_rpb.shape[1]

    off_ak = 0
    off_av = off_ak + a_heads * A_DK
    gate_lo = off_av + a_heads * A_DV
    gate_hi = gate_lo + 4 * a_heads
    off_ck = gate_lo
    off_cv = off_ck + c_heads * C_DH
    kv_width = off_cv + c_heads * C_DH
    off_aq = kv_width
    off_ao = off_aq + a_heads * A_DK
    off_bb = off_ao + a_heads * A_DV
    off_bc = off_bb + d
    off_bx = off_bc + d
    off_cq = off_bx + d
    off_gate = off_cq + c_heads * C_DH

    rows = -(-(bsz + 1) // 8) * 8
    cond = jnp.concatenate([c, c_ctx[None, :], jnp.zeros((rows - bsz - 1, d), F32)], axis=0)
    mods = adaln(cond, w_ada, b_ada)

    rope_tabs = _rope_tables(seq, A_DK)
    h_ctx = ctx
    for l in range(depth):
        last = l == depth - 1
        mod_x = [mods[l, :bsz, k * d:(k + 1) * d] for k in range(6)]
        mod_c = [jnp.broadcast_to(mods[l, bsz, k * d:(k + 1) * d], (bsz, d)) for k in range(6)]
        w_proj, w_gates = split_cols_cast(w_in, l, gate_lo, gate_hi)
        g_q, g_k = c_qk_g[l, 0], c_qk_g[l, 1]

        cn = norm_mod(h_ctx, norm_g[l, 0], mod_c[0], mod_c[1])
        pc = matmul(cn, w_proj, n_cols=kv_width if last else None)
        gc_c, gr_c = gate_prep(cn, w_gates, a_gate_b[l])
        offs_c = (off_ak, off_ak, off_av) if last else (off_aq, off_ak, off_av)
        ninf = _init_state(bsz, a_heads, -jnp.inf)
        _, _, state_f, state_b = mlstm(pc, offs_c, gc_c, gr_c, None, ninf, ninf)

        xn = norm_mod(x, norm_g[l, 0], mod_x[0], mod_x[1])
        p = matmul(xn, w_proj)
        gc_x, gr_x = gate_prep(xn, w_gates, a_gate_b[l])
        h_f, h_b, _, _ = mlstm(p, (off_aq, off_ak, off_av), gc_x, gr_x, rope_tabs, state_f, state_b)
        ya = mlstm_out(h_f, h_b, p, off_ao, a_hnorm_g[l])
        yb = short_conv(p, off_bb, off_bc, off_bx, b_conv[l])
        yc = natten(p, pc, off_cq, off_ck, off_cv, c_rpb[l], g_q, g_k)
        w_a, w_b, w_c, w_o = [cast_bf16(w, l) for w in (w_a_out, w_b_out, w_c_out, w_out)]
        y = merge(ya, yb, yc, w_a, w_b, w_c, p, off_gate)
        x_new = matmul(y, w_o, residual=(x, mod_x[2]))
        xn2 = norm_mod(x_new, norm_g[l, 1], mod_x[3], mod_x[4])
        wq_b = cast_bf16(peer_wq, l)
        u_b = cast_bf16(peer_u, l)
        vt_b = cast_transpose(peer_v, l)
        sel = peer_select(matmul(xn2, wq_b), peer_keys[l])
        x_out = peer_dense(xn2, u_b, vt_b, sel, x_new, mod_x[5])

        if not last:
            zero = _init_state(bsz, a_heads, 0.0)
            hc_f, hc_b, _, _ = mlstm(pc, (off_aq, off_ak, off_av), gc_c, gr_c, None, zero, zero)
            ya_c = mlstm_out(hc_f, hc_b, pc, off_ao, a_hnorm_g[l])
            yb_c = short_conv(pc, off_bb, off_bc, off_bx, b_conv[l])
            yc_c = ctx_attn(pc, off_cq, off_ck, off_cv, c_heads, g_q, g_k)
            y_c = merge(ya_c, yb_c, yc_c, w_a, w_b, w_c, pc, off_gate)
            hc = matmul(y_c, w_o, residual=(h_ctx, mod_c[2]))
            hcn = norm_mod(hc, norm_g[l, 1], mod_c[3], mod_c[4])
            sel_c = peer_select(matmul(hcn, wq_b), peer_keys[l])
            h_ctx = peer_dense(hcn, u_b, vt_b, sel_c, hc, mod_c[5])
        x = x_out
    return x
```

```python
import functools

import jax
import jax.numpy as jnp
import numpy as np
from jax import lax
from jax.experimental import pallas as pl
from jax.experimental.pallas import tpu as pltpu

F32 = jnp.float32
BF16 = jnp.bfloat16
HIGHEST = lax.Precision.HIGHEST

GRID_W = 64
ROPE_BASE = 10000.0
EPS = 1e-6
PEER_TOPK = 16
A_DK = 128
A_DV = 256
C_DH = 128
SCAN_CHUNK = 256
NA_QROWS = 8
NA_KROWS = 16
NA_QPARTS = 2
NEG = -1e30

VMEM_LIMIT = 56 * 1024 * 1024
LANE = 128


def _cp(*sem, flags=None):
    return pltpu.CompilerParams(dimension_semantics=sem, vmem_limit_bytes=VMEM_LIMIT, flags=flags)


def _sigmoid(x):
    return 1.0 / (1.0 + jnp.exp(-x))


def _log_sigmoid(x):
    return jnp.minimum(x, 0.0) - jnp.log1p(jnp.exp(-jnp.abs(x)))


def _pick(n, pref):
    t = min(n, pref)
    while n % t:
        t //= 2
    return t


def _adaln_kernel(c_ref, w_ref, b_ref, o_ref):
    cc = c_ref[...]
    a = cc * _sigmoid(cc)
    o_ref[...] = jnp.dot(a, w_ref[...], precision=HIGHEST, preferred_element_type=F32) + b_ref[...]


def adaln(cond, w_ada, b_ada):
    nl, d, n = w_ada.shape
    rows = cond.shape[0]
    tn = _pick(n, 1024)
    return pl.pallas_call(
        _adaln_kernel,
        grid=(nl, n // tn),
        in_specs=[
            pl.BlockSpec((rows, d), lambda l, j: (0, 0)),
            pl.BlockSpec((None, d, tn), lambda l, j: (l, 0, j)),
            pl.BlockSpec((None, 1, tn), lambda l, j: (l, 0, j)),
        ],
        out_specs=pl.BlockSpec((None, rows, tn), lambda l, j: (l, 0, j)),
        out_shape=jax.ShapeDtypeStruct((nl, rows, n), F32),
        compiler_params=_cp("parallel", "parallel"),
        name="adaln",
    )(cond, w_ada, b_ada.reshape(nl, 1, n))


def _norm_mod_kernel(x_ref, g_ref, sh_ref, sc_ref, o_ref):
    x = x_ref[...]
    y = x * lax.rsqrt(jnp.mean(x * x, axis=-1, keepdims=True) + EPS) * g_ref[...]
    o_ref[...] = (y * (1.0 + sc_ref[...]) + sh_ref[...]).astype(o_ref.dtype)


def norm_mod(x, g, shift, scale):
    b, s, d = x.shape
    tm = _pick(s, 512)
    vec = pl.BlockSpec((None, 1, d), lambda bi, i: (bi, 0, 0))
    return pl.pallas_call(
        _norm_mod_kernel,
        grid=(b, s // tm),
        in_specs=[
            pl.BlockSpec((None, tm, d), lambda bi, i: (bi, i, 0)),
            pl.BlockSpec((1, d), lambda bi, i: (0, 0)),
            vec, vec,
        ],
        out_specs=pl.BlockSpec((None, tm, d), lambda bi, i: (bi, i, 0)),
        out_shape=jax.ShapeDtypeStruct((b, s, d), BF16),
        compiler_params=_cp("parallel", "parallel"),
        name="norm_mod",
    )(x, g.reshape(1, d), shift.reshape(b, 1, d), scale.reshape(b, 1, d))


def _mm_kernel(a_ref, w_ref, o_ref):
    o_ref[...] = jnp.dot(a_ref[...], w_ref[...], preferred_element_type=F32)


def _mm_res_kernel(a_ref, w_ref, x_ref, g_ref, o_ref):
    acc = jnp.dot(a_ref[...], w_ref[...], preferred_element_type=F32)
    o_ref[...] = x_ref[...] + g_ref[...] * acc


def matmul(a, w, n_cols=None, residual=None):
    b, s, k = a.shape
    n = w.shape[1] if n_cols is None else n_cols
    tm = _pick(s, 1024)
    tn = _pick(n, 1024)
    in_specs = [
        pl.BlockSpec((None, tm, k), lambda bi, i, j: (bi, i, 0)),
        pl.BlockSpec((k, tn), lambda bi, i, j: (0, j)),
    ]
    args = [a, w]
    kern = _mm_kernel
    if residual is not None:
        x, gate = residual
        in_specs += [
            pl.BlockSpec((None, tm, tn), lambda bi, i, j: (bi, i, j)),
            pl.BlockSpec((None, 1, tn), lambda bi, i, j: (bi, 0, j)),
        ]
        args += [x, gate.reshape(b, 1, n)]
        kern = _mm_res_kernel
    return pl.pallas_call(
        kern,
        grid=(b, s // tm, n // tn),
        in_specs=in_specs,
        out_specs=pl.BlockSpec((None, tm, tn), lambda bi, i, j: (bi, i, j)),
        out_shape=jax.ShapeDtypeStruct((b, s, n), F32),
        compiler_params=_cp("parallel", "parallel", "arbitrary"),
        name="proj_res" if residual is not None else "proj",
    )(*args)


def _gates_kernel(x_ref, wf_ref, wi_ref, wft_ref, wit_ref, bfc_ref, bic_ref, bfr_ref, bir_ref,
                  gc_ref, gr_ref, *, heads):
    x = x_ref[...]
    ln = x.shape[0]
    r_i = lax.broadcasted_iota(jnp.int32, (ln, ln), 0)
    c_i = lax.broadcasted_iota(jnp.int32, (ln, ln), 1)
    tril = (c_i <= r_i).astype(F32)
    triu = (c_i >= r_i).astype(F32)

    gf = jnp.dot(x, wf_ref[...], preferred_element_type=F32) + bfc_ref[...]
    gi = jnp.dot(x, wi_ref[...], preferred_element_type=F32) + bic_ref[...]
    ls = _log_sigmoid(gf)
    pre = jnp.dot(tril, ls, precision=HIGHEST, preferred_element_type=F32)
    suf = jnp.dot(triu, ls, precision=HIGHEST, preferred_element_type=F32)
    lane = lax.broadcasted_iota(jnp.int32, gf.shape, 1)
    cum = jnp.where(lane < 2 * heads, pre, suf)
    is_a = (lane < heads) | ((lane >= 2 * heads) & (lane < 3 * heads))
    gc_ref[...] = jnp.where(is_a, cum, gi - cum)

    dn = (((1,), (1,)), ((), ()))
    gft = lax.dot_general(wft_ref[...], x, dn, preferred_element_type=F32) + bfr_ref[...]
    git = lax.dot_general(wit_ref[...], x, dn, preferred_element_type=F32) + bir_ref[...]
    lst = _log_sigmoid(gft)
    pre_t = jnp.dot(lst, triu, precision=HIGHEST, preferred_element_type=F32)
    suf_t = jnp.dot(lst, tril, precision=HIGHEST, preferred_element_type=F32)
    row = lax.broadcasted_iota(jnp.int32, gft.shape, 0)
    cum_t = jnp.where(row < 2 * heads, pre_t, suf_t)
    is_a_t = (row < heads) | ((row >= 2 * heads) & (row < 3 * heads))
    gr_ref[...] = jnp.where(is_a_t, cum_t, git - cum_t)


def gate_prep(xn, w_gates, gate_b):
    b, s, d = xn.shape
    heads = gate_b.shape[1]
    nrow = 4 * heads
    ln = SCAN_CHUNK
    wi_f, wf_f, wi_b, wf_b = [w_gates[:, g * heads:(g + 1) * heads] for g in range(4)]
    bi_f, bf_f, bi_b, bf_b = [gate_b[g] for g in range(4)]
    zw = jnp.zeros_like(wi_f)
    zb = jnp.zeros_like(bi_f)
    w_f = jnp.concatenate([wf_f, wf_f, wf_b, wf_b], axis=1)
    w_i = jnp.concatenate([zw, wi_f, zw, wi_b], axis=1)
    b_f = jnp.concatenate([bf_f, bf_f, bf_b, bf_b])
    b_i = jnp.concatenate([zb, bi_f, zb, bi_b])
    padc = LANE - nrow
    w_f_c = jnp.pad(w_f, ((0, 0), (0, padc))).astype(BF16)
    w_i_c = jnp.pad(w_i, ((0, 0), (0, padc))).astype(BF16)
    b_f_c = jnp.pad(b_f, (0, padc)).reshape(1, LANE)
    b_i_c = jnp.pad(b_i, (0, padc)).reshape(1, LANE)
    w_f_r = w_f.T.astype(BF16)
    w_i_r = w_i.T.astype(BF16)
    b_f_r = b_f.reshape(nrow, 1)
    b_i_r = b_i.reshape(nrow, 1)
    full = lambda shp: pl.BlockSpec(shp, lambda bi, i: (0, 0))
    return pl.pallas_call(
        functools.partial(_gates_kernel, heads=heads),
        grid=(b, s // ln),
        in_specs=[
            pl.BlockSpec((None, ln, d), lambda bi, i: (bi, i, 0)),
            full((d, LANE)), full((d, LANE)), full((nrow, d)), full((nrow, d)),
            full((1, LANE)), full((1, LANE)), full((nrow, 1)), full((nrow, 1)),
        ],
        out_specs=[
            pl.BlockSpec((None, ln, LANE), lambda bi, i: (bi, i, 0)),
            pl.BlockSpec((None, nrow, ln), lambda bi, i: (bi, 0, i)),
        ],
        out_shape=[
            jax.ShapeDtypeStruct((b, s, LANE), F32),
            jax.ShapeDtypeStruct((b, nrow, s), F32),
        ],
        compiler_params=_cp("parallel", "parallel"),
        name="gate_prep",
    )(xn, w_f_c, w_i_c, w_f_r, w_i_r, b_f_c, b_i_c, b_f_r, b_i_r)


def _rope(t, cos, sin):
    lane = lax.broadcasted_iota(jnp.int32, t.shape, 1)
    quarter = t.shape[1] // 4
    partner = jnp.where((lane & (2 * quarter - 1)) < quarter,
                        pltpu.roll(t, t.shape[1] - quarter, axis=1),
                        pltpu.roll(t, quarter, axis=1))
    return t * cos + partner * sin


def _mlstm_dir(q, k, v, a_col, r_col, r_row, b_last, mask, c_ref, m_ref, h_ref):
    ln = q.shape[0]
    dv = v.shape[1]
    m = m_ref[...][:, 0:1]
    v_ext = jnp.concatenate([v, jnp.ones((ln, LANE), F32)], axis=1).astype(BF16)
    dlog = jnp.where(mask, a_col + r_row, NEG)
    m_row = jnp.maximum(a_col + m, jnp.max(dlog, axis=1, keepdims=True))
    w_inter = jnp.exp(a_col + m - m_row)
    dmat = jnp.exp(dlog - m_row)
    s = lax.dot_general(q, k.astype(BF16), (((1,), (1,)), ((), ())), preferred_element_type=F32)
    sc = (s * dmat).astype(BF16)
    c_ext = c_ref[...]
    num = w_inter * jnp.dot(q, c_ext.astype(BF16), preferred_element_type=F32)
    num = num + jnp.dot(sc, v_ext, preferred_element_type=F32)
    den = num[:, dv:]
    denom = jnp.maximum(jnp.abs(den), jnp.exp(-m_row))
    inv = 1.0 / denom
    h_ref[...] = num[:, :dv] * jnp.concatenate([inv] * (dv // LANE), axis=1)
    r_max = jnp.max(r_row, axis=1, keepdims=True)
    m_new = b_last + jnp.maximum(m, r_max)
    decay = jnp.exp(b_last + m - m_new)
    ke = (k * jnp.exp(b_last + r_col - m_new)).astype(BF16)
    upd = lax.dot_general(ke, v_ext, (((0,), (0,)), ((), ())), preferred_element_type=F32)
    c_ref[...] = decay * c_ext + upd
    m_ref[...] = jnp.broadcast_to(m_new, m_ref.shape)


def _mlstm_kernel(qf_ref, kf_ref, vf_ref, gcf_ref, grf_ref, qb_ref, kb_ref, vb_ref, gcb_ref, grb_ref,
                  cosf_ref, sinf_ref, cosb_ref, sinb_ref, c0f_ref, m0f_ref, c0b_ref, m0b_ref,
                  hf_ref, hb_ref, cff_ref, mff_ref, cfb_ref, mfb_ref,
                  cf_s, mf_s, cb_s, mb_s, *, heads, hpb, use_rope):
    hg = pl.program_id(1)
    i = pl.program_id(2)

    @pl.when(i == 0)
    def _():
        cf_s[...] = c0f_ref[...]
        mf_s[...] = m0f_ref[...]
        cb_s[...] = c0b_ref[...]
        mb_s[...] = m0b_ref[...]

    ln = qf_ref.shape[0]
    r_i = lax.broadcasted_iota(jnp.int32, (ln, ln), 0)
    c_i = lax.broadcasted_iota(jnp.int32, (ln, ln), 1)
    lane = lax.broadcasted_iota(jnp.int32, (ln, LANE), 1)
    rowi = lax.broadcasted_iota(jnp.int32, (4 * heads, ln), 0)
    scale = A_DK ** -0.5

    def col(g_ref, idx):
        return jnp.sum(jnp.where(lane == idx, g_ref[...], 0.0), axis=1, keepdims=True)

    def rowv(g_ref, idx):
        return jnp.sum(jnp.where(rowi == idx, g_ref[...], 0.0), axis=0, keepdims=True)

    def prep(q_ref, k_ref, cos_ref, sin_ref, hh):
        q = q_ref[:, hh * A_DK:(hh + 1) * A_DK]
        k = k_ref[:, hh * A_DK:(hh + 1) * A_DK]
        if use_rope:
            q = _rope(q, cos_ref[...], sin_ref[...])
            k = _rope(k, cos_ref[...], sin_ref[...])
        return (q * scale).astype(BF16), k

    for hh in range(hpb):
        h = hg * hpb + hh
        vs = slice(hh * A_DV, (hh + 1) * A_DV)
        q, k = prep(qf_ref, kf_ref, cosf_ref, sinf_ref, hh)
        a_col = col(gcf_ref, h)
        r_col = col(gcf_ref, heads + h)
        r_row = rowv(grf_ref, heads + h)
        b_last = a_col[ln - 1:ln, :]
        _mlstm_dir(q, k, vf_ref[:, vs], a_col, r_col, r_row, b_last, c_i <= r_i,
                   cf_s.at[hh], mf_s.at[hh], hf_ref.at[:, vs])
        q, k = prep(qb_ref, kb_ref, cosb_ref, sinb_ref, hh)
        a_col = col(gcb_ref, 2 * heads + h)
        r_col = col(gcb_ref, 3 * heads + h)
        r_row = rowv(grb_ref, 3 * heads + h)
        b_last = a_col[0:1, :]
        _mlstm_dir(q, k, vb_ref[:, vs], a_col, r_col, r_row, b_last, c_i >= r_i,
                   cb_s.at[hh], mb_s.at[hh], hb_ref.at[:, vs])

    cff_ref[...] = cf_s[...]
    mff_ref[...] = mf_s[...]
    cfb_ref[...] = cb_s[...]
    mfb_ref[...] = mb_s[...]


def mlstm(p, offs, gc, gr, rope_tabs, state_f, state_b):
    b, s, _ = p.shape
    heads = gr.shape[1] // 4
    ln = SCAN_CHUNK
    nc = s // ln
    oq, ok, ov = [o // A_DK for o in offs[:2]] + [offs[2] // A_DV]
    hpb = max(c for c in (4, 2, 1) if all(n % c == 0 for n in (heads, oq, ok, ov)))
    use_rope = rope_tabs is not None
    if not use_rope:
        rope_tabs = (jnp.zeros((s, A_DK), F32),) * 2
    cos, sin = rope_tabs

    fwd = lambda i: i
    bwd = lambda i: nc - 1 - i

    def specs(ix):
        return [
            pl.BlockSpec((None, ln, hpb * A_DK), lambda bi, h, i: (bi, ix(i), oq // hpb + h)),
            pl.BlockSpec((None, ln, hpb * A_DK), lambda bi, h, i: (bi, ix(i), ok // hpb + h)),
            pl.BlockSpec((None, ln, hpb * A_DV), lambda bi, h, i: (bi, ix(i), ov // hpb + h)),
            pl.BlockSpec((None, ln, LANE), lambda bi, h, i: (bi, ix(i), 0)),
            pl.BlockSpec((None, 4 * heads, ln), lambda bi, h, i: (bi, 0, ix(i))),
        ]

    def tab(ix):
        return pl.BlockSpec((ln, A_DK), lambda bi, h, i: (ix(i), 0))

    dce = A_DV + LANE
    c_spec = pl.BlockSpec((None, hpb, A_DK, dce), lambda bi, h, i: (bi, h, 0, 0))
    m_spec = pl.BlockSpec((None, hpb, 1, LANE), lambda bi, h, i: (bi, h, 0, 0))
    outs = pl.pallas_call(
        functools.partial(_mlstm_kernel, heads=heads, hpb=hpb, use_rope=use_rope),
        grid=(b, heads // hpb, nc),
        in_specs=specs(fwd) + specs(bwd) + [tab(fwd), tab(fwd), tab(bwd), tab(bwd),
                                            c_spec, m_spec, c_spec, m_spec],
        out_specs=[
            pl.BlockSpec((None, ln, hpb * A_DV), lambda bi, h, i: (bi, i, h)),
            pl.BlockSpec((None, ln, hpb * A_DV), lambda bi, h, i: (bi, nc - 1 - i, h)),
            c_spec, m_spec, c_spec, m_spec,
        ],
        out_shape=[
            jax.ShapeDtypeStruct((b, s, heads * A_DV), F32),
            jax.ShapeDtypeStruct((b, s, heads * A_DV), F32),
            jax.ShapeDtypeStruct((b, heads, A_DK, dce), F32),
            jax.ShapeDtypeStruct((b, heads, 1, LANE), F32),
            jax.ShapeDtypeStruct((b, heads, A_DK, dce), F32),
            jax.ShapeDtypeStruct((b, heads, 1, LANE), F32),
        ],
        scratch_shapes=[
            pltpu.VMEM((hpb, A_DK, dce), F32), pltpu.VMEM((hpb, 1, LANE), F32),
            pltpu.VMEM((hpb, A_DK, dce), F32), pltpu.VMEM((hpb, 1, LANE), F32),
        ],
        compiler_params=_cp("parallel", "parallel", "arbitrary"),
        name="mlstm",
    )(p, p, p, gc, gr, p, p, p, gc, gr, cos, sin, cos, sin,
      state_f[0], state_f[1], state_b[0], state_b[1])
    h_f, h_b, cf, mf, cb, mb = outs
    return h_f, h_b, (cf, mf), (cb, mb)


def _mlstm_out_kernel(hf_ref, hb_ref, o_ref, g_ref, y_ref):
    hs = hf_ref[...] + hb_ref[...]
    hs = hs * lax.rsqrt(jnp.mean(hs * hs, axis=-1, keepdims=True) + EPS) * g_ref[...]
    y_ref[...] = (_sigmoid(o_ref[...]) * hs).astype(y_ref.dtype)


def mlstm_out(h_f, h_b, p, off_o, hnorm_g):
    b, s, w = h_f.shape
    heads = w // A_DV
    tm = _pick(s, 1024)
    oo = off_o // A_DV
    blk = lambda off: pl.BlockSpec((None, tm, A_DV), lambda bi, i, h: (bi, i, off + h))
    return pl.pallas_call(
        _mlstm_out_kernel,
        grid=(b, s // tm, heads),
        in_specs=[blk(0), blk(0), blk(oo), pl.BlockSpec((1, A_DV), lambda bi, i, h: (0, h))],
        out_specs=blk(0),
        out_shape=jax.ShapeDtypeStruct((b, s, w), BF16),
        compiler_params=_cp("parallel", "parallel", "parallel"),
        name="mlstm_out",
    )(h_f, h_b, p, hnorm_g.reshape(1, w))


def _conv_kernel(bb_ref, bc_ref, bx_ref, pc_ref, px_ref, nc_ref, nx_ref, w_ref, y_ref):
    i = pl.program_id(1)
    last = pl.num_programs(1) - 1
    u = bc_ref[...] * bx_ref[...]
    tm = u.shape[0]
    hr = pc_ref.shape[0]
    u_prev = pc_ref[...][hr - 1:hr, :] * px_ref[...][hr - 1:hr, :]
    u_next = nc_ref[...][0:1, :] * nx_ref[...][0:1, :]
    u_prev = jnp.where(i == 0, 0.0, u_prev)
    u_next = jnp.where(i == last, 0.0, u_next)
    row = lax.broadcasted_iota(jnp.int32, u.shape, 0)
    dn = jnp.where(row == 0, u_prev, pltpu.roll(u, 1, axis=0))
    up = jnp.where(row == tm - 1, u_next, pltpu.roll(u, tm - 1, axis=0))
    w = w_ref[...]
    y = dn * w[0:1, :] + u * w[1:2, :] + up * w[2:3, :]
    y_ref[...] = (bb_ref[...] * y).astype(y_ref.dtype)


def short_conv(p, off_b, off_c, off_x, w_conv):
    b, s, _ = p.shape
    kw, width = w_conv.shape
    tn = _pick(width, 512)
    tm = _pick(s, 512)
    hr = 8
    nhb = s // hr
    ob, oc, ox = off_b // tn, off_c // tn, off_x // tn
    main = lambda o: pl.BlockSpec((None, tm, tn), lambda bi, i, j: (bi, i, o + j))
    prev = lambda o: pl.BlockSpec(
        (None, hr, tn), lambda bi, i, j: (bi, jnp.maximum(i * (tm // hr) - 1, 0), o + j))
    nxt = lambda o: pl.BlockSpec(
        (None, hr, tn), lambda bi, i, j: (bi, jnp.minimum((i + 1) * (tm // hr), nhb - 1), o + j))
    return pl.pallas_call(
        _conv_kernel,
        grid=(b, s // tm, width // tn),
        in_specs=[main(ob), main(oc), main(ox), prev(oc), prev(ox), nxt(oc), nxt(ox),
                  pl.BlockSpec((kw, tn), lambda bi, i, j: (0, j))],
        out_specs=pl.BlockSpec((None, tm, tn), lambda bi, i, j: (bi, i, j)),
        out_shape=jax.ShapeDtypeStruct((b, s, width), BF16),
        compiler_params=_cp("parallel", "parallel", "parallel"),
        name="short_conv",
    )(p, p, p, p, p, p, p, w_conv)


def _rms(t, g):
    return t * lax.rsqrt(jnp.mean(t * t, axis=-1, keepdims=True) + EPS) * g


def _natten_kernel(q_ref, k_ref, v_ref, kc_ref, vc_ref, tab_ref, gq_ref, gk_ref, o_ref, *, rows, win_r):
    j = pl.program_id(2)
    nk = NA_KROWS * GRID_W
    w0 = jnp.clip(NA_QROWS * j - (NA_KROWS - NA_QROWS) // 2, 0, rows - NA_KROWS)
    start = pl.multiple_of(w0 * GRID_W, GRID_W * 4)
    gk = gk_ref[...]
    kw = _rms(k_ref[pl.ds(start, nk), :], gk).astype(BF16)
    vw = v_ref[pl.ds(start, nk), :].astype(BF16)
    kc = _rms(kc_ref[...], gk).astype(BF16)
    vc = vc_ref[...].astype(BF16)

    n_tab = tab_ref.shape[0]
    odd = (lax.broadcasted_iota(jnp.int32, (GRID_W, 2 * GRID_W), 1) >= GRID_W).astype(jnp.int32)
    off = w0 - NA_QROWS * j
    dn = (((1,), (1,)), ((), ()))
    for part in range(NA_QPARTS):
        qrows = range(part * (NA_QROWS // NA_QPARTS), (part + 1) * (NA_QROWS // NA_QPARTS))
        rs = slice(qrows[0] * GRID_W, (qrows[-1] + 1) * GRID_W)
        qn = (_rms(q_ref[rs, :], gq_ref[...]) * (C_DH ** -0.5)).astype(BF16)
        bias_rows = []
        for qr in qrows:
            lo = jnp.clip(NA_QROWS * j + qr - win_r // 2, 0, rows - win_r) - w0
            tiles = []
            for kp in range(NA_KROWS // 2):
                ke = 2 * kp
                idx = jnp.clip(off + ke - qr + win_r, 0, n_tab - 1)
                krow = odd + ke
                visible = (krow >= lo) & (krow < lo + win_r)
                tiles.append(jnp.where(visible, tab_ref[idx], NEG))
            bias_rows.append(jnp.concatenate(tiles, axis=1))
        bias = jnp.concatenate(bias_rows, axis=0)
        s_loc = lax.dot_general(qn, kw, dn, preferred_element_type=F32) + bias
        s_ctx = lax.dot_general(qn, kc, dn, preferred_element_type=F32)
        m = jnp.maximum(jnp.max(s_loc, axis=1, keepdims=True), jnp.max(s_ctx, axis=1, keepdims=True))
        p_loc = jnp.exp(s_loc - m)
        p_ctx = jnp.exp(s_ctx - m)
        l = jnp.sum(p_loc, axis=1, keepdims=True) + jnp.sum(p_ctx, axis=1, keepdims=True)
        o = jnp.dot(p_loc.astype(BF16), vw, preferred_element_type=F32)
        o = o + jnp.dot(p_ctx.astype(BF16), vc, preferred_element_type=F32)
        o_ref[rs, :] = (o / l).astype(o_ref.dtype)


def _natten_tiles(rpb):
    heads, nr, ncb = rpb.shape
    win_c = (ncb + 1) // 2
    qc = np.arange(GRID_W)[:, None]
    kc = np.arange(GRID_W)[None, :]
    col0 = np.clip(qc - win_c // 2, 0, GRID_W - win_c)
    col_ok = (kc >= col0) & (kc < col0 + win_c)
    dc = np.clip(kc - qc + win_c - 1, 0, ncb - 1)
    hot_c = (dc[:, :, None] == np.arange(ncb)).astype(np.float32)
    t = jnp.einsum('hrc,qkc->hrqk', rpb.astype(F32), hot_c, precision=HIGHEST)
    t = jnp.where(jnp.asarray(col_ok), t, NEG)
    neg = jnp.full((heads, 1, GRID_W, GRID_W), NEG, F32)
    t = jnp.concatenate([neg, t, neg], axis=1)
    return jnp.concatenate([t[:, :-1], t[:, 1:]], axis=-1)


def natten(p, pc, off_q, off_k, off_v, rpb, g_q, g_k):
    b, s, _ = p.shape
    ctx = pc.shape[1]
    heads, nr, _ = rpb.shape
    rows = s // GRID_W
    nblk = rows // NA_QROWS
    nq = NA_QROWS * GRID_W
    oq, ok, ov = off_q // C_DH, off_k // C_DH, off_v // C_DH
    tiles = _natten_tiles(rpb)
    return pl.pallas_call(
        functools.partial(_natten_kernel, rows=rows, win_r=(nr + 1) // 2),
        grid=(b, heads, nblk),
        in_specs=[
            pl.BlockSpec((None, nq, C_DH), lambda bi, h, j: (bi, j, oq + h)),
            pl.BlockSpec((None, s, C_DH), lambda bi, h, j: (bi, 0, ok + h)),
            pl.BlockSpec((None, s, C_DH), lambda bi, h, j: (bi, 0, ov + h)),
            pl.BlockSpec((None, ctx, C_DH), lambda bi, h, j: (bi, 0, ok + h)),
            pl.BlockSpec((None, ctx, C_DH), lambda bi, h, j: (bi, 0, ov + h)),
            pl.BlockSpec((None, nr + 1, GRID_W, 2 * GRID_W), lambda bi, h, j: (h, 0, 0, 0)),
            pl.BlockSpec((1, C_DH), lambda bi, h, j: (0, 0)),
            pl.BlockSpec((1, C_DH), lambda bi, h, j: (0, 0)),
        ],
        out_specs=pl.BlockSpec((None, nq, C_DH), lambda bi, h, j: (bi, j, h)),
        out_shape=jax.ShapeDtypeStruct((b, s, heads * C_DH), BF16),
        compiler_params=_cp("parallel", "parallel", "arbitrary"),
        name="natten",
    )(p, p, p, pc, pc, tiles, g_q.reshape(1, C_DH), g_k.reshape(1, C_DH))


def _ctx_attn_kernel(q_ref, k_ref, v_ref, gq_ref, gk_ref, o_ref):
    qn = (_rms(q_ref[...], gq_ref[...]) * (C_DH ** -0.5)).astype(BF16)
    kn = _rms(k_ref[...], gk_ref[...]).astype(BF16)
    s = lax.dot_general(qn, kn, (((1,), (1,)), ((), ())), preferred_element_type=F32)
    m = jnp.max(s, axis=1, keepdims=True)
    pr = jnp.exp(s - m)
    l = jnp.sum(pr, axis=1, keepdims=True)
    o = jnp.dot(pr.astype(BF16), v_ref[...].astype(BF16), preferred_element_type=F32)
    o_ref[...] = (o / l).astype(o_ref.dtype)


def ctx_attn(pc, off_q, off_k, off_v, heads, g_q, g_k):
    b, ctx, _ = pc.shape
    blk = lambda o: pl.BlockSpec((None, ctx, C_DH), lambda bi, h: (bi, 0, o // C_DH + h))
    gsp = pl.BlockSpec((1, C_DH), lambda bi, h: (0, 0))
    return pl.pallas_call(
        _ctx_attn_kernel,
        grid=(b, heads),
        in_specs=[blk(off_q), blk(off_k), blk(off_v), gsp, gsp],
        out_specs=blk(0),
        out_shape=jax.ShapeDtypeStruct((b, ctx, heads * C_DH), BF16),
        compiler_params=_cp("parallel", "parallel"),
        name="ctx_attn",
    )(pc, pc, pc, g_q.reshape(1, C_DH), g_k.reshape(1, C_DH))


def _merge_kernel(ya_ref, yb_ref, yc_ref, wa_ref, wb_ref, wc_ref, ga_ref, gb_ref, gc_ref, o_ref):
    y = _sigmoid(ga_ref[...]) * jnp.dot(ya_ref[...], wa_ref[...], preferred_element_type=F32)
    y = y + _sigmoid(gb_ref[...]) * jnp.dot(yb_ref[...], wb_ref[...], preferred_element_type=F32)
    y = y + _sigmoid(gc_ref[...]) * jnp.dot(yc_ref[...], wc_ref[...], preferred_element_type=F32)
    o_ref[...] = y.astype(o_ref.dtype)


def merge(ya, yb, yc, w_a, w_b, w_c, p, off_gate):
    b, s, k = ya.shape
    d = w_a.shape[1]
    tm = _pick(s, 512)
    tn = _pick(d, 512)
    og = off_gate // tn
    nb = d // tn
    act = pl.BlockSpec((None, tm, k), lambda bi, i, j: (bi, i, 0))
    wsp = pl.BlockSpec((k, tn), lambda bi, i, j: (0, j))
    gate = lambda g: pl.BlockSpec((None, tm, tn), lambda bi, i, j: (bi, i, og + g * nb + j))
    return pl.pallas_call(
        _merge_kernel,
        grid=(b, s // tm, nb),
        in_specs=[act, act, act, wsp, wsp, wsp, gate(0), gate(1), gate(2)],
        out_specs=pl.BlockSpec((None, tm, tn), lambda bi, i, j: (bi, i, j)),
        out_shape=jax.ShapeDtypeStruct((b, s, d), BF16),
        compiler_params=_cp("parallel", "parallel", "arbitrary"),
        name="merge",
    )(ya, yb, yc, w_a, w_b, w_c, p, p, p)


def _top_values(cur, k):
    n = cur.shape[0]
    idx = lax.broadcasted_iota(jnp.int32, cur.shape, 0).astype(F32)
    vals = []
    for _ in range(k):
        mx = jnp.max(cur, axis=0, keepdims=True)
        vals.append(mx)
        first = jnp.min(jnp.where(cur == mx, idx, float(n)), axis=0, keepdims=True)
        cur = jnp.where(idx == first, -jnp.inf, cur)
    return vals


SUBLANES = 8


def _sort_network(n):
    pairs = []
    p = 1
    while p < n:
        k = p
        while k >= 1:
            for j in range(k % p, n - k, 2 * k):
                for i in range(min(k, n - j - k)):
                    if (i + j) // (2 * p) == (i + j + k) // (2 * p):
                        pairs.append((i + j, i + j + k))
            k //= 2
        p *= 2
    return pairs


def _exchange(v, i, j):
    hi, lo = jnp.maximum(v[i], v[j]), jnp.minimum(v[i], v[j])
    v[i], v[j] = hi, lo


def _topk_sorted(st):
    k = PEER_TOPK
    assert st.shape[0] == k * SUBLANES
    v = [st[i * SUBLANES:(i + 1) * SUBLANES, :] for i in range(k)]
    for i, j in _sort_network(k):
        _exchange(v, i, j)
    shift = SUBLANES // 2
    while shift >= 1:
        v = [jnp.maximum(v[i], pltpu.roll(v[k - 1 - i], shift, axis=0)) for i in range(k)]
        step = k // 2
        while step >= 1:
            for i in range(k):
                if i & step == 0:
                    _exchange(v, i, i + step)
            step //= 2
        shift //= 2
    return v


def _stack_rows(vals, like):
    row = lax.broadcasted_iota(jnp.int32, like.shape, 0)
    out = jnp.full(like.shape, -jnp.inf, F32)
    for r, val in enumerate(vals):
        out = jnp.where(row == r, val, out)
    return out


def _pair_candidates(v0, v1):
    k = PEER_TOPK
    like = v0[0]
    row = lax.broadcasted_iota(jnp.int32, like.shape, 0)
    v1_rows = [_stack_rows(v1[g:g + SUBLANES], like) for g in range(0, k, SUBLANES)]
    blocks, singles = [], []
    for i in range(k):
        cnt = min(k, (k + 1) // (i + 1))
        if cnt == 1:
            singles.append(v0[i])
            continue
        for g in range(-(-cnt // SUBLANES)):
            valid = min(SUBLANES, cnt - g * SUBLANES)
            blk = v0[i] + v1_rows[g]
            blocks.append(blk if valid == SUBLANES else jnp.where(row < valid, blk, -jnp.inf))
    for g in range(0, len(singles), SUBLANES):
        blocks.append(_stack_rows(singles[g:g + SUBLANES], like) + v1[0])
    return jnp.concatenate(blocks, axis=0)


def _peer_select_kernel(q_ref, keys_ref, s0_ref, e0_ref, c_ref, e1_ref, *, heads):
    dn = (((1,), (1,)), ((), ()))
    half = keys_ref.shape[2]
    k = PEER_TOPK
    for h in range(heads):
        sides = []
        for side in range(2):
            hp = 2 * h + side
            qh = q_ref[:, hp * half:(hp + 1) * half].astype(BF16)
            st = lax.dot_general(keys_ref[hp], qh, dn, preferred_element_type=F32)
            sides.append((st, _topk_sorted(st)))
        (s0, v0), (s1, v1) = sides
        best = _top_values(_pair_candidates(v0, v1), k + 1)
        thr = 0.5 * (best[k - 1] + best[k])
        z = jnp.ones_like(thr)
        for kk in range(1, k):
            z = z + jnp.exp(best[kk] - best[0])
        inv_z = 1.0 / z
        s0_ref[h] = s0
        e0_ref[h] = jnp.where(s0 >= v0[k - 1][0:1, :], jnp.exp(s0 - v0[0][0:1, :]), 0.0) * inv_z
        c_ref[h] = thr - s1
        e1_ref[h] = jnp.where(s1 >= v1[k - 1][0:1, :], jnp.exp(s1 - v1[0][0:1, :]), 0.0)


def peer_select(q, keys):
    b, s, _ = q.shape
    heads, _, nkeys, half = keys.shape
    tt = _pick(s, 256)
    keys_b = keys.reshape(heads * 2, nkeys, half).astype(BF16)
    big = pl.BlockSpec((None, heads, nkeys, tt), lambda bi, i: (bi, 0, 0, i))
    big_shape = jax.ShapeDtypeStruct((b, heads, nkeys, s), F32)
    return pl.pallas_call(
        functools.partial(_peer_select_kernel, heads=heads),
        grid=(b, s // tt),
        in_specs=[
            pl.BlockSpec((None, tt, q.shape[2]), lambda bi, i: (bi, i, 0)),
            pl.BlockSpec((heads * 2, nkeys, half), lambda bi, i: (0, 0, 0)),
        ],
        out_specs=[big, big, big, big],
        out_shape=[big_shape] * 4,
        compiler_params=_cp("parallel", "parallel"),
        name="peer_select",
    )(q, keys_b)


def _gelu(x):
    return 0.5 * x * (1.0 + lax.erf(x * (2.0 ** -0.5)))


def _split_cols_kernel(w_ref, o_ref, g_ref, *, lo, hi):
    o_ref[:, :lo] = w_ref[:, :lo].astype(o_ref.dtype)
    o_ref[:, lo:] = w_ref[:, hi:].astype(o_ref.dtype)
    g_ref[...] = w_ref[:, lo:lo + LANE]


def split_cols_cast(w, layer, lo, hi):
    _, k, n = w.shape
    tk = _pick(k, 128)
    rest, win = pl.pallas_call(
        functools.partial(_split_cols_kernel, lo=lo, hi=hi),
        grid=(k // tk,),
        in_specs=[pl.BlockSpec((None, tk, n), lambda i: (layer, i, 0))],
        out_specs=[pl.BlockSpec((tk, n - (hi - lo)), lambda i: (i, 0)),
                   pl.BlockSpec((tk, LANE), lambda i: (i, 0))],
        out_shape=[jax.ShapeDtypeStruct((k, n - (hi - lo)), BF16),
                   jax.ShapeDtypeStruct((k, LANE), F32)],
        compiler_params=_cp("parallel"),
        name="split_cols_cast",
    )(w)
    return rest, win[:, :hi - lo]


def _cast_kernel(x_ref, o_ref):
    o_ref[...] = x_ref[...].astype(o_ref.dtype)


def cast_bf16(x, layer):
    _, n, d = x.shape
    tn = _pick(n, 1024)
    return pl.pallas_call(
        _cast_kernel,
        grid=(n // tn,),
        in_specs=[pl.BlockSpec((None, tn, d), lambda i: (layer, i, 0))],
        out_specs=pl.BlockSpec((tn, d), lambda i: (i, 0)),
        out_shape=jax.ShapeDtypeStruct((n, d), BF16),
        compiler_params=_cp("parallel"),
        name="cast_bf16",
    )(x)


def _cast_t_kernel(x_ref, o_ref):
    o_ref[...] = x_ref[...].T.astype(o_ref.dtype)


def cast_transpose(x, layer):
    _, n, d = x.shape
    tn = _pick(n, 512)
    return pl.pallas_call(
        _cast_t_kernel,
        grid=(n // tn,),
        in_specs=[pl.BlockSpec((None, tn, d), lambda i: (layer, i, 0))],
        out_specs=pl.BlockSpec((d, tn), lambda i: (0, i)),
        out_shape=jax.ShapeDtypeStruct((d, n), BF16),
        compiler_params=_cp("parallel"),
        name="cast_transpose",
    )(x)


def _peer_dense_kernel(x_ref, u_ref, vt_ref, s0_ref, e0_ref, c_ref, e1_ref, res_ref, g_ref,
                       o_ref, st_s, wg_s, acc_s, *, heads, a_blk, nchunk):
    t = pl.program_id(1)
    applied = jnp.maximum(t - 1, 0) % nchunk
    nkeys = c_ref.shape[1]
    tt = x_ref.shape[0]

    @pl.when(t == 0)
    def _():
        wg_s[...] = jnp.zeros_like(wg_s)

    @pl.when(applied == 0)
    def _():
        acc_s[...] = jnp.zeros_like(acc_s)

    pair = 2 * nkeys
    nsl = a_blk // 2
    for k in range(nsl):
        rows = slice(k * pair, (k + 1) * pair)
        st_s[k] = lax.dot_general(u_ref[rows, :], x_ref[...], (((1,), (1,)), ((), ())),
                                  preferred_element_type=F32)
        acc_s[...] += jnp.dot(vt_ref[:, rows], wg_s[rows, :], preferred_element_type=F32)
    bg = 32
    for k in range(nsl):
        for tl in range(tt // LANE):
            ls = slice(tl * LANE, (tl + 1) * LANE)
            for g in range(nkeys // bg):
                bs = slice(g * bg, (g + 1) * bg)
                w = [jnp.zeros((bg, LANE), F32) for _ in range(2)]
                for h in range(heads):
                    cv = c_ref[h, bs, ls]
                    ev = e1_ref[h, bs, ls]
                    for a2 in range(2):
                        a = 2 * k + a2
                        s0r = s0_ref[h, a:a + 1, ls]
                        e0r = e0_ref[h, a:a + 1, ls]
                        w[a2] = w[a2] + jnp.where(s0r >= cv, ev * e0r, 0.0)
                for a2 in range(2):
                    r0 = a2 * nkeys + g * bg
                    act = _gelu(st_s[k, r0:r0 + bg, ls])
                    wg_s[k * pair + r0:k * pair + r0 + bg, ls] = (w[a2] * act).astype(wg_s.dtype)

    @pl.when((applied == nchunk - 1) & (t > 0))
    def _():
        o_ref[...] = res_ref[...] + g_ref[...] * acc_s[...].T


def peer_dense(xn, u_b, vt_b, sel, res, gate):
    b, s, d = xn.shape
    s0t, e0t, ct, e1t = sel
    heads, nkeys = s0t.shape[1], s0t.shape[2]
    n_exp = u_b.shape[0]
    a_blk = 8
    ec = a_blk * nkeys
    nchunk = n_exp // ec
    tt = _pick(s, 512)
    n_pairs = (s // tt) * nchunk
    once = pl.Buffered(1)
    scored = lambda t: jnp.minimum(t, n_pairs - 1)
    applied = lambda t: jnp.maximum(t - 1, 0)
    a_sel = pl.BlockSpec((None, heads, a_blk, tt),
                         lambda bi, t: (bi, 0, scored(t) % nchunk, scored(t) // nchunk))
    b_sel = pl.BlockSpec((None, heads, nkeys, tt), lambda bi, t: (bi, 0, 0, scored(t) // nchunk),
                         pipeline_mode=once)
    return pl.pallas_call(
        functools.partial(_peer_dense_kernel, heads=heads, a_blk=a_blk, nchunk=nchunk),
        grid=(b, n_pairs + 1),
        in_specs=[pl.BlockSpec((None, tt, d), lambda bi, t: (bi, scored(t) // nchunk, 0), pipeline_mode=once),
                  pl.BlockSpec((ec, d), lambda bi, t: (scored(t) % nchunk, 0)),
                  pl.BlockSpec((d, ec), lambda bi, t: (0, applied(t) % nchunk)),
                  a_sel, a_sel, b_sel, b_sel,
                  pl.BlockSpec((None, tt, d), lambda bi, t: (bi, applied(t) // nchunk, 0), pipeline_mode=once),
                  pl.BlockSpec((None, 1, d), lambda bi, t: (bi, 0, 0))],
        out_specs=pl.BlockSpec((None, tt, d), lambda bi, t: (bi, applied(t) // nchunk, 0)),
        out_shape=jax.ShapeDtypeStruct((b, s, d), F32),
        scratch_shapes=[pltpu.VMEM((a_blk // 2, 2 * nkeys, tt), F32), pltpu.VMEM((ec, tt), BF16),
                        pltpu.VMEM((d, tt), F32)],
        compiler_params=_cp("parallel", "arbitrary"),
        name="peer_dense",
    )(xn, u_b, vt_b, s0t, e0t, ct, e1t, res, gate.reshape(b, 1, d))


def _rope_tables(s, dk):
    half = dk // 2
    nf = half // 2
    inv = ROPE_BASE ** (-jnp.arange(nf, dtype=F32) / nf)
    pos = jnp.arange(s)
    ang_r = (pos // GRID_W).astype(F32)[:, None] * inv
    ang_c = (pos % GRID_W).astype(F32)[:, None] * inv
    cos = jnp.concatenate([jnp.cos(ang_r)] * 2 + [jnp.cos(ang_c)] * 2, axis=1)
    sin = jnp.concatenate([-jnp.sin(ang_r), jnp.sin(ang_r), -jnp.sin(ang_c), jnp.sin(ang_c)], axis=1)
    return cos, sin


def _init_state(b, heads, m_value):
    return (jnp.zeros((b, heads, A_DK, A_DV + LANE), F32),
            jnp.full((b, heads, 1, LANE), m_value, F32))


def kernel(x, c, ctx, c_ctx, w_ada, b_ada, norm_g, w_in, a_gate_b, a_hnorm_g, b_conv, c_qk_g, c_rpb,
           w_a_out, w_b_out, w_c_out, w_out, peer_wq, peer_keys, peer_u, peer_v):
    bsz, seq, d = x.shape
    depth = w_ada.shape[0]
    a_heads = a_gate_b.shape[2]
    c_heads = c_rpb.shape[1]

    off_ak = 0
    off_av = off_ak + a_heads * A_DK
    gate_lo = off_av + a_heads * A_DV
    gate_hi = gate_lo + 4 * a_heads
    off_ck = gate_lo
    off_cv = off_ck + c_heads * C_DH
    kv_width = off_cv + c_heads * C_DH
    off_aq = kv_width
    off_ao = off_aq + a_heads * A_DK
    off_bb = off_ao + a_heads * A_DV
    off_bc = off_bb + d
    off_bx = off_bc + d
    off_cq = off_bx + d
    off_gate = off_cq + c_heads * C_DH

    rows = -(-(bsz + 1) // 8) * 8
    cond = jnp.concatenate([c, c_ctx[None, :], jnp.zeros((rows - bsz - 1, d), F32)], axis=0)
    mods = adaln(cond, w_ada, b_ada)

    rope_tabs = _rope_tables(seq, A_DK)
    h_ctx = ctx
    for l in range(depth):
        last = l == depth - 1
        mod_x = [mods[l, :bsz, k * d:(k + 1) * d] for k in range(6)]
        mod_c = [jnp.broadcast_to(mods[l, bsz, k * d:(k + 1) * d], (bsz, d)) for k in range(6)]
        w_proj, w_gates = split_cols_cast(w_in, l, gate_lo, gate_hi)
        g_q, g_k = c_qk_g[l, 0], c_qk_g[l, 1]

        cn = norm_mod(h_ctx, norm_g[l, 0], mod_c[0], mod_c[1])
        pc = matmul(cn, w_proj, n_cols=kv_width if last else None)
        gc_c, gr_c = gate_prep(cn, w_gates, a_gate_b[l])
        offs_c = (off_ak, off_ak, off_av) if last else (off_aq, off_ak, off_av)
        ninf = _init_state(bsz, a_heads, -jnp.inf)
        _, _, state_f, state_b = mlstm(pc, offs_c, gc_c, gr_c, None, ninf, ninf)

        xn = norm_mod(x, norm_g[l, 0], mod_x[0], mod_x[1])
        p = matmul(xn, w_proj)
        gc_x, gr_x = gate_prep(xn, w_gates, a_gate_b[l])
        h_f, h_b, _, _ = mlstm(p, (off_aq, off_ak, off_av), gc_x, gr_x, rope_tabs, state_f, state_b)
        ya = mlstm_out(h_f, h_b, p, off_ao, a_hnorm_g[l])
        yb = short_conv(p, off_bb, off_bc, off_bx, b_conv[l])
        yc = natten(p, pc, off_cq, off_ck, off_cv, c_rpb[l], g_q, g_k)
        w_a, w_b, w_c, w_o = [cast_bf16(w, l) for w in (w_a_out, w_b_out, w_c_out, w_out)]
        y = merge(ya, yb, yc, w_a, w_b, w_c, p, off_gate)
        x_new = matmul(y, w_o, residual=(x, mod_x[2]))
        xn2 = norm_mod(x_new, norm_g[l, 1], mod_x[3], mod_x[4])
        wq_b = cast_bf16(peer_wq, l)
        u_b = cast_bf16(peer_u, l)
        vt_b = cast_transpose(peer_v, l)
        sel = peer_select(matmul(xn2, wq_b), peer_keys[l])
        x_out = peer_dense(xn2, u_b, vt_b, sel, x_new, mod_x[5])

        if not last:
            zero = _init_state(bsz, a_heads, 0.0)
            hc_f, hc_b, _, _ = mlstm(pc, (off_aq, off_ak, off_av), gc_c, gr_c, None, zero, zero)
            ya_c = mlstm_out(hc_f, hc_b, pc, off_ao, a_hnorm_g[l])
            yb_c = short_conv(pc, off_bb, off_bc, off_bx, b_conv[l])
            yc_c = ctx_attn(pc, off_cq, off_ck, off_cv, c_heads, g_q, g_k)
            y_c = merge(ya_c, yb_c, yc_c, w_a, w_b, w_c, pc, off_gate)
            hc = matmul(y_c, w_o, residual=(h_ctx, mod_c[2]))
            hcn = norm_mod(hc, norm_g[l, 1], mod_c[3], mod_c[4])
            sel_c = peer_select(matmul(hcn, wq_b), peer_keys[l])
            h_ctx = peer_dense(hcn, u_b, vt_b, sel_c, hc, mod_c[5])
        x = x_out
    return x
```

```python
import functools

import jax
import jax.numpy as jnp
import numpy as np
from jax import lax
from jax.experimental import pallas as pl
from jax.experimental.pallas import tpu as pltpu

F32 = jnp.float32
BF16 = jnp.bfloat16
HIGHEST = lax.Precision.HIGHEST

GRID_W = 64
ROPE_BASE = 10000.0
EPS = 1e-6
PEER_TOPK = 16
A_DK = 128
A_DV = 256
C_DH = 128
SCAN_CHUNK = 256
NA_QROWS = 8
NA_KROWS = 16
NA_QPARTS = 2
NEG = -1e30

VMEM_LIMIT = 56 * 1024 * 1024
LANE = 128


def _cp(*sem, flags=None):
    return pltpu.CompilerParams(dimension_semantics=sem, vmem_limit_bytes=VMEM_LIMIT, flags=flags)


def _sigmoid(x):
    return 1.0 / (1.0 + jnp.exp(-x))


def _log_sigmoid(x):
    return jnp.minimum(x, 0.0) - jnp.log1p(jnp.exp(-jnp.abs(x)))


def _pick(n, pref):
    t = min(n, pref)
    while n % t:
        t //= 2
    return t


def _adaln_kernel(c_ref, w_ref, b_ref, o_ref):
    cc = c_ref[...]
    a = cc * _sigmoid(cc)
    o_ref[...] = jnp.dot(a, w_ref[...], precision=HIGHEST, preferred_element_type=F32) + b_ref[...]


def adaln(cond, w_ada, b_ada):
    nl, d, n = w_ada.shape
    rows = cond.shape[0]
    tn = _pick(n, 1024)
    return pl.pallas_call(
        _adaln_kernel,
        grid=(nl, n // tn),
        in_specs=[
            pl.BlockSpec((rows, d), lambda l, j: (0, 0)),
            pl.BlockSpec((None, d, tn), lambda l, j: (l, 0, j)),
            pl.BlockSpec((None, 1, tn), lambda l, j: (l, 0, j)),
        ],
        out_specs=pl.BlockSpec((None, rows, tn), lambda l, j: (l, 0, j)),
        out_shape=jax.ShapeDtypeStruct((nl, rows, n), F32),
        compiler_params=_cp("parallel", "parallel"),
        name="adaln",
    )(cond, w_ada, b_ada.reshape(nl, 1, n))


def _norm_mod_kernel(x_ref, g_ref, sh_ref, sc_ref, o_ref):
    x = x_ref[...]
    y = x * lax.rsqrt(jnp.mean(x * x, axis=-1, keepdims=True) + EPS) * g_ref[...]
    o_ref[...] = (y * (1.0 + sc_ref[...]) + sh_ref[...]).astype(o_ref.dtype)


def norm_mod(x, g, shift, scale):
    b, s, d = x.shape
    tm = _pick(s, 512)
    vec = pl.BlockSpec((None, 1, d), lambda bi, i: (bi, 0, 0))
    return pl.pallas_call(
        _norm_mod_kernel,
        grid=(b, s // tm),
        in_specs=[
            pl.BlockSpec((None, tm, d), lambda bi, i: (bi, i, 0)),
            pl.BlockSpec((1, d), lambda bi, i: (0, 0)),
            vec, vec,
        ],
        out_specs=pl.BlockSpec((None, tm, d), lambda bi, i: (bi, i, 0)),
        out_shape=jax.ShapeDtypeStruct((b, s, d), BF16),
        compiler_params=_cp("parallel", "parallel"),
        name="norm_mod",
    )(x, g.reshape(1, d), shift.reshape(b, 1, d), scale.reshape(b, 1, d))


def _mm_kernel(a_ref, w_ref, o_ref):
    o_ref[...] = jnp.dot(a_ref[...], w_ref[...], preferred_element_type=F32)


def _mm_res_kernel(a_ref, w_ref, x_ref, g_ref, o_ref):
    acc = jnp.dot(a_ref[...], w_ref[...], preferred_element_type=F32)
    o_ref[...] = x_ref[...] + g_ref[...] * acc


def matmul(a, w, n_cols=None, residual=None):
    b, s, k = a.shape
    n = w.shape[1] if n_cols is None else n_cols
    tm = _pick(s, 1024)
    tn = _pick(n, 1024)
    in_specs = [
        pl.BlockSpec((None, tm, k), lambda j, bi, i: (bi, i, 0)),
        pl.BlockSpec((k, tn), lambda j, bi, i: (0, j)),
    ]
    args = [a, w]
    kern = _mm_kernel
    if residual is not None:
        x, gate = residual
        in_specs += [
            pl.BlockSpec((None, tm, tn), lambda j, bi, i: (bi, i, j)),
            pl.BlockSpec((None, 1, tn), lambda j, bi, i: (bi, 0, j)),
        ]
        args += [x, gate.reshape(b, 1, n)]
        kern = _mm_res_kernel
    return pl.pallas_call(
        kern,
        grid=(n // tn, b, s // tm),
        in_specs=in_specs,
        out_specs=pl.BlockSpec((None, tm, tn), lambda j, bi, i: (bi, i, j)),
        out_shape=jax.ShapeDtypeStruct((b, s, n), F32),
        compiler_params=_cp("parallel", "parallel", "arbitrary"),
        name="proj_res" if residual is not None else "proj",
    )(*args)


def _gates_kernel(x_ref, wf_ref, wi_ref, wft_ref, wit_ref, bfc_ref, bic_ref, bfr_ref, bir_ref,
                  gc_ref, gr_ref, *, heads):
    x = x_ref[...]
    ln = x.shape[0]
    r_i = lax.broadcasted_iota(jnp.int32, (ln, ln), 0)
    c_i = lax.broadcasted_iota(jnp.int32, (ln, ln), 1)
    tril = (c_i <= r_i).astype(F32)
    triu = (c_i >= r_i).astype(F32)

    gf = jnp.dot(x, wf_ref[...], preferred_element_type=F32) + bfc_ref[...]
    gi = jnp.dot(x, wi_ref[...], preferred_element_type=F32) + bic_ref[...]
    ls = _log_sigmoid(gf)
    pre = jnp.dot(tril, ls, precision=HIGHEST, preferred_element_type=F32)
    suf = jnp.dot(triu, ls, precision=HIGHEST, preferred_element_type=F32)
    lane = lax.broadcasted_iota(jnp.int32, gf.shape, 1)
    cum = jnp.where(lane < 2 * heads, pre, suf)
    is_a = (lane < heads) | ((lane >= 2 * heads) & (lane < 3 * heads))
    gc_ref[...] = jnp.where(is_a, cum, gi - cum)

    dn = (((1,), (1,)), ((), ()))
    gft = lax.dot_general(wft_ref[...], x, dn, preferred_element_type=F32) + bfr_ref[...]
    git = lax.dot_general(wit_ref[...], x, dn, preferred_element_type=F32) + bir_ref[...]
    lst = _log_sigmoid(gft)
    pre_t = jnp.dot(lst, triu, precision=HIGHEST, preferred_element_type=F32)
    suf_t = jnp.dot(lst, tril, precision=HIGHEST, preferred_element_type=F32)
    row = lax.broadcasted_iota(jnp.int32, gft.shape, 0)
    cum_t = jnp.where(row < 2 * heads, pre_t, suf_t)
    is_a_t = (row < heads) | ((row >= 2 * heads) & (row < 3 * heads))
    gr_ref[...] = jnp.where(is_a_t, cum_t, git - cum_t)


def gate_prep(xn, w_gates, gate_b):
    b, s, d = xn.shape
    heads = gate_b.shape[1]
    nrow = 4 * heads
    ln = SCAN_CHUNK
    wi_f, wf_f, wi_b, wf_b = [w_gates[:, g * heads:(g + 1) * heads] for g in range(4)]
    bi_f, bf_f, bi_b, bf_b = [gate_b[g] for g in range(4)]
    zw = jnp.zeros_like(wi_f)
    zb = jnp.zeros_like(bi_f)
    w_f = jnp.concatenate([wf_f, wf_f, wf_b, wf_b], axis=1)
    w_i = jnp.concatenate([zw, wi_f, zw, wi_b], axis=1)
    b_f = jnp.concatenate([bf_f, bf_f, bf_b, bf_b])
    b_i = jnp.concatenate([zb, bi_f, zb, bi_b])
    padc = LANE - nrow
    w_f_c = jnp.pad(w_f, ((0, 0), (0, padc))).astype(BF16)
    w_i_c = jnp.pad(w_i, ((0, 0), (0, padc))).astype(BF16)
    b_f_c = jnp.pad(b_f, (0, padc)).reshape(1, LANE)
    b_i_c = jnp.pad(b_i, (0, padc)).reshape(1, LANE)
    w_f_r = w_f.T.astype(BF16)
    w_i_r = w_i.T.astype(BF16)
    b_f_r = b_f.reshape(nrow, 1)
    b_i_r = b_i.reshape(nrow, 1)
    full = lambda shp: pl.BlockSpec(shp, lambda bi, i: (0, 0))
    return pl.pallas_call(
        functools.partial(_gates_kernel, heads=heads),
        grid=(b, s // ln),
        in_specs=[
            pl.BlockSpec((None, ln, d), lambda bi, i: (bi, i, 0)),
            full((d, LANE)), full((d, LANE)), full((nrow, d)), full((nrow, d)),
            full((1, LANE)), full((1, LANE)), full((nrow, 1)), full((nrow, 1)),
        ],
        out_specs=[
            pl.BlockSpec((None, ln, LANE), lambda bi, i: (bi, i, 0)),
            pl.BlockSpec((None, nrow, ln), lambda bi, i: (bi, 0, i)),
        ],
        out_shape=[
            jax.ShapeDtypeStruct((b, s, LANE), F32),
            jax.ShapeDtypeStruct((b, nrow, s), F32),
        ],
        compiler_params=_cp("parallel", "parallel"),
        name="gate_prep",
    )(xn, w_f_c, w_i_c, w_f_r, w_i_r, b_f_c, b_i_c, b_f_r, b_i_r)


def _rope(t, cos, sin):
    lane = lax.broadcasted_iota(jnp.int32, t.shape, 1)
    quarter = t.shape[1] // 4
    partner = jnp.where((lane & (2 * quarter - 1)) < quarter,
                        pltpu.roll(t, t.shape[1] - quarter, axis=1),
                        pltpu.roll(t, quarter, axis=1))
    return t * cos + partner * sin


def _mlstm_dir(q, k, v, a_col, r_col, r_row, b_last, mask, c_ref, m_ref, h_ref):
    ln = q.shape[0]
    dv = v.shape[1]
    m = m_ref[...][:, 0:1]
    v_ext = jnp.concatenate([v, jnp.ones((ln, LANE), F32)], axis=1).astype(BF16)
    dlog = jnp.where(mask, a_col + r_row, NEG)
    m_row = jnp.maximum(a_col + m, jnp.max(dlog, axis=1, keepdims=True))
    w_inter = jnp.exp(a_col + m - m_row)
    dmat = jnp.exp(dlog - m_row)
    s = lax.dot_general(q, k.astype(BF16), (((1,), (1,)), ((), ())), preferred_element_type=F32)
    sc = (s * dmat).astype(BF16)
    c_ext = c_ref[...]
    num = w_inter * jnp.dot(q, c_ext.astype(BF16), preferred_element_type=F32)
    num = num + jnp.dot(sc, v_ext, preferred_element_type=F32)
    den = num[:, dv:]
    denom = jnp.maximum(jnp.abs(den), jnp.exp(-m_row))
    inv = 1.0 / denom
    h_ref[...] = num[:, :dv] * jnp.concatenate([inv] * (dv // LANE), axis=1)
    r_max = jnp.max(r_row, axis=1, keepdims=True)
    m_new = b_last + jnp.maximum(m, r_max)
    decay = jnp.exp(b_last + m - m_new)
    ke = (k * jnp.exp(b_last + r_col - m_new)).astype(BF16)
    upd = lax.dot_general(ke, v_ext, (((0,), (0,)), ((), ())), preferred_element_type=F32)
    c_ref[...] = decay * c_ext + upd
    m_ref[...] = jnp.broadcast_to(m_new, m_ref.shape)


def _mlstm_kernel(qf_ref, kf_ref, vf_ref, gcf_ref, grf_ref, qb_ref, kb_ref, vb_ref, gcb_ref, grb_ref,
                  cosf_ref, sinf_ref, cosb_ref, sinb_ref, c0f_ref, m0f_ref, c0b_ref, m0b_ref,
                  hf_ref, hb_ref, cff_ref, mff_ref, cfb_ref, mfb_ref,
                  cf_s, mf_s, cb_s, mb_s, *, heads, hpb, use_rope):
    hg = pl.program_id(1)
    i = pl.program_id(2)

    @pl.when(i == 0)
    def _():
        cf_s[...] = c0f_ref[...]
        mf_s[...] = m0f_ref[...]
        cb_s[...] = c0b_ref[...]
        mb_s[...] = m0b_ref[...]

    ln = qf_ref.shape[0]
    r_i = lax.broadcasted_iota(jnp.int32, (ln, ln), 0)
    c_i = lax.broadcasted_iota(jnp.int32, (ln, ln), 1)
    lane = lax.broadcasted_iota(jnp.int32, (ln, LANE), 1)
    rowi = lax.broadcasted_iota(jnp.int32, (4 * heads, ln), 0)
    scale = A_DK ** -0.5

    def col(g_ref, idx):
        return jnp.sum(jnp.where(lane == idx, g_ref[...], 0.0), axis=1, keepdims=True)

    def rowv(g_ref, idx):
        return jnp.sum(jnp.where(rowi == idx, g_ref[...], 0.0), axis=0, keepdims=True)

    def prep(q_ref, k_ref, cos_ref, sin_ref, hh):
        q = q_ref[:, hh * A_DK:(hh + 1) * A_DK]
        k = k_ref[:, hh * A_DK:(hh + 1) * A_DK]
        if use_rope:
            q = _rope(q, cos_ref[...], sin_ref[...])
            k = _rope(k, cos_ref[...], sin_ref[...])
        return (q * scale).astype(BF16), k

    for hh in range(hpb):
        h = hg * hpb + hh
        vs = slice(hh * A_DV, (hh + 1) * A_DV)
        q, k = prep(qf_ref, kf_ref, cosf_ref, sinf_ref, hh)
        a_col = col(gcf_ref, h)
        r_col = col(gcf_ref, heads + h)
        r_row = rowv(grf_ref, heads + h)
        b_last = a_col[ln - 1:ln, :]
        _mlstm_dir(q, k, vf_ref[:, vs], a_col, r_col, r_row, b_last, c_i <= r_i,
                   cf_s.at[hh], mf_s.at[hh], hf_ref.at[:, vs])
        q, k = prep(qb_ref, kb_ref, cosb_ref, sinb_ref, hh)
        a_col = col(gcb_ref, 2 * heads + h)
        r_col = col(gcb_ref, 3 * heads + h)
        r_row = rowv(grb_ref, 3 * heads + h)
        b_last = a_col[0:1, :]
        _mlstm_dir(q, k, vb_ref[:, vs], a_col, r_col, r_row, b_last, c_i >= r_i,
                   cb_s.at[hh], mb_s.at[hh], hb_ref.at[:, vs])

    cff_ref[...] = cf_s[...]
    mff_ref[...] = mf_s[...]
    cfb_ref[...] = cb_s[...]
    mfb_ref[...] = mb_s[...]


def mlstm(p, offs, gc, gr, rope_tabs, state_f, state_b):
    b, s, _ = p.shape
    heads = gr.shape[1] // 4
    ln = SCAN_CHUNK
    nc = s // ln
    oq, ok, ov = [o // A_DK for o in offs[:2]] + [offs[2] // A_DV]
    hpb = 2 if all(n % 2 == 0 for n in (heads, oq, ok, ov)) else 1
    use_rope = rope_tabs is not None
    if not use_rope:
        rope_tabs = (jnp.zeros((s, A_DK), F32),) * 2
    cos, sin = rope_tabs

    fwd = lambda i: i
    bwd = lambda i: nc - 1 - i

    def specs(ix):
        return [
            pl.BlockSpec((None, ln, hpb * A_DK), lambda bi, h, i: (bi, ix(i), oq // hpb + h)),
            pl.BlockSpec((None, ln, hpb * A_DK), lambda bi, h, i: (bi, ix(i), ok // hpb + h)),
            pl.BlockSpec((None, ln, hpb * A_DV), lambda bi, h, i: (bi, ix(i), ov // hpb + h)),
            pl.BlockSpec((None, ln, LANE), lambda bi, h, i: (bi, ix(i), 0)),
            pl.BlockSpec((None, 4 * heads, ln), lambda bi, h, i: (bi, 0, ix(i))),
        ]

    def tab(ix):
        return pl.BlockSpec((ln, A_DK), lambda bi, h, i: (ix(i), 0))

    dce = A_DV + LANE
    c_spec = pl.BlockSpec((None, hpb, A_DK, dce), lambda bi, h, i: (bi, h, 0, 0))
    m_spec = pl.BlockSpec((None, hpb, 1, LANE), lambda bi, h, i: (bi, h, 0, 0))
    outs = pl.pallas_call(
        functools.partial(_mlstm_kernel, heads=heads, hpb=hpb, use_rope=use_rope),
        grid=(b, heads // hpb, nc),
        in_specs=specs(fwd) + specs(bwd) + [tab(fwd), tab(fwd), tab(bwd), tab(bwd),
                                            c_spec, m_spec, c_spec, m_spec],
        out_specs=[
            pl.BlockSpec((None, ln, hpb * A_DV), lambda bi, h, i: (bi, i, h)),
            pl.BlockSpec((None, ln, hpb * A_DV), lambda bi, h, i: (bi, nc - 1 - i, h)),
            c_spec, m_spec, c_spec, m_spec,
        ],
        out_shape=[
            jax.ShapeDtypeStruct((b, s, heads * A_DV), F32),
            jax.ShapeDtypeStruct((b, s, heads * A_DV), F32),
            jax.ShapeDtypeStruct((b, heads, A_DK, dce), F32),
            jax.ShapeDtypeStruct((b, heads, 1, LANE), F32),
            jax.ShapeDtypeStruct((b, heads, A_DK, dce), F32),
            jax.ShapeDtypeStruct((b, heads, 1, LANE), F32),
        ],
        scratch_shapes=[
            pltpu.VMEM((hpb, A_DK, dce), F32), pltpu.VMEM((hpb, 1, LANE), F32),
            pltpu.VMEM((hpb, A_DK, dce), F32), pltpu.VMEM((hpb, 1, LANE), F32),
        ],
        compiler_params=_cp("parallel", "parallel", "arbitrary"),
        name="mlstm",
    )(p, p, p, gc, gr, p, p, p, gc, gr, cos, sin, cos, sin,
      state_f[0], state_f[1], state_b[0], state_b[1])
    h_f, h_b, cf, mf, cb, mb = outs
    return h_f, h_b, (cf, mf), (cb, mb)


def _mlstm_out_kernel(hf_ref, hb_ref, o_ref, g_ref, y_ref):
    hs = hf_ref[...] + hb_ref[...]
    hs = hs * lax.rsqrt(jnp.mean(hs * hs, axis=-1, keepdims=True) + EPS) * g_ref[...]
    y_ref[...] = (_sigmoid(o_ref[...]) * hs).astype(y_ref.dtype)


def mlstm_out(h_f, h_b, p, off_o, hnorm_g):
    b, s, w = h_f.shape
    heads = w // A_DV
    tm = _pick(s, 1024)
    oo = off_o // A_DV
    blk = lambda off: pl.BlockSpec((None, tm, A_DV), lambda bi, i, h: (bi, i, off + h))
    return pl.pallas_call(
        _mlstm_out_kernel,
        grid=(b, s // tm, heads),
        in_specs=[blk(0), blk(0), blk(oo), pl.BlockSpec((1, A_DV), lambda bi, i, h: (0, h))],
        out_specs=blk(0),
        out_shape=jax.ShapeDtypeStruct((b, s, w), BF16),
        compiler_params=_cp("parallel", "parallel", "parallel"),
        name="mlstm_out",
    )(h_f, h_b, p, hnorm_g.reshape(1, w))


def _conv_kernel(bb_ref, bc_ref, bx_ref, pc_ref, px_ref, nc_ref, nx_ref, w_ref, y_ref):
    i = pl.program_id(1)
    last = pl.num_programs(1) - 1
    u = bc_ref[...] * bx_ref[...]
    tm = u.shape[0]
    hr = pc_ref.shape[0]
    u_prev = pc_ref[...][hr - 1:hr, :] * px_ref[...][hr - 1:hr, :]
    u_next = nc_ref[...][0:1, :] * nx_ref[...][0:1, :]
    u_prev = jnp.where(i == 0, 0.0, u_prev)
    u_next = jnp.where(i == last, 0.0, u_next)
    row = lax.broadcasted_iota(jnp.int32, u.shape, 0)
    dn = jnp.where(row == 0, u_prev, pltpu.roll(u, 1, axis=0))
    up = jnp.where(row == tm - 1, u_next, pltpu.roll(u, tm - 1, axis=0))
    w = w_ref[...]
    y = dn * w[0:1, :] + u * w[1:2, :] + up * w[2:3, :]
    y_ref[...] = (bb_ref[...] * y).astype(y_ref.dtype)


def short_conv(p, off_b, off_c, off_x, w_conv):
    b, s, _ = p.shape
    kw, width = w_conv.shape
    tn = _pick(width, 512)
    tm = _pick(s, 512)
    hr = 8
    nhb = s // hr
    ob, oc, ox = off_b // tn, off_c // tn, off_x // tn
    main = lambda o: pl.BlockSpec((None, tm, tn), lambda bi, i, j: (bi, i, o + j))
    prev = lambda o: pl.BlockSpec(
        (None, hr, tn), lambda bi, i, j: (bi, jnp.maximum(i * (tm // hr) - 1, 0), o + j))
    nxt = lambda o: pl.BlockSpec(
        (None, hr, tn), lambda bi, i, j: (bi, jnp.minimum((i + 1) * (tm // hr), nhb - 1), o + j))
    return pl.pallas_call(
        _conv_kernel,
        grid=(b, s // tm, width // tn),
        in_specs=[main(ob), main(oc), main(ox), prev(oc), prev(ox), nxt(oc), nxt(ox),
                  pl.BlockSpec((kw, tn), lambda bi, i, j: (0, j))],
        out_specs=pl.BlockSpec((None, tm, tn), lambda bi, i, j: (bi, i, j)),
        out_shape=jax.ShapeDtypeStruct((b, s, width), BF16),
        compiler_params=_cp("parallel", "parallel", "parallel"),
        name="short_conv",
    )(p, p, p, p, p, p, p, w_conv)


def _rms(t, g):
    return t * lax.rsqrt(jnp.mean(t * t, axis=-1, keepdims=True) + EPS) * g


def _natten_kernel(q_ref, k_ref, v_ref, kc_ref, vc_ref, tab_ref, gq_ref, gk_ref, o_ref, *, rows, win_r):
    j = pl.program_id(2)
    nk = NA_KROWS * GRID_W
    w0 = jnp.clip(NA_QROWS * j - (NA_KROWS - NA_QROWS) // 2, 0, rows - NA_KROWS)
    start = pl.multiple_of(w0 * GRID_W, GRID_W * 4)
    gk = gk_ref[...]
    kw = _rms(k_ref[pl.ds(start, nk), :], gk).astype(BF16)
    vw = v_ref[pl.ds(start, nk), :].astype(BF16)
    kc = _rms(kc_ref[...], gk).astype(BF16)
    vc = vc_ref[...].astype(BF16)

    n_tab = tab_ref.shape[0]
    odd = (lax.broadcasted_iota(jnp.int32, (GRID_W, 2 * GRID_W), 1) >= GRID_W).astype(jnp.int32)
    off = w0 - NA_QROWS * j
    dn = (((1,), (1,)), ((), ()))
    for part in range(NA_QPARTS):
        qrows = range(part * (NA_QROWS // NA_QPARTS), (part + 1) * (NA_QROWS // NA_QPARTS))
        rs = slice(qrows[0] * GRID_W, (qrows[-1] + 1) * GRID_W)
        qn = (_rms(q_ref[rs, :], gq_ref[...]) * (C_DH ** -0.5)).astype(BF16)
        bias_rows = []
        for qr in qrows:
            lo = jnp.clip(NA_QROWS * j + qr - win_r // 2, 0, rows - win_r) - w0
            tiles = []
            for kp in range(NA_KROWS // 2):
                ke = 2 * kp
                idx = jnp.clip(off + ke - qr + win_r, 0, n_tab - 1)
                krow = odd + ke
                visible = (krow >= lo) & (krow < lo + win_r)
                tiles.append(jnp.where(visible, tab_ref[idx], NEG))
            bias_rows.append(jnp.concatenate(tiles, axis=1))
        bias = jnp.concatenate(bias_rows, axis=0)
        s_loc = lax.dot_general(qn, kw, dn, preferred_element_type=F32) + bias
        s_ctx = lax.dot_general(qn, kc, dn, preferred_element_type=F32)
        m = jnp.maximum(jnp.max(s_loc, axis=1, keepdims=True), jnp.max(s_ctx, axis=1, keepdims=True))
        p_loc = jnp.exp(s_loc - m)
        p_ctx = jnp.exp(s_ctx - m)
        l = jnp.sum(p_loc, axis=1, keepdims=True) + jnp.sum(p_ctx, axis=1, keepdims=True)
        o = jnp.dot(p_loc.astype(BF16), vw, preferred_element_type=F32)
        o = o + jnp.dot(p_ctx.astype(BF16), vc, preferred_element_type=F32)
        o_ref[rs, :] = (o / l).astype(o_ref.dtype)


def _natten_tiles(rpb):
    heads, nr, ncb = rpb.shape
    win_c = (ncb + 1) // 2
    qc = np.arange(GRID_W)[:, None]
    kc = np.arange(GRID_W)[None, :]
    col0 = np.clip(qc - win_c // 2, 0, GRID_W - win_c)
    col_ok = (kc >= col0) & (kc < col0 + win_c)
    dc = np.clip(kc - qc + win_c - 1, 0, ncb - 1)
    hot_c = (dc[:, :, None] == np.arange(ncb)).astype(np.float32)
    t = jnp.einsum('hrc,qkc->hrqk', rpb.astype(F32), hot_c, precision=HIGHEST)
    t = jnp.where(jnp.asarray(col_ok), t, NEG)
    neg = jnp.full((heads, 1, GRID_W, GRID_W), NEG, F32)
    t = jnp.concatenate([neg, t, neg], axis=1)
    return jnp.concatenate([t[:, :-1], t[:, 1:]], axis=-1)


def natten(p, pc, off_q, off_k, off_v, rpb, g_q, g_k):
    b, s, _ = p.shape
    ctx = pc.shape[1]
    heads, nr, _ = rpb.shape
    rows = s // GRID_W
    nblk = rows // NA_QROWS
    nq = NA_QROWS * GRID_W
    oq, ok, ov = off_q // C_DH, off_k // C_DH, off_v // C_DH
    tiles = _natten_tiles(rpb)
    return pl.pallas_call(
        functools.partial(_natten_kernel, rows=rows, win_r=(nr + 1) // 2),
        grid=(b, heads, nblk),
        in_specs=[
            pl.BlockSpec((None, nq, C_DH), lambda bi, h, j: (bi, j, oq + h)),
            pl.BlockSpec((None, s, C_DH), lambda bi, h, j: (bi, 0, ok + h)),
            pl.BlockSpec((None, s, C_DH), lambda bi, h, j: (bi, 0, ov + h)),
            pl.BlockSpec((None, ctx, C_DH), lambda bi, h, j: (bi, 0, ok + h)),
            pl.BlockSpec((None, ctx, C_DH), lambda bi, h, j: (bi, 0, ov + h)),
            pl.BlockSpec((None, nr + 1, GRID_W, 2 * GRID_W), lambda bi, h, j: (h, 0, 0, 0)),
            pl.BlockSpec((1, C_DH), lambda bi, h, j: (0, 0)),
            pl.BlockSpec((1, C_DH), lambda bi, h, j: (0, 0)),
        ],
        out_specs=pl.BlockSpec((None, nq, C_DH), lambda bi, h, j: (bi, j, h)),
        out_shape=jax.ShapeDtypeStruct((b, s, heads * C_DH), BF16),
        compiler_params=_cp("parallel", "parallel", "arbitrary"),
        name="natten",
    )(p, p, p, pc, pc, tiles, g_q.reshape(1, C_DH), g_k.reshape(1, C_DH))


def _ctx_attn_kernel(q_ref, k_ref, v_ref, gq_ref, gk_ref, o_ref):
    qn = (_rms(q_ref[...], gq_ref[...]) * (C_DH ** -0.5)).astype(BF16)
    kn = _rms(k_ref[...], gk_ref[...]).astype(BF16)
    s = lax.dot_general(qn, kn, (((1,), (1,)), ((), ())), preferred_element_type=F32)
    m = jnp.max(s, axis=1, keepdims=True)
    pr = jnp.exp(s - m)
    l = jnp.sum(pr, axis=1, keepdims=True)
    o = jnp.dot(pr.astype(BF16), v_ref[...].astype(BF16), preferred_element_type=F32)
    o_ref[...] = (o / l).astype(o_ref.dtype)


def ctx_attn(pc, off_q, off_k, off_v, heads, g_q, g_k):
    b, ctx, _ = pc.shape
    blk = lambda o: pl.BlockSpec((None, ctx, C_DH), lambda bi, h: (bi, 0, o // C_DH + h))
    gsp = pl.BlockSpec((1, C_DH), lambda bi, h: (0, 0))
    return pl.pallas_call(
        _ctx_attn_kernel,
        grid=(b, heads),
        in_specs=[blk(off_q), blk(off_k), blk(off_v), gsp, gsp],
        out_specs=blk(0),
        out_shape=jax.ShapeDtypeStruct((b, ctx, heads * C_DH), BF16),
        compiler_params=_cp("parallel", "parallel"),
        name="ctx_attn",
    )(pc, pc, pc, g_q.reshape(1, C_DH), g_k.reshape(1, C_DH))


def _merge_kernel(ya_ref, yb_ref, yc_ref, wa_ref, wb_ref, wc_ref, ga_ref, gb_ref, gc_ref, o_ref):
    y = _sigmoid(ga_ref[...]) * jnp.dot(ya_ref[...], wa_ref[...], preferred_element_type=F32)
    y = y + _sigmoid(gb_ref[...]) * jnp.dot(yb_ref[...], wb_ref[...], preferred_element_type=F32)
    y = y + _sigmoid(gc_ref[...]) * jnp.dot(yc_ref[...], wc_ref[...], preferred_element_type=F32)
    o_ref[...] = y.astype(o_ref.dtype)


def merge(ya, yb, yc, w_a, w_b, w_c, p, off_gate):
    b, s, k = ya.shape
    d = w_a.shape[1]
    tm = _pick(s, 512)
    tn = _pick(d, 512)
    og = off_gate // tn
    nb = d // tn
    act = pl.BlockSpec((None, tm, k), lambda bi, i, j: (bi, i, 0))
    wsp = pl.BlockSpec((k, tn), lambda bi, i, j: (0, j))
    gate = lambda g: pl.BlockSpec((None, tm, tn), lambda bi, i, j: (bi, i, og + g * nb + j))
    return pl.pallas_call(
        _merge_kernel,
        grid=(b, s // tm, nb),
        in_specs=[act, act, act, wsp, wsp, wsp, gate(0), gate(1), gate(2)],
        out_specs=pl.BlockSpec((None, tm, tn), lambda bi, i, j: (bi, i, j)),
        out_shape=jax.ShapeDtypeStruct((b, s, d), BF16),
        compiler_params=_cp("parallel", "parallel", "arbitrary"),
        name="merge",
    )(ya, yb, yc, w_a, w_b, w_c, p, p, p)


def _top_values(cur, k):
    n = cur.shape[0]
    idx = lax.broadcasted_iota(jnp.int32, cur.shape, 0).astype(F32)
    vals = []
    for _ in range(k):
        mx = jnp.max(cur, axis=0, keepdims=True)
        vals.append(mx)
        first = jnp.min(jnp.where(cur == mx, idx, float(n)), axis=0, keepdims=True)
        cur = jnp.where(idx == first, -jnp.inf, cur)
    return vals


SUBLANES = 8


def _sort_network(n):
    pairs = []
    p = 1
    while p < n:
        k = p
        while k >= 1:
            for j in range(k % p, n - k, 2 * k):
                for i in range(min(k, n - j - k)):
                    if (i + j) // (2 * p) == (i + j + k) // (2 * p):
                        pairs.append((i + j, i + j + k))
            k //= 2
        p *= 2
    return pairs


def _exchange(v, i, j):
    hi, lo = jnp.maximum(v[i], v[j]), jnp.minimum(v[i], v[j])
    v[i], v[j] = hi, lo


def _topk_sorted(st):
    k = PEER_TOPK
    assert st.shape[0] == k * SUBLANES
    v = [st[i * SUBLANES:(i + 1) * SUBLANES, :] for i in range(k)]
    for i, j in _sort_network(k):
        _exchange(v, i, j)
    shift = SUBLANES // 2
    while shift >= 1:
        v = [jnp.maximum(v[i], pltpu.roll(v[k - 1 - i], shift, axis=0)) for i in range(k)]
        step = k // 2
        while step >= 1:
            for i in range(k):
                if i & step == 0:
                    _exchange(v, i, i + step)
            step //= 2
        shift //= 2
    return v


def _stack_rows(vals, like):
    row = lax.broadcasted_iota(jnp.int32, like.shape, 0)
    out = jnp.full(like.shape, -jnp.inf, F32)
    for r, val in enumerate(vals):
        out = jnp.where(row == r, val, out)
    return out


def _pair_candidates(v0, v1):
    k = PEER_TOPK
    like = v0[0]
    row = lax.broadcasted_iota(jnp.int32, like.shape, 0)
    v1_rows = [_stack_rows(v1[g:g + SUBLANES], like) for g in range(0, k, SUBLANES)]
    blocks, singles = [], []
    for i in range(k):
        cnt = min(k, (k + 1) // (i + 1))
        if cnt == 1:
            singles.append(v0[i])
            continue
        for g in range(-(-cnt // SUBLANES)):
            valid = min(SUBLANES, cnt - g * SUBLANES)
            blk = v0[i] + v1_rows[g]
            blocks.append(blk if valid == SUBLANES else jnp.where(row < valid, blk, -jnp.inf))
    for g in range(0, len(singles), SUBLANES):
        blocks.append(_stack_rows(singles[g:g + SUBLANES], like) + v1[0])
    return jnp.concatenate(blocks, axis=0)


def _peer_select_kernel(q_ref, keys_ref, s0_ref, e0_ref, c_ref, e1_ref, *, heads):
    dn = (((1,), (1,)), ((), ()))
    half = keys_ref.shape[2]
    k = PEER_TOPK
    for h in range(heads):
        sides = []
        for side in range(2):
            hp = 2 * h + side
            qh = q_ref[:, hp * half:(hp + 1) * half].astype(BF16)
            st = lax.dot_general(keys_ref[hp], qh, dn, preferred_element_type=F32)
            sides.append((st, _topk_sorted(st)))
        (s0, v0), (s1, v1) = sides
        best = _top_values(_pair_candidates(v0, v1), k + 1)
        thr = 0.5 * (best[k - 1] + best[k])
        z = jnp.ones_like(thr)
        for kk in range(1, k):
            z = z + jnp.exp(best[kk] - best[0])
        inv_z = 1.0 / z
        s0_ref[h] = s0
        e0_ref[h] = jnp.where(s0 >= v0[k - 1][0:1, :], jnp.exp(s0 - v0[0][0:1, :]), 0.0) * inv_z
        c_ref[h] = thr - s1
        e1_ref[h] = jnp.where(s1 >= v1[k - 1][0:1, :], jnp.exp(s1 - v1[0][0:1, :]), 0.0)


def peer_select(q, keys):
    b, s, _ = q.shape
    heads, _, nkeys, half = keys.shape
    tt = _pick(s, 256)
    keys_b = keys.reshape(heads * 2, nkeys, half).astype(BF16)
    big = pl.BlockSpec((None, heads, nkeys, tt), lambda bi, i: (bi, 0, 0, i))
    big_shape = jax.ShapeDtypeStruct((b, heads, nkeys, s), F32)
    return pl.pallas_call(
        functools.partial(_peer_select_kernel, heads=heads),
        grid=(b, s // tt),
        in_specs=[
            pl.BlockSpec((None, tt, q.shape[2]), lambda bi, i: (bi, i, 0)),
            pl.BlockSpec((heads * 2, nkeys, half), lambda bi, i: (0, 0, 0)),
        ],
        out_specs=[big, big, big, big],
        out_shape=[big_shape] * 4,
        compiler_params=_cp("parallel", "parallel"),
        name="peer_select",
    )(q, keys_b)


def _gelu(x):
    return 0.5 * x * (1.0 + lax.erf(x * (2.0 ** -0.5)))


def _split_cols_kernel(w_ref, o_ref, g_ref, *, lo, hi):
    o_ref[:, :lo] = w_ref[:, :lo].astype(o_ref.dtype)
    o_ref[:, lo:] = w_ref[:, hi:].astype(o_ref.dtype)
    g_ref[...] = w_ref[:, lo:lo + LANE]


def split_cols_cast(w, layer, lo, hi):
    _, k, n = w.shape
    tk = _pick(k, 128)
    rest, win = pl.pallas_call(
        functools.partial(_split_cols_kernel, lo=lo, hi=hi),
        grid=(k // tk,),
        in_specs=[pl.BlockSpec((None, tk, n), lambda i: (layer, i, 0))],
        out_specs=[pl.BlockSpec((tk, n - (hi - lo)), lambda i: (i, 0)),
                   pl.BlockSpec((tk, LANE), lambda i: (i, 0))],
        out_shape=[jax.ShapeDtypeStruct((k, n - (hi - lo)), BF16),
                   jax.ShapeDtypeStruct((k, LANE), F32)],
        compiler_params=_cp("parallel"),
        name="split_cols_cast",
    )(w)
    return rest, win[:, :hi - lo]


def _cast_kernel(x_ref, o_ref):
    o_ref[...] = x_ref[...].astype(o_ref.dtype)


def cast_bf16(x, layer):
    _, n, d = x.shape
    tn = _pick(n, 1024)
    return pl.pallas_call(
        _cast_kernel,
        grid=(n // tn,),
        in_specs=[pl.BlockSpec((None, tn, d), lambda i: (layer, i, 0))],
        out_specs=pl.BlockSpec((tn, d), lambda i: (i, 0)),
        out_shape=jax.ShapeDtypeStruct((n, d), BF16),
        compiler_params=_cp("parallel"),
        name="cast_bf16",
    )(x)


def _cast_t_kernel(x_ref, o_ref):
    o_ref[...] = x_ref[...].T.astype(o_ref.dtype)


def cast_transpose(x, layer):
    _, n, d = x.shape
    tn = _pick(n, 512)
    return pl.pallas_call(
        _cast_t_kernel,
        grid=(n // tn,),
        in_specs=[pl.BlockSpec((None, tn, d), lambda i: (layer, i, 0))],
        out_specs=pl.BlockSpec((d, tn), lambda i: (0, i)),
        out_shape=jax.ShapeDtypeStruct((d, n), BF16),
        compiler_params=_cp("parallel"),
        name="cast_transpose",
    )(x)


def _peer_dense_kernel(x_ref, u_ref, vt_ref, s0_ref, e0_ref, c_ref, e1_ref, res_ref, g_ref,
                       o_ref, st_s, wg_s, acc_s, *, heads, a_blk, nchunk):
    t = pl.program_id(1)
    applied = jnp.maximum(t - 1, 0) % nchunk
    nkeys = c_ref.shape[1]
    tt = x_ref.shape[0]

    @pl.when(t == 0)
    def _():
        wg_s[...] = jnp.zeros_like(wg_s)

    @pl.when(applied == 0)
    def _():
        acc_s[...] = jnp.zeros_like(acc_s)

    pair = 2 * nkeys
    nsl = a_blk // 2
    for k in range(nsl):
        rows = slice(k * pair, (k + 1) * pair)
        st_s[k] = lax.dot_general(u_ref[rows, :], x_ref[...], (((1,), (1,)), ((), ())),
                                  preferred_element_type=F32)
        acc_s[...] += jnp.dot(vt_ref[:, rows], wg_s[rows, :], preferred_element_type=F32)
    bg = 32
    for k in range(nsl):
        for tl in range(tt // LANE):
            ls = slice(tl * LANE, (tl + 1) * LANE)
            for g in range(nkeys // bg):
                bs = slice(g * bg, (g + 1) * bg)
                w = [jnp.zeros((bg, LANE), F32) for _ in range(2)]
                for h in range(heads):
                    cv = c_ref[h, bs, ls]
                    ev = e1_ref[h, bs, ls]
                    for a2 in range(2):
                        a = 2 * k + a2
                        s0r = s0_ref[h, a:a + 1, ls]
                        e0r = e0_ref[h, a:a + 1, ls]
                        w[a2] = w[a2] + jnp.where(s0r >= cv, ev * e0r, 0.0)
                for a2 in range(2):
                    r0 = a2 * nkeys + g * bg
                    act = _gelu(st_s[k, r0:r0 + bg, ls])
                    wg_s[k * pair + r0:k * pair + r0 + bg, ls] = (w[a2] * act).astype(wg_s.dtype)

    @pl.when((applied == nchunk - 1) & (t > 0))
    def _():
        o_ref[...] = res_ref[...] + g_ref[...] * acc_s[...].T


def peer_dense(xn, u_b, vt_b, sel, res, gate):
    b, s, d = xn.shape
    s0t, e0t, ct, e1t = sel
    heads, nkeys = s0t.shape[1], s0t.shape[2]
    n_exp = u_b.shape[0]
    a_blk = 8
    ec = a_blk * nkeys
    nchunk = n_exp // ec
    tt = _pick(s, 512)
    n_pairs = (s // tt) * nchunk
    once = pl.Buffered(1)
    scored = lambda t: jnp.minimum(t, n_pairs - 1)
    applied = lambda t: jnp.maximum(t - 1, 0)
    a_sel = pl.BlockSpec((None, heads, a_blk, tt),
                         lambda bi, t: (bi, 0, scored(t) % nchunk, scored(t) // nchunk))
    b_sel = pl.BlockSpec((None, heads, nkeys, tt), lambda bi, t: (bi, 0, 0, scored(t) // nchunk),
                         pipeline_mode=once)
    return pl.pallas_call(
        functools.partial(_peer_dense_kernel, heads=heads, a_blk=a_blk, nchunk=nchunk),
        grid=(b, n_pairs + 1),
        in_specs=[pl.BlockSpec((None, tt, d), lambda bi, t: (bi, scored(t) // nchunk, 0), pipeline_mode=once),
                  pl.BlockSpec((ec, d), lambda bi, t: (scored(t) % nchunk, 0)),
                  pl.BlockSpec((d, ec), lambda bi, t: (0, applied(t) % nchunk)),
                  a_sel, a_sel, b_sel, b_sel,
                  pl.BlockSpec((None, tt, d), lambda bi, t: (bi, applied(t) // nchunk, 0), pipeline_mode=once),
                  pl.BlockSpec((None, 1, d), lambda bi, t: (bi, 0, 0))],
        out_specs=pl.BlockSpec((None, tt, d), lambda bi, t: (bi, applied(t) // nchunk, 0)),
        out_shape=jax.ShapeDtypeStruct((b, s, d), F32),
        scratch_shapes=[pltpu.VMEM((a_blk // 2, 2 * nkeys, tt), F32), pltpu.VMEM((ec, tt), BF16),
                        pltpu.VMEM((d, tt), F32)],
        compiler_params=_cp("parallel", "arbitrary"),
        name="peer_dense",
    )(xn, u_b, vt_b, s0t, e0t, ct, e1t, res, gate.reshape(b, 1, d))


def _rope_tables(s, dk):
    half = dk // 2
    nf = half // 2
    inv = ROPE_BASE ** (-jnp.arange(nf, dtype=F32) / nf)
    pos = jnp.arange(s)
    ang_r = (pos // GRID_W).astype(F32)[:, None] * inv
    ang_c = (pos % GRID_W).astype(F32)[:, None] * inv
    cos = jnp.concatenate([jnp.cos(ang_r)] * 2 + [jnp.cos(ang_c)] * 2, axis=1)
    sin = jnp.concatenate([-jnp.sin(ang_r), jnp.sin(ang_r), -jnp.sin(ang_c), jnp.sin(ang_c)], axis=1)
    return cos, sin


def _init_state(b, heads, m_value):
    return (jnp.zeros((b, heads, A_DK, A_DV + LANE), F32),
            jnp.full((b, heads, 1, LANE), m_value, F32))


def kernel(x, c, ctx, c_ctx, w_ada, b_ada, norm_g, w_in, a_gate_b, a_hnorm_g, b_conv, c_qk_g, c_rpb,
           w_a_out, w_b_out, w_c_out, w_out, peer_wq, peer_keys, peer_u, peer_v):
    bsz, seq, d = x.shape
    depth = w_ada.shape[0]
    a_heads = a_gate_b.shape[2]
    c_heads = c_rpb.shape[1]

    off_ak = 0
    off_av = off_ak + a_heads * A_DK
    gate_lo = off_av + a_heads * A_DV
    gate_hi = gate_lo + 4 * a_heads
    off_ck = gate_lo
    off_cv = off_ck + c_heads * C_DH
    kv_width = off_cv + c_heads * C_DH
    off_aq = kv_width
    off_ao = off_aq + a_heads * A_DK
    off_bb = off_ao + a_heads * A_DV
    off_bc = off_bb + d
    off_bx = off_bc + d
    off_cq = off_bx + d
    off_gate = off_cq + c_heads * C_DH

    rows = -(-(bsz + 1) // 8) * 8
    cond = jnp.concatenate([c, c_ctx[None, :], jnp.zeros((rows - bsz - 1, d), F32)], axis=0)
    mods = adaln(cond, w_ada, b_ada)

    rope_tabs = _rope_tables(seq, A_DK)
    h_ctx = ctx
    for l in range(depth):
        last = l == depth - 1
        mod_x = [mods[l, :bsz, k * d:(k + 1) * d] for k in range(6)]
        mod_c = [jnp.broadcast_to(mods[l, bsz, k * d:(k + 1) * d], (bsz, d)) for k in range(6)]
        w_proj, w_gates = split_cols_cast(w_in, l, gate_lo, gate_hi)
        g_q, g_k = c_qk_g[l, 0], c_qk_g[l, 1]

        cn = norm_mod(h_ctx, norm_g[l, 0], mod_c[0], mod_c[1])
        pc = matmul(cn, w_proj, n_cols=kv_width if last else None)
        gc_c, gr_c = gate_prep(cn, w_gates, a_gate_b[l])
        offs_c = (off_ak, off_ak, off_av) if last else (off_aq, off_ak, off_av)
        ninf = _init_state(bsz, a_heads, -jnp.inf)
        _, _, state_f, state_b = mlstm(pc, offs_c, gc_c, gr_c, None, ninf, ninf)

        xn = norm_mod(x, norm_g[l, 0], mod_x[0], mod_x[1])
        p = matmul(xn, w_proj)
        gc_x, gr_x = gate_prep(xn, w_gates, a_gate_b[l])
        h_f, h_b, _, _ = mlstm(p, (off_aq, off_ak, off_av), gc_x, gr_x, rope_tabs, state_f, state_b)
        ya = mlstm_out(h_f, h_b, p, off_ao, a_hnorm_g[l])
        yb = short_conv(p, off_bb, off_bc, off_bx, b_conv[l])
        yc = natten(p, pc, off_cq, off_ck, off_cv, c_rpb[l], g_q, g_k)
        w_a, w_b, w_c, w_o = [cast_bf16(w, l) for w in (w_a_out, w_b_out, w_c_out, w_out)]
        y = merge(ya, yb, yc, w_a, w_b, w_c, p, off_gate)
        x_new = matmul(y, w_o, residual=(x, mod_x[2]))
        xn2 = norm_mod(x_new, norm_g[l, 1], mod_x[3], mod_x[4])
        wq_b = cast_bf16(peer_wq, l)
        u_b = cast_bf16(peer_u, l)
        vt_b = cast_transpose(peer_v, l)
        sel = peer_select(matmul(xn2, wq_b), peer_keys[l])
        x_out = peer_dense(xn2, u_b, vt_b, sel, x_new, mod_x[5])

        if not last:
            zero = _init_state(bsz, a_heads, 0.0)
            hc_f, hc_b, _, _ = mlstm(pc, (off_aq, off_ak, off_av), gc_c, gr_c, None, zero, zero)
            ya_c = mlstm_out(hc_f, hc_b, pc, off_ao, a_hnorm_g[l])
            yb_c = short_conv(pc, off_bb, off_bc, off_bx, b_conv[l])
            yc_c = ctx_attn(pc, off_cq, off_ck, off_cv, c_heads, g_q, g_k)
            y_c = merge(ya_c, yb_c, yc_c, w_a, w_b, w_c, pc, off_gate)
            hc = matmul(y_c, w_o, residual=(h_ctx, mod_c[2]))
            hcn = norm_mod(hc, norm_g[l, 1], mod_c[3], mod_c[4])
            sel_c = peer_select(matmul(hcn, wq_b), peer_keys[l])
            h_ctx = peer_dense(hcn, u_b, vt_b, sel_c, hc, mod_c[5])
        x = x_out
    return x
```
